```python
import jax, jax.numpy as jnp
from jax import lax
import numpy as np

D_MODEL = 1024
BATCH = 2
SEQ = 8192
DEPTH = 2

HEAD_DIM = 64
RET_HEADS = 8
DIL_HEADS = 8
SB_HEADS = D_MODEL // HEAD_DIM
RET_WIDTH = RET_HEADS * HEAD_DIM
DIL_WIDTH = DIL_HEADS * HEAD_DIM
SB_WIDTH = SB_HEADS * HEAD_DIM
HYB_IN = 4 * RET_WIDTH + 3 * DIL_WIDTH
D_FF = 2816
BLOCK = 128
RET_CHUNK = 128
RET_ROPE_THETA = 10000.0
ROPE_THETA = 500000.0
ROPE_DIM = HEAD_DIM // 4
DILATED_PATTERNS = ((128, 1), (512, 4), (2048, 16))
NORM_EPS = 1e-6
GN_EPS = 1e-5
N_EVEN = (DEPTH + 1) // 2
N_ODD = DEPTH // 2

kernel_name = "hybrid_retention_dilated_stickbreaking_macaron"


def rms_norm(x, g):
    xf = x.astype(jnp.float32)
    y = xf * lax.rsqrt(jnp.mean(xf * xf, axis=-1, keepdims=True) + NORM_EPS)
    return (y * g.astype(jnp.float32)).astype(x.dtype)


def swiglu_ffn(x, g, w_in, w_out):
    h = rms_norm(x, g)
    gate, up = jnp.split(h @ w_in, 2, axis=-1)
    return (jax.nn.silu(gate) * up) @ w_out


def split_heads(t, n_heads):
    b, s, _ = t.shape
    return t.reshape(b, s, n_heads, HEAD_DIM).transpose(0, 2, 1, 3)


def merge_heads(t):
    b, h, s, dh = t.shape
    return t.transpose(0, 2, 1, 3).reshape(b, s, h * dh)


def rotary(x, rot_dim, theta):
    s = x.shape[-2]
    half = rot_dim // 2
    inv_freq = 1.0 / (theta ** (jnp.arange(half, dtype=jnp.float32) / half))
    ang = jnp.arange(s, dtype=jnp.float32)[:, None] * inv_freq[None, :]
    cos, sin = jnp.cos(ang), jnp.sin(ang)
    xr = x[..., :rot_dim].astype(jnp.float32)
    x1, x2 = xr[..., :half], xr[..., half:]
    rot = jnp.concatenate([x1 * cos - x2 * sin, x2 * cos + x1 * sin], axis=-1).astype(x.dtype)
    return jnp.concatenate([rot, x[..., rot_dim:]], axis=-1)


def retention_chunkwise(q, k, v):
    b, h, s, dk = q.shape
    c = RET_CHUNK
    nc = s // c
    out_dtype = v.dtype
    q, k, v = (t.astype(jnp.float32) for t in (q, k, v))
    k = k * (dk ** -0.5)
    log_g = jnp.log(1.0 - 2.0 ** (-5.0 - jnp.arange(h, dtype=jnp.float32)))
    i = jnp.arange(c, dtype=jnp.float32)
    diff = i[:, None] - i[None, :]
    decay_in = jnp.where(diff >= 0, jnp.exp(jnp.maximum(diff, 0.0)[None] * log_g[:, None, None]), 0.0)
    qc = q.reshape(b, h, nc, c, dk)
    kc = k.reshape(b, h, nc, c, dk)
    vc = v.reshape(b, h, nc, c, -1)
    scores = jnp.einsum('bhnid,bhnjd->bhnij', qc, kc) * decay_in[None, :, None]
    inner = jnp.einsum('bhnij,bhnje->bhnie', scores, vc)
    k_decay = jnp.exp((c - 1 - i)[None, :] * log_g[:, None])
    kv = jnp.einsum('bhnjd,bhnje->nbhde', kc * k_decay[None, :, None, :, None], vc)
    chunk_decay = jnp.exp(c * log_g)[None, :, None, None]

    def step(state, kv_n):
        return state * chunk_decay + kv_n, state

    _, state_prev = lax.scan(step, jnp.zeros_like(kv[0]), kv)
    q_decay = jnp.exp((i + 1)[None, :] * log_g[:, None])
    cross = jnp.einsum('bhnid,nbhde->bhnie', qc * q_decay[None, :, None, :, None], state_prev)
    return (inner + cross).reshape(b, h, s, -1).astype(out_dtype)


def head_group_norm(o, g):
    of = o.astype(jnp.float32)
    mu = jnp.mean(of, axis=-1, keepdims=True)
    var = jnp.mean(jnp.square(of - mu), axis=-1, keepdims=True)
    y = (of - mu) * lax.rsqrt(var + GN_EPS)
    y = y * g.astype(jnp.float32).reshape(o.shape[1], 1, o.shape[3])
    return y.astype(o.dtype)


def band_attention(q, k, v, n_back):
    *lead, l_len, dh = q.shape
    nb = l_len // BLOCK
    qb = q.reshape(*lead, nb, BLOCK, dh)

    def with_prev(t):
        tb = t.reshape(*lead, nb, BLOCK, dh)
        prev = jnp.concatenate([jnp.zeros_like(tb[..., :1, :, :]), tb[..., :-1, :, :]], axis=-3)
        return jnp.concatenate([prev, tb], axis=-2)

    kb, vb = with_prev(k), with_prev(v)
    s = jnp.einsum('...nqd,...nkd->...nqk', qb, kb).astype(jnp.float32) * (dh ** -0.5)
    dist = (jnp.arange(BLOCK)[:, None] + BLOCK) - jnp.arange(2 * BLOCK)[None, :]
    in_band = (dist >= 0) & (dist <= n_back)
    key_pos = jnp.arange(nb)[:, None] * BLOCK - BLOCK + jnp.arange(2 * BLOCK)[None, :]
    mask = in_band[None] & (key_pos >= 0)[:, None, :]
    s = jnp.where(mask, s, -jnp.inf)
    m = jnp.max(s, axis=-1, keepdims=True)
    p = jnp.exp(s - m)
    den = jnp.sum(p, axis=-1, keepdims=True)
    o = jnp.einsum('...nqk,...nkd->...nqd', p, vb.astype(jnp.float32)) / den
    lse = (m + jnp.log(den))[..., 0]
    return o.reshape(*lead, l_len, dh), lse.reshape(*lead, l_len)


def dilated_attention(q, k, v):
    b, h, s, dh = q.shape
    outs, lses = [], []
    for window, dil in DILATED_PATTERNS:
        mult = dil * BLOCK
        s_pad = -(-s // mult) * mult
        pad = ((0, 0), (0, 0), (0, s_pad - s), (0, 0))

        def by_residue(t):
            return jnp.pad(t, pad).reshape(b, h, s_pad // dil, dil, dh).transpose(0, 1, 3, 2, 4)

        o, lse = band_attention(by_residue(q), by_residue(k), by_residue(v), window // dil)
        outs.append(o.transpose(0, 1, 3, 2, 4).reshape(b, h, s_pad, dh)[:, :, :s])
        lses.append(lse.transpose(0, 1, 3, 2).reshape(b, h, s_pad)[:, :, :s])
    w = jax.nn.softmax(jnp.stack(lses, axis=0), axis=0)
    o = jnp.sum(w[..., None] * jnp.stack(outs, axis=0), axis=0)
    return o.astype(q.dtype)


def stick_breaking_attention(q, k, v):
    b, h, s, dh = q.shape
    nb = s // BLOCK
    kf, vf = k.astype(jnp.float32), v.astype(jnp.float32)
    key_pos = jnp.arange(s)

    def one_block(args):
        q_blk, blk = args
        z = jnp.einsum('bhqd,bhkd->bhqk', q_blk.astype(jnp.float32), kf) * (dh ** -0.5)
        q_pos = blk * BLOCK + jnp.arange(BLOCK)
        causal = key_pos[None, :] < q_pos[:, None]
        log_stay = jnp.where(causal, jax.nn.log_sigmoid(-z), 0.0)
        later = lax.cumsum(log_stay, axis=3, reverse=True) - log_stay
        a = jnp.where(causal, jnp.exp(jax.nn.log_sigmoid(z) + later), 0.0)
        return jnp.einsum('bhqk,bhkd->bhqd', a, vf)

    q_blocks = q.reshape(b, h, nb, BLOCK, dh).transpose(2, 0, 1, 3, 4)
    o = lax.map(one_block, (q_blocks, jnp.arange(nb)))
    return o.transpose(1, 2, 0, 3, 4).reshape(b, h, s, dh).astype(q.dtype)


def retention_dilated_mixer(x, norm_g, w_in, ret_gn, w_out):
    h = rms_norm(x, norm_g)
    proj = h @ w_in
    cuts = [RET_WIDTH, 2 * RET_WIDTH, 3 * RET_WIDTH, 4 * RET_WIDTH,
            4 * RET_WIDTH + DIL_WIDTH, 4 * RET_WIDTH + 2 * DIL_WIDTH]
    rq, rk, rv, rg, dq, dk, dv = jnp.split(proj, cuts, axis=-1)
    rq = rotary(split_heads(rq, RET_HEADS), HEAD_DIM, RET_ROPE_THETA)
    rk = rotary(split_heads(rk, RET_HEADS), HEAD_DIM, RET_ROPE_THETA)
    ret = retention_chunkwise(rq, rk, split_heads(rv, RET_HEADS))
    ret = merge_heads(head_group_norm(ret, ret_gn)) * jax.nn.silu(rg)
    dq = rotary(split_heads(dq, DIL_HEADS), ROPE_DIM, ROPE_THETA)
    dk = rotary(split_heads(dk, DIL_HEADS), ROPE_DIM, ROPE_THETA)
    dil = merge_heads(dilated_attention(dq, dk, split_heads(dv, DIL_HEADS)))
    return jnp.concatenate([ret, dil], axis=-1) @ w_out


def stick_breaking_mixer(x, norm_g, w_in, w_out):
    h = rms_norm(x, norm_g)
    q, k, v = jnp.split(h @ w_in, 3, axis=-1)
    o = stick_breaking_attention(split_heads(q, SB_HEADS), split_heads(k, SB_HEADS), split_heads(v, SB_HEADS))
    return merge_heads(o) @ w_out


def setup_inputs(seed: int = 0) -> dict:
    key = jax.random.key(seed)
    ks = jax.random.split(key, 16)
    f32 = jnp.float32

    def gain(k, shape):
        return 1.0 + 0.02 * jax.random.normal(k, shape, f32)

    def dense(k, shape, fan_in):
        return jax.random.normal(k, shape, f32) * (fan_in ** -0.5)

    return {
        'x': jax.random.normal(ks[0], (BATCH, SEQ, D_MODEL), f32),
        'ffn1_norm': gain(ks[1], (DEPTH, D_MODEL)),
        'ffn1_w_in': dense(ks[2], (DEPTH, D_MODEL, 2 * D_FF), D_MODEL),
        'ffn1_w_out': dense(ks[3], (DEPTH, D_FF, D_MODEL), D_FF),
        'mix_norm': gain(ks[4], (DEPTH, D_MODEL)),
        'ffn2_norm': gain(ks[5], (DEPTH, D_MODEL)),
        'ffn2_w_in': dense(ks[6], (DEPTH, D_MODEL, 2 * D_FF), D_MODEL),
        'ffn2_w_out': dense(ks[7], (DEPTH, D_FF, D_MODEL), D_FF),
        'hyb_w_in': dense(ks[8], (N_EVEN, D_MODEL, HYB_IN), D_MODEL),
        'ret_gn': gain(ks[9], (N_EVEN, RET_WIDTH)),
        'hyb_w_out': dense(ks[10], (N_EVEN, RET_WIDTH + DIL_WIDTH, D_MODEL), RET_WIDTH + DIL_WIDTH),
        'sb_w_in': dense(ks[11], (N_ODD, D_MODEL, 3 * SB_WIDTH), D_MODEL),
        'sb_w_out': dense(ks[12], (N_ODD, SB_WIDTH, D_MODEL), SB_WIDTH),
        'final_norm': gain(ks[13], (D_MODEL,)),
    }


def reference(x, ffn1_norm, ffn1_w_in, ffn1_w_out, mix_norm, ffn2_norm, ffn2_w_in, ffn2_w_out,
              hyb_w_in, ret_gn, hyb_w_out, sb_w_in, sb_w_out, final_norm):
    for layer in range(DEPTH):
        x = x + 0.5 * swiglu_ffn(x, ffn1_norm[layer], ffn1_w_in[layer], ffn1_w_out[layer])
        if layer % 2 == 0:
            e = layer // 2
            x = x + retention_dilated_mixer(x, mix_norm[layer], hyb_w_in[e], ret_gn[e], hyb_w_out[e])
        else:
            o = layer // 2
            x = x + stick_breaking_mixer(x, mix_norm[layer], sb_w_in[o], sb_w_out[o])
        x = x + 0.5 * swiglu_ffn(x, ffn2_norm[layer], ffn2_w_in[layer], ffn2_w_out[layer])
    return rms_norm(x, final_norm)
```

```python
import functools

import numpy as np
import jax
import jax.numpy as jnp
from jax import lax
from jax.experimental import pallas as pl
from jax.experimental.pallas import tpu as pltpu

F32 = jnp.float32
BF16 = jnp.bfloat16

D_MODEL = 1024
HEAD_DIM = 64
RET_HEADS = 8
DIL_HEADS = 8
SB_HEADS = 16
RET_WIDTH = RET_HEADS * HEAD_DIM
DIL_WIDTH = DIL_HEADS * HEAD_DIM
SB_WIDTH = SB_HEADS * HEAD_DIM
HYB_IN = 4 * RET_WIDTH + 3 * DIL_WIDTH
D_FF = 2816
BLOCK = 128
RET_CHUNK = 128
RET_ROPE_THETA = 10000.0
ROPE_THETA = 500000.0
ROPE_DIM = HEAD_DIM // 4
DILATIONS = (1, 4, 16)
NORM_EPS = 1e-6
GN_EPS = 1e-5

LANES = 128
PAIR = LANES // HEAD_DIM
VMEM_LIMIT = 56 * 1024 * 1024
NEG_BIG = -1e30

TOK_TILE = 512
FF_CHUNK = D_FF // 2
SB_TQ = 128
SB_TK = 128


def _cparams(n_axes):
    return pltpu.CompilerParams(
        dimension_semantics=("arbitrary",) * n_axes,
        vmem_limit_bytes=VMEM_LIMIT)


def _resident(shape):
    return pl.BlockSpec(shape, lambda *_: (0,) * len(shape),
                        pipeline_mode=pl.Buffered(1))


def _rms(x, g):
    return x * lax.rsqrt(jnp.mean(x * x, axis=-1, keepdims=True) + NORM_EPS) * g


def _silu(x):
    return x / (1.0 + jnp.exp(-x))


def _lane_is_head0(shape):
    return lax.broadcasted_iota(jnp.int32, shape, len(shape) - 1) < HEAD_DIM


def _stack_heads(q):
    h0 = _lane_is_head0(q.shape)
    zero = jnp.zeros_like(q)
    return jnp.concatenate([jnp.where(h0, q, zero), jnp.where(h0, zero, q)], axis=0)


def _unstack_heads(o2):
    t = o2.shape[0] // 2
    return jnp.where(_lane_is_head0((t, LANES)), o2[:t], o2[t:])


def _ffn_body(x_ref, g_ref, win_ref, wout_ref, fg_ref, o_ref, *, final_norm):
    x = x_ref[...]
    hb = _rms(x, g_ref[...]).astype(BF16)
    y = jnp.zeros_like(x)
    for c in range(D_FF // FF_CHUNK):
        lo = c * FF_CHUNK
        gate = jnp.dot(hb, win_ref[:, lo:lo + FF_CHUNK], preferred_element_type=F32)
        up = jnp.dot(hb, win_ref[:, D_FF + lo:D_FF + lo + FF_CHUNK], preferred_element_type=F32)
        act = (_silu(gate) * up).astype(BF16)
        y = y + jnp.dot(act, wout_ref[lo:lo + FF_CHUNK, :], preferred_element_type=F32)
    y = x + 0.5 * y
    if final_norm:
        y = _rms(y, fg_ref[...])
    o_ref[...] = y


def _ffn(x, g, w_in, w_out, final_g, final_norm):
    n = x.shape[0]
    row = pl.BlockSpec((TOK_TILE, D_MODEL), lambda i: (i, 0))
    return pl.pallas_call(
        functools.partial(_ffn_body, final_norm=final_norm),
        grid=(n // TOK_TILE,),
        in_specs=[row, _resident((1, D_MODEL)), _resident((D_MODEL, 2 * D_FF)),
                  _resident((D_FF, D_MODEL)), _resident((1, D_MODEL))],
        out_specs=row,
        out_shape=jax.ShapeDtypeStruct((n, D_MODEL), F32),
        compiler_params=_cparams(1),
        name="ffn",
    )(x, g.reshape(1, D_MODEL), w_in.astype(BF16), w_out.astype(BF16),
      final_g.reshape(1, D_MODEL))


PROJ_GROUP = 512


def _rotate(y, cos, sin_a, sin_b, half):
    return (y * cos + pltpu.roll(y, LANES - half, 1) * sin_a + pltpu.roll(y, half, 1) * sin_b)


def _proj_body(x_ref, g_ref, w_ref, *rest, plan):
    tables, o_ref = rest[:-1], rest[-1]
    hb = _rms(x_ref[...], g_ref[...]).astype(BF16)
    for gi, (kind, scale) in enumerate(plan):
        lo = gi * PROJ_GROUP
        y = jnp.dot(hb, w_ref[:, lo:lo + PROJ_GROUP], preferred_element_type=F32)
        for s in range(PROJ_GROUP // LANES):
            ys = y[:, s * LANES:(s + 1) * LANES]
            if kind == "ret_rot":
                ys = _rotate(ys, tables[0][...], tables[1][...], tables[2][...], HEAD_DIM // 2)
            elif kind == "dil_rot":
                ys = _rotate(ys, tables[3][...], tables[4][...], tables[5][...], ROPE_DIM // 2)
            if scale != 1.0:
                ys = ys * scale
            o_ref[:, lo + s * LANES:lo + (s + 1) * LANES] = ys.astype(BF16)


def _proj(x, g, w, plan, tables, seq):
    n = x.shape[0]
    width = w.shape[1]
    row = pl.BlockSpec((TOK_TILE, D_MODEL), lambda i: (i, 0))
    per_seq = seq // TOK_TILE
    tab = pl.BlockSpec((TOK_TILE, LANES), lambda i: (i % per_seq, 0))
    return pl.pallas_call(
        functools.partial(_proj_body, plan=plan),
        grid=(n // TOK_TILE,),
        in_specs=[row, _resident((1, D_MODEL)), _resident((D_MODEL, width))] + [tab] * len(tables),
        out_specs=pl.BlockSpec((TOK_TILE, width), lambda i: (i, 0)),
        out_shape=jax.ShapeDtypeStruct((n, width), BF16),
        compiler_params=_cparams(1),
        name="proj",
    )(x, g.reshape(1, D_MODEL), w.astype(BF16), *tables)


def _rotary_tables(seq, rot_dim, theta):
    half = rot_dim // 2
    lane = np.arange(LANES) % HEAD_DIM
    inv_freq = 1.0 / (theta ** (np.arange(half, dtype=np.float64) / half))
    ang = (np.arange(seq, dtype=np.float64)[:, None] * inv_freq[None, :])[:, lane % half]
    cos, sin = np.cos(ang), np.sin(ang)
    is_x1 = (lane < half)[None, :]
    is_x2 = ((lane >= half) & (lane < rot_dim))[None, :]
    cos_t = np.where(is_x1 | is_x2, cos, 1.0)
    sin_a = np.where(is_x1, -sin, 0.0)
    sin_b = np.where(is_x2, sin, 0.0)
    return tuple(jnp.asarray(t, dtype=F32) for t in (cos_t, sin_a, sin_b))


RET_TOK = 512


def _retention_consts():
    h = np.arange(RET_HEADS, dtype=np.float64)
    log_g = np.log(1.0 - 2.0 ** (-5.0 - h))
    i = np.arange(RET_CHUNK, dtype=np.float64)
    diff = i[:, None] - i[None, :]
    decay_in = np.where(diff >= 0, np.exp(np.maximum(diff, 0.0)[None] * log_g[:, None, None]), 0.0)
    n_pairs = RET_HEADS // PAIR
    decay = decay_in.reshape(n_pairs, PAIR * RET_CHUNK, RET_CHUNK)
    lane_head = np.arange(LANES) // HEAD_DIM
    head_of = (np.arange(n_pairs)[:, None] * PAIR + lane_head[None, :])
    lg = log_g[head_of]
    q_dec = np.exp((i + 1.0)[None, :, None] * lg[:, None, :])
    k_dec = np.exp((RET_CHUNK - 1.0 - i)[None, :, None] * lg[:, None, :])
    same_head = (lane_head[:, None] == lane_head[None, :])
    c_dec = np.exp(RET_CHUNK * lg)[:, :, None] * same_head[None]
    bd = np.broadcast_to(same_head[None], c_dec.shape)
    return tuple(jnp.asarray(t, dtype=F32) for t in (decay, q_dec, k_dec, c_dec, bd))


def _retention_body(q_ref, k_ref, v_ref, g_ref, dec_ref, qd_ref, kd_ref, cd_ref, bd_ref, gn_ref,
                    o_ref, state_ref):
    @pl.when(pl.program_id(1) == 0)
    def _():
        state_ref[...] = jnp.zeros_like(state_ref)

    h0 = _lane_is_head0((RET_CHUNK, LANES))
    inv = 1.0 / HEAD_DIM
    for c in range(RET_TOK // RET_CHUNK):
        rows = slice(c * RET_CHUNK, (c + 1) * RET_CHUNK)
        for p in range(RET_HEADS // PAIR):
            cols = slice(p * LANES, (p + 1) * LANES)
            q, k, v = q_ref[rows, cols], k_ref[rows, cols], v_ref[rows, cols]
            sc = lax.dot_general(_stack_heads(q), k, (((1,), (1,)), ((), ())),
                                 preferred_element_type=F32) * dec_ref[p]
            inner = _unstack_heads(jnp.dot(sc.astype(BF16), v, preferred_element_type=F32))
            state = state_ref[p]
            qd = (q.astype(F32) * qd_ref[p]).astype(BF16)
            o = inner + jnp.dot(qd, state.astype(BF16), preferred_element_type=F32)
            kd = (k.astype(F32) * kd_ref[p]).astype(BF16)
            kv = lax.dot_general(kd, v, (((0,), (0,)), ((), ())), preferred_element_type=F32)
            state_ref[p] = state * cd_ref[p] + kv * bd_ref[p]
            s0 = jnp.sum(jnp.where(h0, o, 0.0), axis=-1, keepdims=True)
            s1 = jnp.sum(jnp.where(h0, 0.0, o), axis=-1, keepdims=True)
            d = o - jnp.where(h0, s0, s1) * inv
            d2 = d * d
            v0 = jnp.sum(jnp.where(h0, d2, 0.0), axis=-1, keepdims=True)
            v1 = jnp.sum(jnp.where(h0, 0.0, d2), axis=-1, keepdims=True)
            y = d * lax.rsqrt(jnp.where(h0, v0, v1) * inv + GN_EPS) * gn_ref[:, cols]
            o_ref[rows, cols] = (y * _silu(g_ref[rows, cols].astype(F32))).astype(BF16)


def _retention(proj, ret_gn, batch, seq):
    n = proj.shape[0]
    per_seq = seq // RET_TOK
    consts = _retention_consts()

    def slab(j):
        return pl.BlockSpec((RET_TOK, RET_WIDTH), lambda b, i, j=j: (b * per_seq + i, j))

    return pl.pallas_call(
        _retention_body,
        grid=(batch, per_seq),
        in_specs=[slab(0), slab(1), slab(2), slab(3)]
        + [_resident(c.shape) for c in consts] + [_resident((1, RET_WIDTH))],
        out_specs=pl.BlockSpec((RET_TOK, RET_WIDTH), lambda b, i: (b * per_seq + i, 0)),
        out_shape=jax.ShapeDtypeStruct((n, RET_WIDTH), BF16),
        scratch_shapes=[pltpu.VMEM((RET_HEADS // PAIR, LANES, LANES), F32)],
        compiler_params=_cparams(2),
        name="retention",
    )(proj, proj, proj, proj, *consts, ret_gn.reshape(1, RET_WIDTH))


BAND_ROWS = 512


def _band_body(q_ref, kc_ref, kp_ref, vc_ref, vp_ref, o_ref, lse_ref, kf_ref, vf_ref, *, per_seq):
    first = (pl.program_id(0) % per_seq) == 0
    kf_ref[0:BLOCK, :] = kp_ref[...]
    kf_ref[BLOCK:, :] = kc_ref[...]
    vf_ref[0:BLOCK, :] = vp_ref[...]
    vf_ref[BLOCK:, :] = vc_ref[...]
    qi = lax.broadcasted_iota(jnp.int32, (PAIR * BLOCK, 2 * BLOCK), 0) % BLOCK
    kj = lax.broadcasted_iota(jnp.int32, (PAIR * BLOCK, 2 * BLOCK), 1)
    band = (kj >= qi) & (kj <= qi + BLOCK)
    k_min = jnp.where(first, BLOCK, 0)
    band_first = band & (kj >= k_min)
    for n in range(BAND_ROWS // BLOCK):
        rows = slice(n * BLOCK, (n + 1) * BLOCK)
        keys = kf_ref[n * BLOCK:(n + 2) * BLOCK, :]
        vals = vf_ref[n * BLOCK:(n + 2) * BLOCK, :]
        s = lax.dot_general(_stack_heads(q_ref[rows, :]), keys, (((1,), (1,)), ((), ())),
                            preferred_element_type=F32)
        mask = band_first if n == 0 else band
        s = jnp.where(mask, s, NEG_BIG)
        m = jnp.max(s, axis=-1, keepdims=True)
        p = jnp.exp(s - m)
        den = jnp.sum(p, axis=-1, keepdims=True)
        o2 = jnp.dot(p.astype(BF16), vals, preferred_element_type=F32) / den
        lse2 = jnp.broadcast_to(m + jnp.log(den), (PAIR * BLOCK, LANES))
        o_ref[rows, :] = _unstack_heads(o2).astype(BF16)
        lse_ref[rows, :] = _unstack_heads(lse2)


def _band_attention(proj, dil, seq):
    n, width = proj.shape
    view = proj.reshape(n // dil, dil * width)
    cb = width // LANES
    q0, k0, v0 = ((4 * RET_WIDTH + t * DIL_WIDTH) // LANES for t in range(3))
    sub = BAND_ROWS // BLOCK
    per_seq = seq // dil // BAND_ROWS
    n_pairs = DIL_HEADS // PAIR

    def cur(c0):
        return pl.BlockSpec((BAND_ROWS, LANES), lambda i, r, p: (i, r * cb + c0 + p))

    def prev(c0):
        return pl.BlockSpec((BLOCK, LANES),
                            lambda i, r, p: (jnp.maximum(i * sub - 1, 0), r * cb + c0 + p))

    out = pl.BlockSpec((BAND_ROWS, LANES), lambda i, r, p: (i, r * n_pairs + p))
    o, lse = pl.pallas_call(
        functools.partial(_band_body, per_seq=per_seq),
        grid=(n // dil // BAND_ROWS, dil, n_pairs),
        in_specs=[cur(q0), cur(k0), prev(k0), cur(v0), prev(v0)],
        out_specs=[out, out],
        out_shape=[jax.ShapeDtypeStruct((n // dil, dil * DIL_WIDTH), BF16),
                   jax.ShapeDtypeStruct((n // dil, dil * DIL_WIDTH), F32)],
        scratch_shapes=[pltpu.VMEM((BAND_ROWS + BLOCK, LANES), BF16),
                        pltpu.VMEM((BAND_ROWS + BLOCK, LANES), BF16)],
        compiler_params=_cparams(3),
        name=f"band_d{dil}",
    )(view, view, view, view, view)
    return o.reshape(n, DIL_WIDTH), lse.reshape(n, DIL_WIDTH)


def _hyb_out_body(x_ref, ret_ref, o1_ref, o2_ref, o3_ref, l1_ref, l2_ref, l3_ref, w_ref, y_ref):
    l1, l2, l3 = l1_ref[...], l2_ref[...], l3_ref[...]
    m = jnp.maximum(jnp.maximum(l1, l2), l3)
    e1, e2, e3 = jnp.exp(l1 - m), jnp.exp(l2 - m), jnp.exp(l3 - m)
    dil = (e1 * o1_ref[...].astype(F32) + e2 * o2_ref[...].astype(F32)
           + e3 * o3_ref[...].astype(F32)) / (e1 + e2 + e3)
    y = jnp.dot(ret_ref[...], w_ref[0:RET_WIDTH, :], preferred_element_type=F32)
    y = y + jnp.dot(dil.astype(BF16), w_ref[RET_WIDTH:, :], preferred_element_type=F32)
    y_ref[...] = x_ref[...] + y


def _hyb_out(x, ret, outs, lses, w):
    n = x.shape[0]
    row = pl.BlockSpec((TOK_TILE, D_MODEL), lambda i: (i, 0))
    half = pl.BlockSpec((TOK_TILE, DIL_WIDTH), lambda i: (i, 0))
    return pl.pallas_call(
        _hyb_out_body,
        grid=(n // TOK_TILE,),
        in_specs=[row] + [half] * 7 + [_resident((RET_WIDTH + DIL_WIDTH, D_MODEL))],
        out_specs=row,
        out_shape=jax.ShapeDtypeStruct((n, D_MODEL), F32),
        compiler_params=_cparams(1),
        name="hyb_out",
    )(x, ret, *outs, *lses, w.astype(BF16))


def _out_body(x_ref, a_ref, w_ref, y_ref):
    y_ref[...] = x_ref[...] + jnp.dot(a_ref[...], w_ref[...], preferred_element_type=F32)


def _out_proj(x, a, w):
    n = x.shape[0]
    row = pl.BlockSpec((TOK_TILE, D_MODEL), lambda i: (i, 0))
    return pl.pallas_call(
        _out_body,
        grid=(n // TOK_TILE,),
        in_specs=[row, pl.BlockSpec((TOK_TILE, a.shape[1]), lambda i: (i, 0)),
                  _resident(w.shape)],
        out_specs=row,
        out_shape=jax.ShapeDtypeStruct((n, D_MODEL), F32),
        compiler_params=_cparams(1),
        name="out_proj",
    )(x, a, w.astype(BF16))


def _suffix_matrix():
    j = np.arange(SB_TK)[:, None]
    s = np.arange(2 * SB_TK)[None, :]
    return jnp.asarray(np.where((s >= SB_TK) | (j > s), 1.0, 0.0), dtype=BF16)


def _sb_body(q_ref, k_ref, v_ref, tri_ref, o_ref, acc_ref, carry_ref):
    qi = pl.program_id(2)
    qs = _stack_heads(q_ref[...])
    rows = PAIR * SB_TQ
    q_pos = qi * SB_TQ + (lax.broadcasted_iota(jnp.int32, (rows, SB_TK), 0) & (SB_TQ - 1))
    k_off = lax.broadcasted_iota(jnp.int32, (rows, SB_TK), 1)
    tri = tri_ref[...]
    acc_ref[...] = jnp.zeros_like(acc_ref)
    carry_ref[...] = jnp.zeros_like(carry_ref)

    def step(t, _):
        carry = carry_ref[...]
        kb = qi - t
        start = pl.multiple_of(kb * SB_TK, SB_TK)
        kblk = k_ref[pl.ds(start, SB_TK), :]
        vblk = v_ref[pl.ds(start, SB_TK), :]
        z = lax.dot_general(qs, kblk, (((1,), (1,)), ((), ())), preferred_element_type=F32)
        causal = (start + k_off) < q_pos
        softplus = jnp.maximum(z, 0.0) + jnp.log(1.0 + jnp.exp(-jnp.abs(z)))
        log_stay = jnp.where(causal, -softplus, 0.0)
        hi = log_stay.astype(BF16)
        lo = (log_stay - hi.astype(F32)).astype(BF16)
        sums = (jnp.dot(hi, tri, preferred_element_type=F32)
                + jnp.dot(lo, tri, preferred_element_type=F32))
        later = sums[:, :SB_TK] + carry
        a = jnp.where(causal, jnp.exp(z + log_stay + later), 0.0)
        acc_ref[...] += jnp.dot(a.astype(BF16), vblk, preferred_element_type=F32)
        carry_ref[...] = carry + sums[:, SB_TK:]
        return 0

    lax.fori_loop(0, qi + 1, step, 0)
    o_ref[...] = _unstack_heads(acc_ref[...]).astype(BF16)


def _stick_breaking(proj, batch, seq):
    n = proj.shape[0]
    n_pairs = SB_HEADS // PAIR
    per_seq = seq // SB_TQ
    qo = lambda b, p, i: (b * per_seq + i, p)
    return pl.pallas_call(
        _sb_body,
        grid=(batch, n_pairs, per_seq),
        in_specs=[pl.BlockSpec((SB_TQ, LANES), qo),
                  pl.BlockSpec((seq, LANES), lambda b, p, i: (b, n_pairs + p)),
                  pl.BlockSpec((seq, LANES), lambda b, p, i: (b, 2 * n_pairs + p)),
                  _resident((SB_TK, 2 * SB_TK))],
        out_specs=pl.BlockSpec((SB_TQ, LANES), qo),
        out_shape=jax.ShapeDtypeStruct((n, SB_WIDTH), BF16),
        scratch_shapes=[pltpu.VMEM((PAIR * SB_TQ, LANES), F32),
                        pltpu.VMEM((PAIR * SB_TQ, LANES), F32)],
        compiler_params=_cparams(3),
        name="stick_breaking",
    )(proj, proj, proj, _suffix_matrix())


HYB_PLAN = (("ret_rot", 1.0), ("ret_rot", HEAD_DIM ** -0.5), ("plain", 1.0), ("plain", 1.0),
            ("dil_rot", HEAD_DIM ** -0.5), ("dil_rot", 1.0), ("plain", 1.0))
SB_PLAN = (("plain", HEAD_DIM ** -0.5),) * 2 + (("plain", 1.0),) * 4


def kernel(x, ffn1_norm, ffn1_w_in, ffn1_w_out, mix_norm, ffn2_norm, ffn2_w_in, ffn2_w_out,
           hyb_w_in, ret_gn, hyb_w_out, sb_w_in, sb_w_out, final_norm):
    batch, seq, _ = x.shape
    depth = ffn1_norm.shape[0]
    tables = (_rotary_tables(seq, HEAD_DIM, RET_ROPE_THETA)
              + _rotary_tables(seq, ROPE_DIM, ROPE_THETA))
    h = x.reshape(batch * seq, D_MODEL)
    for layer in range(depth):
        h = _ffn(h, ffn1_norm[layer], ffn1_w_in[layer], ffn1_w_out[layer], final_norm, False)
        if layer % 2 == 0:
            e = layer // 2
            proj = _proj(h, mix_norm[layer], hyb_w_in[e], HYB_PLAN, tables, seq)
            ret = _retention(proj, ret_gn[e], batch, seq)
            bands = [_band_attention(proj, d, seq) for d in DILATIONS]
            h = _hyb_out(h, ret, [o for o, _ in bands], [l for _, l in bands], hyb_w_out[e])
        else:
            o = layer // 2
            proj = _proj(h, mix_norm[layer], sb_w_in[o], SB_PLAN, (), seq)
            h = _out_proj(h, _stick_breaking(proj, batch, seq), sb_w_out[o])
        h = _ffn(h, ffn2_norm[layer], ffn2_w_in[layer], ffn2_w_out[layer], final_norm,
                 layer == depth - 1)
    return h.reshape(batch, seq, D_MODEL)
```

```python
import functools

import numpy as np
import jax
import jax.numpy as jnp
from jax import lax
from jax.experimental import pallas as pl
from jax.experimental.pallas import tpu as pltpu

F32 = jnp.float32
BF16 = jnp.bfloat16

D_MODEL = 1024
HEAD_DIM = 64
RET_HEADS = 8
DIL_HEADS = 8
SB_HEADS = 16
RET_WIDTH = RET_HEADS * HEAD_DIM
DIL_WIDTH = DIL_HEADS * HEAD_DIM
SB_WIDTH = SB_HEADS * HEAD_DIM
HYB_IN = 4 * RET_WIDTH + 3 * DIL_WIDTH
D_FF = 2816
BLOCK = 128
RET_CHUNK = 128
RET_ROPE_THETA = 10000.0
ROPE_THETA = 500000.0
ROPE_DIM = HEAD_DIM // 4
DILATIONS = (1, 4, 16)
NORM_EPS = 1e-6
GN_EPS = 1e-5

LANES = 128
PAIR = LANES // HEAD_DIM
VMEM_LIMIT = 56 * 1024 * 1024
NEG_BIG = -1e30

TOK_TILE = 512
FF_CHUNK = D_FF // 2
SB_TQ = 256
SB_TK = 256


def _cparams(n_axes):
    return pltpu.CompilerParams(
        dimension_semantics=("arbitrary",) * n_axes,
        vmem_limit_bytes=VMEM_LIMIT)


def _resident(shape):
    return pl.BlockSpec(shape, lambda *_: (0,) * len(shape),
                        pipeline_mode=pl.Buffered(1))


def _rms(x, g):
    return x * lax.rsqrt(jnp.mean(x * x, axis=-1, keepdims=True) + NORM_EPS) * g


def _silu(x):
    return x / (1.0 + jnp.exp(-x))


def _lane_is_head0(shape):
    return lax.broadcasted_iota(jnp.int32, shape, len(shape) - 1) < HEAD_DIM


def _stack_heads(q):
    h0 = _lane_is_head0(q.shape)
    zero = jnp.zeros_like(q)
    return jnp.concatenate([jnp.where(h0, q, zero), jnp.where(h0, zero, q)], axis=0)


def _unstack_heads(o2):
    t = o2.shape[0] // 2
    return jnp.where(_lane_is_head0((t, LANES)), o2[:t], o2[t:])


def _ffn_body(x_ref, g_ref, win_ref, wout_ref, fg_ref, o_ref, *, final_norm):
    x = x_ref[...]
    hb = _rms(x, g_ref[...]).astype(BF16)
    y = jnp.zeros_like(x)
    for c in range(D_FF // FF_CHUNK):
        lo = c * FF_CHUNK
        gate = jnp.dot(hb, win_ref[:, lo:lo + FF_CHUNK], preferred_element_type=F32)
        up = jnp.dot(hb, win_ref[:, D_FF + lo:D_FF + lo + FF_CHUNK], preferred_element_type=F32)
        act = (_silu(gate) * up).astype(BF16)
        y = y + jnp.dot(act, wout_ref[lo:lo + FF_CHUNK, :], preferred_element_type=F32)
    y = x + 0.5 * y
    if final_norm:
        y = _rms(y, fg_ref[...])
    o_ref[...] = y


def _ffn(x, g, w_in, w_out, final_g, final_norm):
    n = x.shape[0]
    row = pl.BlockSpec((TOK_TILE, D_MODEL), lambda i: (i, 0))
    return pl.pallas_call(
        functools.partial(_ffn_body, final_norm=final_norm),
        grid=(n // TOK_TILE,),
        in_specs=[row, _resident((1, D_MODEL)), _resident((D_MODEL, 2 * D_FF)),
                  _resident((D_FF, D_MODEL)), _resident((1, D_MODEL))],
        out_specs=row,
        out_shape=jax.ShapeDtypeStruct((n, D_MODEL), F32),
        compiler_params=_cparams(1),
        name="ffn",
    )(x, g.reshape(1, D_MODEL), w_in.astype(BF16), w_out.astype(BF16),
      final_g.reshape(1, D_MODEL))


PROJ_GROUP = 512


def _rotate(y, cos, sin_a, sin_b, half):
    return (y * cos + pltpu.roll(y, LANES - half, 1) * sin_a + pltpu.roll(y, half, 1) * sin_b)


def _proj_body(x_ref, g_ref, w_ref, *rest, plan):
    tables, o_ref = rest[:-1], rest[-1]
    hb = _rms(x_ref[...], g_ref[...]).astype(BF16)
    for gi, (kind, scale) in enumerate(plan):
        lo = gi * PROJ_GROUP
        y = jnp.dot(hb, w_ref[:, lo:lo + PROJ_GROUP], preferred_element_type=F32)
        for s in range(PROJ_GROUP // LANES):
            ys = y[:, s * LANES:(s + 1) * LANES]
            if kind == "ret_rot":
                ys = _rotate(ys, tables[0][...], tables[1][...], tables[2][...], HEAD_DIM // 2)
            elif kind == "dil_rot":
                ys = _rotate(ys, tables[3][...], tables[4][...], tables[5][...], ROPE_DIM // 2)
            if scale != 1.0:
                ys = ys * scale
            o_ref[:, lo + s * LANES:lo + (s + 1) * LANES] = ys.astype(BF16)


def _proj(x, g, w, plan, tables, seq):
    n = x.shape[0]
    width = w.shape[1]
    row = pl.BlockSpec((TOK_TILE, D_MODEL), lambda i: (i, 0))
    per_seq = seq // TOK_TILE
    tab = pl.BlockSpec((TOK_TILE, LANES), lambda i: (i % per_seq, 0))
    return pl.pallas_call(
        functools.partial(_proj_body, plan=plan),
        grid=(n // TOK_TILE,),
        in_specs=[row, _resident((1, D_MODEL)), _resident((D_MODEL, width))] + [tab] * len(tables),
        out_specs=pl.BlockSpec((TOK_TILE, width), lambda i: (i, 0)),
        out_shape=jax.ShapeDtypeStruct((n, width), BF16),
        compiler_params=_cparams(1),
        name="proj",
    )(x, g.reshape(1, D_MODEL), w.astype(BF16), *tables)


def _rotary_tables(seq, rot_dim, theta):
    half = rot_dim // 2
    lane = np.arange(LANES) % HEAD_DIM
    inv_freq = 1.0 / (theta ** (np.arange(half, dtype=np.float64) / half))
    ang = (np.arange(seq, dtype=np.float64)[:, None] * inv_freq[None, :])[:, lane % half]
    cos, sin = np.cos(ang), np.sin(ang)
    is_x1 = (lane < half)[None, :]
    is_x2 = ((lane >= half) & (lane < rot_dim))[None, :]
    cos_t = np.where(is_x1 | is_x2, cos, 1.0)
    sin_a = np.where(is_x1, -sin, 0.0)
    sin_b = np.where(is_x2, sin, 0.0)
    return tuple(jnp.asarray(t, dtype=F32) for t in (cos_t, sin_a, sin_b))


RET_TOK = 512


def _retention_consts():
    h = np.arange(RET_HEADS, dtype=np.float64)
    log_g = np.log(1.0 - 2.0 ** (-5.0 - h))
    i = np.arange(RET_CHUNK, dtype=np.float64)
    diff = i[:, None] - i[None, :]
    decay_in = np.where(diff >= 0, np.exp(np.maximum(diff, 0.0)[None] * log_g[:, None, None]), 0.0)
    n_pairs = RET_HEADS // PAIR
    decay = decay_in.reshape(n_pairs, PAIR * RET_CHUNK, RET_CHUNK)
    lane_head = np.arange(LANES) // HEAD_DIM
    head_of = (np.arange(n_pairs)[:, None] * PAIR + lane_head[None, :])
    lg = log_g[head_of]
    q_dec = np.exp((i + 1.0)[None, :, None] * lg[:, None, :])
    k_dec = np.exp((RET_CHUNK - 1.0 - i)[None, :, None] * lg[:, None, :])
    same_head = (lane_head[:, None] == lane_head[None, :])
    c_dec = np.exp(RET_CHUNK * lg)[:, :, None] * same_head[None]
    bd = np.broadcast_to(same_head[None], c_dec.shape)
    return tuple(jnp.asarray(t, dtype=F32) for t in (decay, q_dec, k_dec, c_dec, bd))


def _retention_body(q_ref, k_ref, v_ref, g_ref, dec_ref, qd_ref, kd_ref, cd_ref, bd_ref, gn_ref,
                    o_ref, state_ref):
    @pl.when(pl.program_id(1) == 0)
    def _():
        state_ref[...] = jnp.zeros_like(state_ref)

    h0 = _lane_is_head0((RET_CHUNK, LANES))
    inv = 1.0 / HEAD_DIM
    for c in range(RET_TOK // RET_CHUNK):
        rows = slice(c * RET_CHUNK, (c + 1) * RET_CHUNK)
        for p in range(RET_HEADS // PAIR):
            cols = slice(p * LANES, (p + 1) * LANES)
            q, k, v = q_ref[rows, cols], k_ref[rows, cols], v_ref[rows, cols]
            sc = lax.dot_general(_stack_heads(q), k, (((1,), (1,)), ((), ())),
                                 preferred_element_type=F32) * dec_ref[p]
            inner = _unstack_heads(jnp.dot(sc.astype(BF16), v, preferred_element_type=F32))
            state = state_ref[p]
            qd = (q.astype(F32) * qd_ref[p]).astype(BF16)
            o = inner + jnp.dot(qd, state.astype(BF16), preferred_element_type=F32)
            kd = (k.astype(F32) * kd_ref[p]).astype(BF16)
            kv = lax.dot_general(kd, v, (((0,), (0,)), ((), ())), preferred_element_type=F32)
            state_ref[p] = state * cd_ref[p] + kv * bd_ref[p]
            s0 = jnp.sum(jnp.where(h0, o, 0.0), axis=-1, keepdims=True)
            s1 = jnp.sum(jnp.where(h0, 0.0, o), axis=-1, keepdims=True)
            d = o - jnp.where(h0, s0, s1) * inv
            d2 = d * d
            v0 = jnp.sum(jnp.where(h0, d2, 0.0), axis=-1, keepdims=True)
            v1 = jnp.sum(jnp.where(h0, 0.0, d2), axis=-1, keepdims=True)
            y = d * lax.rsqrt(jnp.where(h0, v0, v1) * inv + GN_EPS) * gn_ref[:, cols]
            o_ref[rows, cols] = (y * _silu(g_ref[rows, cols].astype(F32))).astype(BF16)


def _retention(proj, ret_gn, batch, seq):
    n = proj.shape[0]
    per_seq = seq // RET_TOK
    consts = _retention_consts()

    def slab(j):
        return pl.BlockSpec((RET_TOK, RET_WIDTH), lambda b, i, j=j: (b * per_seq + i, j))

    return pl.pallas_call(
        _retention_body,
        grid=(batch, per_seq),
        in_specs=[slab(0), slab(1), slab(2), slab(3)]
        + [_resident(c.shape) for c in consts] + [_resident((1, RET_WIDTH))],
        out_specs=pl.BlockSpec((RET_TOK, RET_WIDTH), lambda b, i: (b * per_seq + i, 0)),
        out_shape=jax.ShapeDtypeStruct((n, RET_WIDTH), BF16),
        scratch_shapes=[pltpu.VMEM((RET_HEADS // PAIR, LANES, LANES), F32)],
        compiler_params=_cparams(2),
        name="retention",
    )(proj, proj, proj, proj, *consts, ret_gn.reshape(1, RET_WIDTH))


BAND_ROWS = 512


def _band_body(q_ref, kc_ref, kp_ref, vc_ref, vp_ref, o_ref, lse_ref, kf_ref, vf_ref, *, per_seq):
    first = (pl.program_id(0) % per_seq) == 0
    kf_ref[0:BLOCK, :] = kp_ref[...]
    kf_ref[BLOCK:, :] = kc_ref[...]
    vf_ref[0:BLOCK, :] = vp_ref[...]
    vf_ref[BLOCK:, :] = vc_ref[...]
    qi = lax.broadcasted_iota(jnp.int32, (PAIR * BLOCK, 2 * BLOCK), 0) % BLOCK
    kj = lax.broadcasted_iota(jnp.int32, (PAIR * BLOCK, 2 * BLOCK), 1)
    band = (kj >= qi) & (kj <= qi + BLOCK)
    k_min = jnp.where(first, BLOCK, 0)
    band_first = band & (kj >= k_min)
    for n in range(BAND_ROWS // BLOCK):
        rows = slice(n * BLOCK, (n + 1) * BLOCK)
        keys = kf_ref[n * BLOCK:(n + 2) * BLOCK, :]
        vals = vf_ref[n * BLOCK:(n + 2) * BLOCK, :]
        s = lax.dot_general(_stack_heads(q_ref[rows, :]), keys, (((1,), (1,)), ((), ())),
                            preferred_element_type=F32)
        mask = band_first if n == 0 else band
        s = jnp.where(mask, s, NEG_BIG)
        m = jnp.max(s, axis=-1, keepdims=True)
        p = jnp.exp(s - m)
        den = jnp.sum(p, axis=-1, keepdims=True)
        o2 = jnp.dot(p.astype(BF16), vals, preferred_element_type=F32) / den
        lse2 = jnp.broadcast_to(m + jnp.log(den), (PAIR * BLOCK, LANES))
        o_ref[rows, :] = _unstack_heads(o2).astype(BF16)
        lse_ref[rows, :] = _unstack_heads(lse2)


def _band_attention(proj, dil, seq):
    n, width = proj.shape
    view = proj.reshape(n // dil, dil * width)
    cb = width // LANES
    q0, k0, v0 = ((4 * RET_WIDTH + t * DIL_WIDTH) // LANES for t in range(3))
    sub = BAND_ROWS // BLOCK
    per_seq = seq // dil // BAND_ROWS
    n_pairs = DIL_HEADS // PAIR

    def cur(c0):
        return pl.BlockSpec((BAND_ROWS, LANES), lambda i, r, p: (i, r * cb + c0 + p))

    def prev(c0):
        return pl.BlockSpec((BLOCK, LANES),
                            lambda i, r, p: (jnp.maximum(i * sub - 1, 0), r * cb + c0 + p))

    out = pl.BlockSpec((BAND_ROWS, LANES), lambda i, r, p: (i, r * n_pairs + p))
    o, lse = pl.pallas_call(
        functools.partial(_band_body, per_seq=per_seq),
        grid=(n // dil // BAND_ROWS, dil, n_pairs),
        in_specs=[cur(q0), cur(k0), prev(k0), cur(v0), prev(v0)],
        out_specs=[out, out],
        out_shape=[jax.ShapeDtypeStruct((n // dil, dil * DIL_WIDTH), BF16),
                   jax.ShapeDtypeStruct((n // dil, dil * DIL_WIDTH), F32)],
        scratch_shapes=[pltpu.VMEM((BAND_ROWS + BLOCK, LANES), BF16),
                        pltpu.VMEM((BAND_ROWS + BLOCK, LANES), BF16)],
        compiler_params=_cparams(3),
        name=f"band_d{dil}",
    )(view, view, view, view, view)
    return o.reshape(n, DIL_WIDTH), lse.reshape(n, DIL_WIDTH)


def _hyb_out_body(x_ref, ret_ref, o1_ref, o2_ref, o3_ref, l1_ref, l2_ref, l3_ref, w_ref, y_ref):
    l1, l2, l3 = l1_ref[...], l2_ref[...], l3_ref[...]
    m = jnp.maximum(jnp.maximum(l1, l2), l3)
    e1, e2, e3 = jnp.exp(l1 - m), jnp.exp(l2 - m), jnp.exp(l3 - m)
    dil = (e1 * o1_ref[...].astype(F32) + e2 * o2_ref[...].astype(F32)
           + e3 * o3_ref[...].astype(F32)) / (e1 + e2 + e3)
    y = jnp.dot(ret_ref[...], w_ref[0:RET_WIDTH, :], preferred_element_type=F32)
    y = y + jnp.dot(dil.astype(BF16), w_ref[RET_WIDTH:, :], preferred_element_type=F32)
    y_ref[...] = x_ref[...] + y


def _hyb_out(x, ret, outs, lses, w):
    n = x.shape[0]
    row = pl.BlockSpec((TOK_TILE, D_MODEL), lambda i: (i, 0))
    half = pl.BlockSpec((TOK_TILE, DIL_WIDTH), lambda i: (i, 0))
    return pl.pallas_call(
        _hyb_out_body,
        grid=(n // TOK_TILE,),
        in_specs=[row] + [half] * 7 + [_resident((RET_WIDTH + DIL_WIDTH, D_MODEL))],
        out_specs=row,
        out_shape=jax.ShapeDtypeStruct((n, D_MODEL), F32),
        compiler_params=_cparams(1),
        name="hyb_out",
    )(x, ret, *outs, *lses, w.astype(BF16))


def _out_body(x_ref, a_ref, w_ref, y_ref):
    y_ref[...] = x_ref[...] + jnp.dot(a_ref[...], w_ref[...], preferred_element_type=F32)


def _out_proj(x, a, w):
    n = x.shape[0]
    row = pl.BlockSpec((TOK_TILE, D_MODEL), lambda i: (i, 0))
    return pl.pallas_call(
        _out_body,
        grid=(n // TOK_TILE,),
        in_specs=[row, pl.BlockSpec((TOK_TILE, a.shape[1]), lambda i: (i, 0)),
                  _resident(w.shape)],
        out_specs=row,
        out_shape=jax.ShapeDtypeStruct((n, D_MODEL), F32),
        compiler_params=_cparams(1),
        name="out_proj",
    )(x, a, w.astype(BF16))


SB_ROWS = 128
SB_DEAD = -105.0


def _suffix_matrix():
    j = np.arange(SB_TK)[:, None]
    s = np.arange(SB_TK + LANES)[None, :]
    return jnp.asarray(np.where((s >= SB_TK) | (j > s), 1.0, 0.0), dtype=BF16)


def _sb_visit(qs_ref, k_ref, v_ref, tri_ref, acc_ref, carry_ref, kb, diagonal):
    start = pl.multiple_of(kb * SB_TK, SB_TK)
    kblk = k_ref[pl.ds(start, SB_TK), :]
    vblk = v_ref[pl.ds(start, SB_TK), :]
    tri = tri_ref[...]
    top = None
    for rc in range(PAIR * SB_TQ // SB_ROWS):
        rows = slice(rc * SB_ROWS, (rc + 1) * SB_ROWS)
        z = lax.dot_general(qs_ref[rows, :], kblk, (((1,), (1,)), ((), ())),
                            preferred_element_type=F32)
        log_stay = -(jnp.maximum(z, 0.0) + jnp.log(1.0 + jnp.exp(-jnp.abs(z))))
        if diagonal:
            q_row = (rc * SB_ROWS) % SB_TQ + lax.broadcasted_iota(jnp.int32, z.shape, 0)
            causal = lax.broadcasted_iota(jnp.int32, z.shape, 1) < q_row
            log_stay = jnp.where(causal, log_stay, 0.0)
        hi = log_stay.astype(BF16)
        lo = (log_stay - hi.astype(F32)).astype(BF16)
        sums = (jnp.dot(hi, tri, preferred_element_type=F32)
                + jnp.dot(lo, tri, preferred_element_type=F32))
        later = sums[:, :SB_TK]
        total = sums[:, SB_TK:]
        if not diagonal:
            carry = carry_ref[rows, :]
            later = later + jnp.concatenate([carry] * (SB_TK // LANES), axis=1)
            total = total + carry
        a = jnp.exp(z + log_stay + later)
        if diagonal:
            a = jnp.where(causal, a, 0.0)
        pv = jnp.dot(a.astype(BF16), vblk, preferred_element_type=F32)
        if diagonal:
            acc_ref[rows, :] = pv
        else:
            acc_ref[rows, :] += pv
        carry_ref[rows, :] = total
        top = jnp.max(total) if top is None else jnp.maximum(top, jnp.max(total))
    return top


def _sb_body(q_ref, k_ref, v_ref, tri_ref, o_ref, qs_ref, acc_ref, carry_ref):
    qi = pl.program_id(2)
    qs_ref[...] = _stack_heads(q_ref[...])
    visit = functools.partial(_sb_visit, qs_ref, k_ref, v_ref, tri_ref, acc_ref, carry_ref)
    top = visit(qi, True)

    def alive(state):
        t, top = state
        return (t <= qi) & (top > SB_DEAD)

    def step(state):
        t, _ = state
        return t + 1, visit(qi - t, False)

    lax.while_loop(alive, step, (1, top))
    o_ref[...] = _unstack_heads(acc_ref[...]).astype(BF16)


def _stick_breaking(proj, batch, seq):
    n = proj.shape[0]
    n_pairs = SB_HEADS // PAIR
    per_seq = seq // SB_TQ
    qo = lambda b, p, i: (b * per_seq + i, p)
    return pl.pallas_call(
        _sb_body,
        grid=(batch, n_pairs, per_seq),
        in_specs=[pl.BlockSpec((SB_TQ, LANES), qo),
                  pl.BlockSpec((seq, LANES), lambda b, p, i: (b, n_pairs + p)),
                  pl.BlockSpec((seq, LANES), lambda b, p, i: (b, 2 * n_pairs + p)),
                  _resident((SB_TK, SB_TK + LANES))],
        out_specs=pl.BlockSpec((SB_TQ, LANES), qo),
        out_shape=jax.ShapeDtypeStruct((n, SB_WIDTH), BF16),
        scratch_shapes=[pltpu.VMEM((PAIR * SB_TQ, LANES), BF16),
                        pltpu.VMEM((PAIR * SB_TQ, LANES), F32),
                        pltpu.VMEM((PAIR * SB_TQ, LANES), F32)],
        compiler_params=_cparams(3),
        name="stick_breaking",
    )(proj, proj, proj, _suffix_matrix())


HYB_PLAN = (("ret_rot", 1.0), ("ret_rot", HEAD_DIM ** -0.5), ("plain", 1.0), ("plain", 1.0),
            ("dil_rot", HEAD_DIM ** -0.5), ("dil_rot", 1.0), ("plain", 1.0))
SB_PLAN = (("plain", HEAD_DIM ** -0.5),) * 2 + (("plain", 1.0),) * 4


def kernel(x, ffn1_norm, ffn1_w_in, ffn1_w_out, mix_norm, ffn2_norm, ffn2_w_in, ffn2_w_out,
           hyb_w_in, ret_gn, hyb_w_out, sb_w_in, sb_w_out, final_norm):
    batch, seq, _ = x.shape
    depth = ffn1_norm.shape[0]
    tables = (_rotary_tables(seq, HEAD_DIM, RET_ROPE_THETA)
              + _rotary_tables(seq, ROPE_DIM, ROPE_THETA))
    h = x.reshape(batch * seq, D_MODEL)
    for layer in range(depth):
        h = _ffn(h, ffn1_norm[layer], ffn1_w_in[layer], ffn1_w_out[layer], final_norm, False)
        if layer % 2 == 0:
            e = layer // 2
            proj = _proj(h, mix_norm[layer], hyb_w_in[e], HYB_PLAN, tables, seq)
            ret = _retention(proj, ret_gn[e], batch, seq)
            bands = [_band_attention(proj, d, seq) for d in DILATIONS]
            h = _hyb_out(h, ret, [o for o, _ in bands], [l for _, l in bands], hyb_w_out[e])
        else:
            o = layer // 2
            proj = _proj(h, mix_norm[layer], sb_w_in[o], SB_PLAN, (), seq)
            h = _out_proj(h, _stick_breaking(proj, batch, seq), sb_w_out[o])
        h = _ffn(h, ffn2_norm[layer], ffn2_w_in[layer], ffn2_w_out[layer], final_norm,
                 layer == depth - 1)
    return h.reshape(batch, seq, D_MODEL)
```

```python
import functools

import numpy as np
import jax
import jax.numpy as jnp
from jax import lax
from jax.experimental import pallas as pl
from jax.experimental.pallas import tpu as pltpu

F32 = jnp.float32
BF16 = jnp.bfloat16

D_MODEL = 1024
HEAD_DIM = 64
RET_HEADS = 8
DIL_HEADS = 8
SB_HEADS = 16
RET_WIDTH = RET_HEADS * HEAD_DIM
DIL_WIDTH = DIL_HEADS * HEAD_DIM
SB_WIDTH = SB_HEADS * HEAD_DIM
HYB_IN = 4 * RET_WIDTH + 3 * DIL_WIDTH
D_FF = 2816
BLOCK = 128
RET_CHUNK = 128
RET_ROPE_THETA = 10000.0
ROPE_THETA = 500000.0
ROPE_DIM = HEAD_DIM // 4
DILATIONS = (1, 4, 16)
NORM_EPS = 1e-6
GN_EPS = 1e-5

LANES = 128
PAIR = LANES // HEAD_DIM
VMEM_LIMIT = 56 * 1024 * 1024
NEG_BIG = -1e30

TOK_TILE = 512
FF_CHUNK = D_FF // 2
SB_TQ = 256
SB_TK = 256


def _cparams(n_axes):
    return pltpu.CompilerParams(
        dimension_semantics=("arbitrary",) * n_axes,
        vmem_limit_bytes=VMEM_LIMIT)


def _resident(shape):
    return pl.BlockSpec(shape, lambda *_: (0,) * len(shape),
                        pipeline_mode=pl.Buffered(1))


def _rms(x, g):
    return x * lax.rsqrt(jnp.mean(x * x, axis=-1, keepdims=True) + NORM_EPS) * g


def _silu(x):
    return x / (1.0 + jnp.exp(-x))


def _lane_is_head0(shape):
    return lax.broadcasted_iota(jnp.int32, shape, len(shape) - 1) < HEAD_DIM


def _stack_heads(q):
    h0 = _lane_is_head0(q.shape)
    zero = jnp.zeros_like(q)
    return jnp.concatenate([jnp.where(h0, q, zero), jnp.where(h0, zero, q)], axis=0)


def _unstack_heads(o2):
    t = o2.shape[0] // 2
    return jnp.where(_lane_is_head0((t, LANES)), o2[:t], o2[t:])


def _ffn_body(x_ref, g_ref, win_ref, wout_ref, fg_ref, o_ref, *, final_norm):
    x = x_ref[...]
    hb = _rms(x, g_ref[...]).astype(BF16)
    y = jnp.zeros_like(x)
    for c in range(D_FF // FF_CHUNK):
        lo = c * FF_CHUNK
        gate = jnp.dot(hb, win_ref[:, lo:lo + FF_CHUNK], preferred_element_type=F32)
        up = jnp.dot(hb, win_ref[:, D_FF + lo:D_FF + lo + FF_CHUNK], preferred_element_type=F32)
        act = (_silu(gate) * up).astype(BF16)
        y = y + jnp.dot(act, wout_ref[lo:lo + FF_CHUNK, :], preferred_element_type=F32)
    y = x + 0.5 * y
    if final_norm:
        y = _rms(y, fg_ref[...])
    o_ref[...] = y


def _ffn(x, g, w_in, w_out, final_g, final_norm):
    n = x.shape[0]
    row = pl.BlockSpec((TOK_TILE, D_MODEL), lambda i: (i, 0))
    return pl.pallas_call(
        functools.partial(_ffn_body, final_norm=final_norm),
        grid=(n // TOK_TILE,),
        in_specs=[row, _resident((1, D_MODEL)), _resident((D_MODEL, 2 * D_FF)),
                  _resident((D_FF, D_MODEL)), _resident((1, D_MODEL))],
        out_specs=row,
        out_shape=jax.ShapeDtypeStruct((n, D_MODEL), F32),
        compiler_params=_cparams(1),
        name="ffn",
    )(x, g.reshape(1, D_MODEL), w_in.astype(BF16), w_out.astype(BF16),
      final_g.reshape(1, D_MODEL))


PROJ_GROUP = 512


def _rotate(y, cos, sin_a, sin_b, half):
    return (y * cos + pltpu.roll(y, LANES - half, 1) * sin_a + pltpu.roll(y, half, 1) * sin_b)


def _proj_body(x_ref, g_ref, w_ref, *rest, plan):
    tables, o_ref = rest[:-1], rest[-1]
    hb = _rms(x_ref[...], g_ref[...]).astype(BF16)
    for gi, (kind, scale) in enumerate(plan):
        lo = gi * PROJ_GROUP
        y = jnp.dot(hb, w_ref[:, lo:lo + PROJ_GROUP], preferred_element_type=F32)
        for s in range(PROJ_GROUP // LANES):
            ys = y[:, s * LANES:(s + 1) * LANES]
            if kind == "ret_rot":
                ys = _rotate(ys, tables[0][...], tables[1][...], tables[2][...], HEAD_DIM // 2)
            elif kind == "dil_rot":
                ys = _rotate(ys, tables[3][...], tables[4][...], tables[5][...], ROPE_DIM // 2)
            if scale != 1.0:
                ys = ys * scale
            o_ref[:, lo + s * LANES:lo + (s + 1) * LANES] = ys.astype(BF16)


def _proj(x, g, w, plan, tables, seq):
    n = x.shape[0]
    width = w.shape[1]
    row = pl.BlockSpec((TOK_TILE, D_MODEL), lambda i: (i, 0))
    per_seq = seq // TOK_TILE
    tab = pl.BlockSpec((TOK_TILE, LANES), lambda i: (i % per_seq, 0))
    return pl.pallas_call(
        functools.partial(_proj_body, plan=plan),
        grid=(n // TOK_TILE,),
        in_specs=[row, _resident((1, D_MODEL)), _resident((D_MODEL, width))] + [tab] * len(tables),
        out_specs=pl.BlockSpec((TOK_TILE, width), lambda i: (i, 0)),
        out_shape=jax.ShapeDtypeStruct((n, width), BF16),
        compiler_params=_cparams(1),
        name="proj",
    )(x, g.reshape(1, D_MODEL), w.astype(BF16), *tables)


def _rotary_tables(seq, rot_dim, theta):
    half = rot_dim // 2
    lane = np.arange(LANES) % HEAD_DIM
    inv_freq = 1.0 / (theta ** (np.arange(half, dtype=np.float64) / half))
    ang = (np.arange(seq, dtype=np.float64)[:, None] * inv_freq[None, :])[:, lane % half]
    cos, sin = np.cos(ang), np.sin(ang)
    is_x1 = (lane < half)[None, :]
    is_x2 = ((lane >= half) & (lane < rot_dim))[None, :]
    cos_t = np.where(is_x1 | is_x2, cos, 1.0)
    sin_a = np.where(is_x1, -sin, 0.0)
    sin_b = np.where(is_x2, sin, 0.0)
    return tuple(jnp.asarray(t, dtype=F32) for t in (cos_t, sin_a, sin_b))


RET_TOK = 512


def _retention_consts():
    h = np.arange(RET_HEADS, dtype=np.float64)
    log_g = np.log(1.0 - 2.0 ** (-5.0 - h))
    i = np.arange(RET_CHUNK, dtype=np.float64)
    diff = i[:, None] - i[None, :]
    decay_in = np.where(diff >= 0, np.exp(np.maximum(diff, 0.0)[None] * log_g[:, None, None]), 0.0)
    n_pairs = RET_HEADS // PAIR
    decay = decay_in.reshape(n_pairs, PAIR * RET_CHUNK, RET_CHUNK)
    lane_head = np.arange(LANES) // HEAD_DIM
    head_of = (np.arange(n_pairs)[:, None] * PAIR + lane_head[None, :])
    lg = log_g[head_of]
    q_dec = np.exp((i + 1.0)[None, :, None] * lg[:, None, :])
    k_dec = np.exp((RET_CHUNK - 1.0 - i)[None, :, None] * lg[:, None, :])
    same_head = (lane_head[:, None] == lane_head[None, :])
    c_dec = np.exp(RET_CHUNK * lg)[:, :, None] * same_head[None]
    bd = np.broadcast_to(same_head[None], c_dec.shape)
    return tuple(jnp.asarray(t, dtype=F32) for t in (decay, q_dec, k_dec, c_dec, bd))


def _retention_body(q_ref, k_ref, v_ref, g_ref, dec_ref, qd_ref, kd_ref, cd_ref, bd_ref, gn_ref,
                    o_ref, state_ref):
    @pl.when(pl.program_id(1) == 0)
    def _():
        state_ref[...] = jnp.zeros_like(state_ref)

    h0 = _lane_is_head0((RET_CHUNK, LANES))
    inv = 1.0 / HEAD_DIM
    for c in range(RET_TOK // RET_CHUNK):
        rows = slice(c * RET_CHUNK, (c + 1) * RET_CHUNK)
        for p in range(RET_HEADS // PAIR):
            cols = slice(p * LANES, (p + 1) * LANES)
            q, k, v = q_ref[rows, cols], k_ref[rows, cols], v_ref[rows, cols]
            sc = lax.dot_general(_stack_heads(q), k, (((1,), (1,)), ((), ())),
                                 preferred_element_type=F32) * dec_ref[p]
            inner = _unstack_heads(jnp.dot(sc.astype(BF16), v, preferred_element_type=F32))
            state = state_ref[p]
            qd = (q.astype(F32) * qd_ref[p]).astype(BF16)
            o = inner + jnp.dot(qd, state.astype(BF16), preferred_element_type=F32)
            kd = (k.astype(F32) * kd_ref[p]).astype(BF16)
            kv = lax.dot_general(kd, v, (((0,), (0,)), ((), ())), preferred_element_type=F32)
            state_ref[p] = state * cd_ref[p] + kv * bd_ref[p]
            s0 = jnp.sum(jnp.where(h0, o, 0.0), axis=-1, keepdims=True)
            s1 = jnp.sum(jnp.where(h0, 0.0, o), axis=-1, keepdims=True)
            d = o - jnp.where(h0, s0, s1) * inv
            d2 = d * d
            v0 = jnp.sum(jnp.where(h0, d2, 0.0), axis=-1, keepdims=True)
            v1 = jnp.sum(jnp.where(h0, 0.0, d2), axis=-1, keepdims=True)
            y = d * lax.rsqrt(jnp.where(h0, v0, v1) * inv + GN_EPS) * gn_ref[:, cols]
            o_ref[rows, cols] = (y * _silu(g_ref[rows, cols].astype(F32))).astype(BF16)


def _retention(proj, ret_gn, batch, seq):
    n = proj.shape[0]
    per_seq = seq // RET_TOK
    consts = _retention_consts()

    def slab(j):
        return pl.BlockSpec((RET_TOK, RET_WIDTH), lambda b, i, j=j: (b * per_seq + i, j))

    return pl.pallas_call(
        _retention_body,
        grid=(batch, per_seq),
        in_specs=[slab(0), slab(1), slab(2), slab(3)]
        + [_resident(c.shape) for c in consts] + [_resident((1, RET_WIDTH))],
        out_specs=pl.BlockSpec((RET_TOK, RET_WIDTH), lambda b, i: (b * per_seq + i, 0)),
        out_shape=jax.ShapeDtypeStruct((n, RET_WIDTH), BF16),
        scratch_shapes=[pltpu.VMEM((RET_HEADS // PAIR, LANES, LANES), F32)],
        compiler_params=_cparams(2),
        name="retention",
    )(proj, proj, proj, proj, *consts, ret_gn.reshape(1, RET_WIDTH))


BAND_ROWS = 512


def _band_body(q_ref, kc_ref, kp_ref, vc_ref, vp_ref, o_ref, lse_ref, kf_ref, vf_ref, *, per_seq):
    first = (pl.program_id(0) % per_seq) == 0
    kf_ref[0:BLOCK, :] = kp_ref[...]
    kf_ref[BLOCK:, :] = kc_ref[...]
    vf_ref[0:BLOCK, :] = vp_ref[...]
    vf_ref[BLOCK:, :] = vc_ref[...]
    qi = lax.broadcasted_iota(jnp.int32, (PAIR * BLOCK, 2 * BLOCK), 0) % BLOCK
    kj = lax.broadcasted_iota(jnp.int32, (PAIR * BLOCK, 2 * BLOCK), 1)
    band = (kj >= qi) & (kj <= qi + BLOCK)
    k_min = jnp.where(first, BLOCK, 0)
    band_first = band & (kj >= k_min)
    for n in range(BAND_ROWS // BLOCK):
        rows = slice(n * BLOCK, (n + 1) * BLOCK)
        keys = kf_ref[n * BLOCK:(n + 2) * BLOCK, :]
        vals = vf_ref[n * BLOCK:(n + 2) * BLOCK, :]
        s = lax.dot_general(_stack_heads(q_ref[rows, :]), keys, (((1,), (1,)), ((), ())),
                            preferred_element_type=F32)
        mask = band_first if n == 0 else band
        s = jnp.where(mask, s, NEG_BIG)
        m = jnp.max(s, axis=-1, keepdims=True)
        p = jnp.exp(s - m)
        den = jnp.sum(p, axis=-1, keepdims=True)
        o2 = jnp.dot(p.astype(BF16), vals, preferred_element_type=F32) / den
        lse2 = jnp.broadcast_to(m + jnp.log(den), (PAIR * BLOCK, LANES))
        o_ref[rows, :] = _unstack_heads(o2).astype(BF16)
        lse_ref[rows, :] = _unstack_heads(lse2)


def _band_attention(proj, dil, seq):
    n, width = proj.shape
    view = proj.reshape(n // dil, dil * width)
    cb = width // LANES
    q0, k0, v0 = ((4 * RET_WIDTH + t * DIL_WIDTH) // LANES for t in range(3))
    sub = BAND_ROWS // BLOCK
    per_seq = seq // dil // BAND_ROWS
    n_pairs = DIL_HEADS // PAIR

    def cur(c0):
        return pl.BlockSpec((BAND_ROWS, LANES), lambda i, r, p: (i, r * cb + c0 + p))

    def prev(c0):
        return pl.BlockSpec((BLOCK, LANES),
                            lambda i, r, p: (jnp.maximum(i * sub - 1, 0), r * cb + c0 + p))

    out = pl.BlockSpec((BAND_ROWS, LANES), lambda i, r, p: (i, r * n_pairs + p))
    o, lse = pl.pallas_call(
        functools.partial(_band_body, per_seq=per_seq),
        grid=(n // dil // BAND_ROWS, dil, n_pairs),
        in_specs=[cur(q0), cur(k0), prev(k0), cur(v0), prev(v0)],
        out_specs=[out, out],
        out_shape=[jax.ShapeDtypeStruct((n // dil, dil * DIL_WIDTH), BF16),
                   jax.ShapeDtypeStruct((n // dil, dil * DIL_WIDTH), F32)],
        scratch_shapes=[pltpu.VMEM((BAND_ROWS + BLOCK, LANES), BF16),
                        pltpu.VMEM((BAND_ROWS + BLOCK, LANES), BF16)],
        compiler_params=_cparams(3),
        name=f"band_d{dil}",
    )(view, view, view, view, view)
    return o.reshape(n, DIL_WIDTH), lse.reshape(n, DIL_WIDTH)


def _hyb_out_body(x_ref, ret_ref, o1_ref, o2_ref, o3_ref, l1_ref, l2_ref, l3_ref, w_ref, y_ref):
    l1, l2, l3 = l1_ref[...], l2_ref[...], l3_ref[...]
    m = jnp.maximum(jnp.maximum(l1, l2), l3)
    e1, e2, e3 = jnp.exp(l1 - m), jnp.exp(l2 - m), jnp.exp(l3 - m)
    dil = (e1 * o1_ref[...].astype(F32) + e2 * o2_ref[...].astype(F32)
           + e3 * o3_ref[...].astype(F32)) / (e1 + e2 + e3)
    y = jnp.dot(ret_ref[...], w_ref[0:RET_WIDTH, :], preferred_element_type=F32)
    y = y + jnp.dot(dil.astype(BF16), w_ref[RET_WIDTH:, :], preferred_element_type=F32)
    y_ref[...] = x_ref[...] + y


def _hyb_out(x, ret, outs, lses, w):
    n = x.shape[0]
    row = pl.BlockSpec((TOK_TILE, D_MODEL), lambda i: (i, 0))
    half = pl.BlockSpec((TOK_TILE, DIL_WIDTH), lambda i: (i, 0))
    return pl.pallas_call(
        _hyb_out_body,
        grid=(n // TOK_TILE,),
        in_specs=[row] + [half] * 7 + [_resident((RET_WIDTH + DIL_WIDTH, D_MODEL))],
        out_specs=row,
        out_shape=jax.ShapeDtypeStruct((n, D_MODEL), F32),
        compiler_params=_cparams(1),
        name="hyb_out",
    )(x, ret, *outs, *lses, w.astype(BF16))


def _out_body(x_ref, a_ref, w_ref, y_ref):
    y_ref[...] = x_ref[...] + jnp.dot(a_ref[...], w_ref[...], preferred_element_type=F32)


def _out_proj(x, a, w):
    n = x.shape[0]
    row = pl.BlockSpec((TOK_TILE, D_MODEL), lambda i: (i, 0))
    return pl.pallas_call(
        _out_body,
        grid=(n // TOK_TILE,),
        in_specs=[row, pl.BlockSpec((TOK_TILE, a.shape[1]), lambda i: (i, 0)),
                  _resident(w.shape)],
        out_specs=row,
        out_shape=jax.ShapeDtypeStruct((n, D_MODEL), F32),
        compiler_params=_cparams(1),
        name="out_proj",
    )(x, a, w.astype(BF16))


SB_ROWS = 256
SB_DEAD = -105.0


def _suffix_matrix():
    tri = np.arange(SB_TK)[:, None] > np.arange(SB_TK)[None, :]
    return jnp.asarray(np.concatenate([tri, tri], axis=0), dtype=BF16)


def _sb_visit(qs_ref, k_ref, v_ref, tri_ref, acc_ref, carry_ref, first_kb, n_blk, diagonal):
    start = pl.multiple_of(first_kb * SB_TK, SB_TK)
    width = n_blk * SB_TK
    kwin = k_ref[pl.ds(start, width), :]
    vwin = v_ref[pl.ds(start, width), :]
    tri = tri_ref[...]
    for rc in range(PAIR * SB_TQ // SB_ROWS):
        rows = slice(rc * SB_ROWS, (rc + 1) * SB_ROWS)
        z = lax.dot_general(qs_ref[rows, :], kwin, (((1,), (1,)), ((), ())),
                            preferred_element_type=F32)
        log_stay = -(jnp.maximum(z, 0.0) + jnp.log(1.0 + jnp.exp(-jnp.abs(z))))
        if diagonal:
            q_row = (rc * SB_ROWS) % SB_TQ + lax.broadcasted_iota(jnp.int32, z.shape, 0)
            causal = lax.broadcasted_iota(jnp.int32, z.shape, 1) < q_row + (width - SB_TK)
            log_stay = jnp.where(causal, log_stay, 0.0)
        hi = log_stay.astype(BF16)
        lo = (log_stay - hi.astype(F32)).astype(BF16)
        blocks = [slice(b * SB_TK, (b + 1) * SB_TK) for b in range(n_blk)]
        split = jnp.concatenate(
            [jnp.concatenate([hi[:, c], lo[:, c]], axis=1) for c in blocks], axis=0)
        within = jnp.dot(split, tri, preferred_element_type=F32)
        after = None if diagonal else carry_ref[rows, :]
        later = [None] * n_blk
        for b in reversed(range(n_blk)):
            later[b] = within[b * SB_ROWS:(b + 1) * SB_ROWS]
            if after is not None:
                later[b] = later[b] + jnp.concatenate([after] * (SB_TK // LANES), axis=1)
            total = jnp.broadcast_to(jnp.sum(log_stay[:, blocks[b]], axis=-1, keepdims=True),
                                     (SB_ROWS, LANES))
            after = total if after is None else after + total
        a = jnp.exp(z + log_stay + jnp.concatenate(later, axis=1))
        if diagonal:
            a = jnp.where(causal, a, 0.0)
        pv = jnp.dot(a.astype(BF16), vwin, preferred_element_type=F32)
        if diagonal:
            acc_ref[rows, :] = pv
        else:
            acc_ref[rows, :] += pv
        carry_ref[rows, :] = after


def _sb_body(q_ref, k_ref, v_ref, tri_ref, o_ref, qs_ref, acc_ref, carry_ref):
    qi = pl.program_id(2)
    qs_ref[...] = _stack_heads(q_ref[...])
    visit = functools.partial(_sb_visit, qs_ref, k_ref, v_ref, tri_ref, acc_ref, carry_ref)

    @pl.when(qi == 0)
    def _():
        visit(0, 1, True)

    @pl.when(qi > 0)
    def _():
        visit(qi - 1, 2, True)

    def alive(state):
        t, top = state
        return (t <= qi) & (top > SB_DEAD)

    def step(state):
        t, _ = state
        visit(qi - t, 1, False)
        return t + 1, jnp.max(carry_ref[...])

    lax.while_loop(alive, step, (2, jnp.max(carry_ref[...])))
    o_ref[...] = _unstack_heads(acc_ref[...]).astype(BF16)


def _stick_breaking(proj, batch, seq):
    n = proj.shape[0]
    n_pairs = SB_HEADS // PAIR
    per_seq = seq // SB_TQ
    qo = lambda b, p, i: (b * per_seq + i, p)
    return pl.pallas_call(
        _sb_body,
        grid=(batch, n_pairs, per_seq),
        in_specs=[pl.BlockSpec((SB_TQ, LANES), qo),
                  pl.BlockSpec((seq, LANES), lambda b, p, i: (b, n_pairs + p)),
                  pl.BlockSpec((seq, LANES), lambda b, p, i: (b, 2 * n_pairs + p)),
                  _resident((2 * SB_TK, SB_TK))],
        out_specs=pl.BlockSpec((SB_TQ, LANES), qo),
        out_shape=jax.ShapeDtypeStruct((n, SB_WIDTH), BF16),
        scratch_shapes=[pltpu.VMEM((PAIR * SB_TQ, LANES), BF16),
                        pltpu.VMEM((PAIR * SB_TQ, LANES), F32),
                        pltpu.VMEM((PAIR * SB_TQ, LANES), F32)],
        compiler_params=_cparams(3),
        name="stick_breaking",
    )(proj, proj, proj, _suffix_matrix())


HYB_PLAN = (("ret_rot", 1.0), ("ret_rot", HEAD_DIM ** -0.5), ("plain", 1.0), ("plain", 1.0),
            ("dil_rot", HEAD_DIM ** -0.5), ("dil_rot", 1.0), ("plain", 1.0))
SB_PLAN = (("plain", HEAD_DIM ** -0.5),) * 2 + (("plain", 1.0),) * 4


def kernel(x, ffn1_norm, ffn1_w_in, ffn1_w_out, mix_norm, ffn2_norm, ffn2_w_in, ffn2_w_out,
           hyb_w_in, ret_gn, hyb_w_out, sb_w_in, sb_w_out, final_norm):
    batch, seq, _ = x.shape
    depth = ffn1_norm.shape[0]
    tables = (_rotary_tables(seq, HEAD_DIM, RET_ROPE_THETA)
              + _rotary_tables(seq, ROPE_DIM, ROPE_THETA))
    h = x.reshape(batch * seq, D_MODEL)
    for layer in range(depth):
        h = _ffn(h, ffn1_norm[layer], ffn1_w_in[layer], ffn1_w_out[layer], final_norm, False)
        if layer % 2 == 0:
            e = layer // 2
            proj = _proj(h, mix_norm[layer], hyb_w_in[e], HYB_PLAN, tables, seq)
            ret = _retention(proj, ret_gn[e], batch, seq)
            bands = [_band_attention(proj, d, seq) for d in DILATIONS]
            h = _hyb_out(h, ret, [o for o, _ in bands], [l for _, l in bands], hyb_w_out[e])
        else:
            o = layer // 2
            proj = _proj(h, mix_norm[layer], sb_w_in[o], SB_PLAN, (), seq)
            h = _out_proj(h, _stick_breaking(proj, batch, seq), sb_w_out[o])
        h = _ffn(h, ffn2_norm[layer], ffn2_w_in[layer], ffn2_w_out[layer], final_norm,
                 layer == depth - 1)
    return h.reshape(batch, seq, D_MODEL)
```

```python
import functools

import numpy as np
import jax
import jax.numpy as jnp
from jax import lax
from jax.experimental import pallas as pl
from jax.experimental.pallas import tpu as pltpu

F32 = jnp.float32
BF16 = jnp.bfloat16

D_MODEL = 1024
HEAD_DIM = 64
RET_HEADS = 8
DIL_HEADS = 8
SB_HEADS = 16
RET_WIDTH = RET_HEADS * HEAD_DIM
DIL_WIDTH = DIL_HEADS * HEAD_DIM
SB_WIDTH = SB_HEADS * HEAD_DIM
HYB_IN = 4 * RET_WIDTH + 3 * DIL_WIDTH
D_FF = 2816
BLOCK = 128
RET_CHUNK = 128
RET_ROPE_THETA = 10000.0
ROPE_THETA = 500000.0
ROPE_DIM = HEAD_DIM // 4
DILATIONS = (1, 4, 16)
NORM_EPS = 1e-6
GN_EPS = 1e-5

LANES = 128
PAIR = LANES // HEAD_DIM
VMEM_LIMIT = 56 * 1024 * 1024
NEG_BIG = -1e30

TOK_TILE = 512
FF_CHUNK = D_FF // 2
SB_TQ = 256
SB_TK = 256


def _cparams(n_axes):
    return pltpu.CompilerParams(
        dimension_semantics=("arbitrary",) * n_axes,
        vmem_limit_bytes=VMEM_LIMIT)


def _resident(shape):
    return pl.BlockSpec(shape, lambda *_: (0,) * len(shape),
                        pipeline_mode=pl.Buffered(1))


def _rms(x, g):
    return x * lax.rsqrt(jnp.mean(x * x, axis=-1, keepdims=True) + NORM_EPS) * g


def _silu(x):
    return x / (1.0 + jnp.exp(-x))


def _lane_is_head0(shape):
    return lax.broadcasted_iota(jnp.int32, shape, len(shape) - 1) < HEAD_DIM


def _stack_heads(q):
    h0 = _lane_is_head0(q.shape)
    zero = jnp.zeros_like(q)
    return jnp.concatenate([jnp.where(h0, q, zero), jnp.where(h0, zero, q)], axis=0)


def _unstack_heads(o2):
    t = o2.shape[0] // 2
    return jnp.where(_lane_is_head0((t, LANES)), o2[:t], o2[t:])


def _ffn_body(x_ref, g_ref, win_ref, wout_ref, fg_ref, o_ref, *, final_norm):
    x = x_ref[...]
    hb = _rms(x, g_ref[...]).astype(BF16)
    y = jnp.zeros_like(x)
    for c in range(D_FF // FF_CHUNK):
        lo = c * FF_CHUNK
        gate = jnp.dot(hb, win_ref[:, lo:lo + FF_CHUNK], preferred_element_type=F32)
        up = jnp.dot(hb, win_ref[:, D_FF + lo:D_FF + lo + FF_CHUNK], preferred_element_type=F32)
        act = (_silu(gate) * up).astype(BF16)
        y = y + jnp.dot(act, wout_ref[lo:lo + FF_CHUNK, :], preferred_element_type=F32)
    y = x + 0.5 * y
    if final_norm:
        y = _rms(y, fg_ref[...])
    o_ref[...] = y


def _ffn(x, g, w_in, w_out, final_g, final_norm):
    n = x.shape[0]
    row = pl.BlockSpec((TOK_TILE, D_MODEL), lambda i: (i, 0))
    return pl.pallas_call(
        functools.partial(_ffn_body, final_norm=final_norm),
        grid=(n // TOK_TILE,),
        in_specs=[row, _resident((1, D_MODEL)), _resident((D_MODEL, 2 * D_FF)),
                  _resident((D_FF, D_MODEL)), _resident((1, D_MODEL))],
        out_specs=row,
        out_shape=jax.ShapeDtypeStruct((n, D_MODEL), F32),
        compiler_params=_cparams(1),
        name="ffn",
    )(x, g.reshape(1, D_MODEL), w_in.astype(BF16), w_out.astype(BF16),
      final_g.reshape(1, D_MODEL))


PROJ_GROUP = 512


def _rotate(y, cos, sin_a, sin_b, half):
    return (y * cos + pltpu.roll(y, LANES - half, 1) * sin_a + pltpu.roll(y, half, 1) * sin_b)


def _proj_body(x_ref, g_ref, w_ref, *rest, plan):
    tables, o_ref = rest[:-1], rest[-1]
    hb = _rms(x_ref[...], g_ref[...]).astype(BF16)
    for gi, (kind, scale) in enumerate(plan):
        lo = gi * PROJ_GROUP
        y = jnp.dot(hb, w_ref[:, lo:lo + PROJ_GROUP], preferred_element_type=F32)
        for s in range(PROJ_GROUP // LANES):
            ys = y[:, s * LANES:(s + 1) * LANES]
            if kind == "ret_rot":
                ys = _rotate(ys, tables[0][...], tables[1][...], tables[2][...], HEAD_DIM // 2)
            elif kind == "dil_rot":
                ys = _rotate(ys, tables[3][...], tables[4][...], tables[5][...], ROPE_DIM // 2)
            if scale != 1.0:
                ys = ys * scale
            o_ref[:, lo + s * LANES:lo + (s + 1) * LANES] = ys.astype(BF16)


def _proj(x, g, w, plan, tables, seq):
    n = x.shape[0]
    width = w.shape[1]
    row = pl.BlockSpec((TOK_TILE, D_MODEL), lambda i: (i, 0))
    per_seq = seq // TOK_TILE
    tab = pl.BlockSpec((TOK_TILE, LANES), lambda i: (i % per_seq, 0))
    return pl.pallas_call(
        functools.partial(_proj_body, plan=plan),
        grid=(n // TOK_TILE,),
        in_specs=[row, _resident((1, D_MODEL)), _resident((D_MODEL, width))] + [tab] * len(tables),
        out_specs=pl.BlockSpec((TOK_TILE, width), lambda i: (i, 0)),
        out_shape=jax.ShapeDtypeStruct((n, width), BF16),
        compiler_params=_cparams(1),
        name="proj",
    )(x, g.reshape(1, D_MODEL), w.astype(BF16), *tables)


def _rotary_tables(seq, rot_dim, theta):
    half = rot_dim // 2
    lane = np.arange(LANES) % HEAD_DIM
    inv_freq = 1.0 / (theta ** (np.arange(half, dtype=np.float64) / half))
    ang = (np.arange(seq, dtype=np.float64)[:, None] * inv_freq[None, :])[:, lane % half]
    cos, sin = np.cos(ang), np.sin(ang)
    is_x1 = (lane < half)[None, :]
    is_x2 = ((lane >= half) & (lane < rot_dim))[None, :]
    cos_t = np.where(is_x1 | is_x2, cos, 1.0)
    sin_a = np.where(is_x1, -sin, 0.0)
    sin_b = np.where(is_x2, sin, 0.0)
    return tuple(jnp.asarray(t, dtype=F32) for t in (cos_t, sin_a, sin_b))


RET_TOK = 512


def _retention_consts():
    h = np.arange(RET_HEADS, dtype=np.float64)
    log_g = np.log(1.0 - 2.0 ** (-5.0 - h))
    i = np.arange(RET_CHUNK, dtype=np.float64)
    diff = i[:, None] - i[None, :]
    decay_in = np.where(diff >= 0, np.exp(np.maximum(diff, 0.0)[None] * log_g[:, None, None]), 0.0)
    n_pairs = RET_HEADS // PAIR
    decay = decay_in.reshape(n_pairs, PAIR * RET_CHUNK, RET_CHUNK)
    lane_head = np.arange(LANES) // HEAD_DIM
    head_of = (np.arange(n_pairs)[:, None] * PAIR + lane_head[None, :])
    lg = log_g[head_of]
    q_dec = np.exp((i + 1.0)[None, :, None] * lg[:, None, :])
    k_dec = np.exp((RET_CHUNK - 1.0 - i)[None, :, None] * lg[:, None, :])
    same_head = (lane_head[:, None] == lane_head[None, :])
    c_dec = np.exp(RET_CHUNK * lg)[:, :, None] * same_head[None]
    bd = np.broadcast_to(same_head[None], c_dec.shape)
    return tuple(jnp.asarray(t, dtype=F32) for t in (decay, q_dec, k_dec, c_dec, bd))


def _retention_body(q_ref, k_ref, v_ref, g_ref, dec_ref, qd_ref, kd_ref, cd_ref, bd_ref, gn_ref,
                    o_ref, state_ref):
    @pl.when(pl.program_id(1) == 0)
    def _():
        state_ref[...] = jnp.zeros_like(state_ref)

    h0 = _lane_is_head0((RET_CHUNK, LANES))
    inv = 1.0 / HEAD_DIM
    for c in range(RET_TOK // RET_CHUNK):
        rows = slice(c * RET_CHUNK, (c + 1) * RET_CHUNK)
        for p in range(RET_HEADS // PAIR):
            cols = slice(p * LANES, (p + 1) * LANES)
            q, k, v = q_ref[rows, cols], k_ref[rows, cols], v_ref[rows, cols]
            sc = lax.dot_general(_stack_heads(q), k, (((1,), (1,)), ((), ())),
                                 preferred_element_type=F32) * dec_ref[p]
            inner = _unstack_heads(jnp.dot(sc.astype(BF16), v, preferred_element_type=F32))
            state = state_ref[p]
            qd = (q.astype(F32) * qd_ref[p]).astype(BF16)
            o = inner + jnp.dot(qd, state.astype(BF16), preferred_element_type=F32)
            kd = (k.astype(F32) * kd_ref[p]).astype(BF16)
            kv = lax.dot_general(kd, v, (((0,), (0,)), ((), ())), preferred_element_type=F32)
            state_ref[p] = state * cd_ref[p] + kv * bd_ref[p]
            s0 = jnp.sum(jnp.where(h0, o, 0.0), axis=-1, keepdims=True)
            s1 = jnp.sum(jnp.where(h0, 0.0, o), axis=-1, keepdims=True)
            d = o - jnp.where(h0, s0, s1) * inv
            d2 = d * d
            v0 = jnp.sum(jnp.where(h0, d2, 0.0), axis=-1, keepdims=True)
            v1 = jnp.sum(jnp.where(h0, 0.0, d2), axis=-1, keepdims=True)
            y = d * lax.rsqrt(jnp.where(h0, v0, v1) * inv + GN_EPS) * gn_ref[:, cols]
            o_ref[rows, cols] = (y * _silu(g_ref[rows, cols].astype(F32))).astype(BF16)


def _retention(proj, ret_gn, batch, seq):
    n = proj.shape[0]
    per_seq = seq // RET_TOK
    consts = _retention_consts()

    def slab(j):
        return pl.BlockSpec((RET_TOK, RET_WIDTH), lambda b, i, j=j: (b * per_seq + i, j))

    return pl.pallas_call(
        _retention_body,
        grid=(batch, per_seq),
        in_specs=[slab(0), slab(1), slab(2), slab(3)]
        + [_resident(c.shape) for c in consts] + [_resident((1, RET_WIDTH))],
        out_specs=pl.BlockSpec((RET_TOK, RET_WIDTH), lambda b, i: (b * per_seq + i, 0)),
        out_shape=jax.ShapeDtypeStruct((n, RET_WIDTH), BF16),
        scratch_shapes=[pltpu.VMEM((RET_HEADS // PAIR, LANES, LANES), F32)],
        compiler_params=_cparams(2),
        name="retention",
    )(proj, proj, proj, proj, *consts, ret_gn.reshape(1, RET_WIDTH))


DIL_SPAN = BLOCK * max(DILATIONS)
DIL_UNROLL = 4


def _dilated_body(q_ref, kc_ref, kp_ref, vc_ref, vp_ref, out_ref,
                  qf_ref, kf_ref, vf_ref, o_ref, lse_ref):
    first = pl.program_id(2) == 0
    qf_ref[...] = q_ref[...].astype(F32)
    kf_ref[0:DIL_SPAN, :] = kp_ref[...].astype(F32)
    kf_ref[DIL_SPAN:, :] = kc_ref[...].astype(F32)
    vf_ref[0:DIL_SPAN, :] = vp_ref[...].astype(F32)
    vf_ref[DIL_SPAN:, :] = vc_ref[...].astype(F32)
    qi = lax.broadcasted_iota(jnp.int32, (PAIR * BLOCK, 2 * BLOCK), 0) & (BLOCK - 1)
    kj = lax.broadcasted_iota(jnp.int32, (PAIR * BLOCK, 2 * BLOCK), 1)
    band = (kj >= qi) & (kj <= qi + BLOCK)

    for pi, dil in enumerate(DILATIONS):
        per_res = DIL_SPAN // (BLOCK * dil)

        def block(t, pi=pi, dil=dil, per_res=per_res):
            n = t & (per_res - 1)
            r = t // per_res
            q0 = n * (BLOCK * dil) + r
            k0 = q0 + DIL_SPAN - BLOCK * dil
            q = qf_ref[pl.ds(q0, BLOCK, stride=dil), :].astype(BF16)
            keys = kf_ref[pl.ds(k0, 2 * BLOCK, stride=dil), :].astype(BF16)
            vals = vf_ref[pl.ds(k0, 2 * BLOCK, stride=dil), :].astype(BF16)
            s = lax.dot_general(_stack_heads(q), keys, (((1,), (1,)), ((), ())),
                                preferred_element_type=F32)
            k_min = jnp.where(first & (n == 0), BLOCK, 0)
            s = jnp.where(band & (kj >= k_min), s, NEG_BIG)
            m = jnp.max(s, axis=-1, keepdims=True)
            p = jnp.exp(s - m)
            den = jnp.sum(p, axis=-1, keepdims=True)
            o2 = jnp.dot(p.astype(BF16), vals, preferred_element_type=F32) / den
            lse2 = jnp.broadcast_to(m + jnp.log(den), (PAIR * BLOCK, LANES))
            o_ref[pi, pl.ds(q0, BLOCK, stride=dil), :] = _unstack_heads(o2)
            lse_ref[pi, pl.ds(q0, BLOCK, stride=dil), :] = _unstack_heads(lse2)

        def group(i, carry, block=block):
            for u in range(DIL_UNROLL):
                block(i * DIL_UNROLL + u)
            return carry

        lax.fori_loop(0, DIL_SPAN // BLOCK // DIL_UNROLL, group, 0)

    l1, l2, l3 = lse_ref[0], lse_ref[1], lse_ref[2]
    m = jnp.maximum(jnp.maximum(l1, l2), l3)
    e1, e2, e3 = jnp.exp(l1 - m), jnp.exp(l2 - m), jnp.exp(l3 - m)
    out_ref[...] = ((e1 * o_ref[0] + e2 * o_ref[1] + e3 * o_ref[2]) / (e1 + e2 + e3)).astype(BF16)


def _dilated_attention(proj, batch, seq):
    n = proj.shape[0]
    q0, k0, v0 = ((4 * RET_WIDTH + t * DIL_WIDTH) // LANES for t in range(3))
    per_seq = seq // DIL_SPAN
    n_pairs = DIL_HEADS // PAIR

    def cur(c0):
        return pl.BlockSpec((DIL_SPAN, LANES), lambda b, p, i: (b * per_seq + i, c0 + p))

    def prev(c0):
        return pl.BlockSpec((DIL_SPAN, LANES),
                            lambda b, p, i: (jnp.maximum(b * per_seq + i - 1, 0), c0 + p))

    return pl.pallas_call(
        _dilated_body,
        grid=(batch, n_pairs, per_seq),
        in_specs=[cur(q0), cur(k0), prev(k0), cur(v0), prev(v0)],
        out_specs=pl.BlockSpec((DIL_SPAN, LANES), lambda b, p, i: (b * per_seq + i, p)),
        out_shape=jax.ShapeDtypeStruct((n, DIL_WIDTH), BF16),
        scratch_shapes=[pltpu.VMEM((DIL_SPAN, LANES), F32),
                        pltpu.VMEM((2 * DIL_SPAN, LANES), F32),
                        pltpu.VMEM((2 * DIL_SPAN, LANES), F32),
                        pltpu.VMEM((len(DILATIONS), DIL_SPAN, LANES), F32),
                        pltpu.VMEM((len(DILATIONS), DIL_SPAN, LANES), F32)],
        compiler_params=_cparams(3),
        name="dilated",
    )(proj, proj, proj, proj, proj)


def _out_body(x_ref, *rest):
    slabs, w_ref, y_ref = rest[:-2], rest[-2], rest[-1]
    y = x_ref[...]
    lo = 0
    for a_ref in slabs:
        width = a_ref.shape[1]
        y = y + jnp.dot(a_ref[...], w_ref[lo:lo + width, :], preferred_element_type=F32)
        lo += width
    y_ref[...] = y


def _out_proj(x, slabs, w):
    n = x.shape[0]
    row = pl.BlockSpec((TOK_TILE, D_MODEL), lambda i: (i, 0))
    return pl.pallas_call(
        _out_body,
        grid=(n // TOK_TILE,),
        in_specs=[row] + [pl.BlockSpec((TOK_TILE, a.shape[1]), lambda i: (i, 0)) for a in slabs]
        + [_resident(w.shape)],
        out_specs=row,
        out_shape=jax.ShapeDtypeStruct((n, D_MODEL), F32),
        compiler_params=_cparams(1),
        name="out_proj",
    )(x, *slabs, w.astype(BF16))


SB_ROWS = 256
SB_DEAD = -105.0


def _suffix_matrix():
    tri = np.arange(SB_TK)[:, None] > np.arange(SB_TK)[None, :]
    return jnp.asarray(np.concatenate([tri, tri], axis=0), dtype=BF16)


def _sb_visit(qs_ref, k_ref, v_ref, tri_ref, acc_ref, carry_ref, first_kb, n_blk, diagonal):
    start = pl.multiple_of(first_kb * SB_TK, SB_TK)
    width = n_blk * SB_TK
    kwin = k_ref[pl.ds(start, width), :]
    vwin = v_ref[pl.ds(start, width), :]
    tri = tri_ref[...]
    for rc in range(PAIR * SB_TQ // SB_ROWS):
        rows = slice(rc * SB_ROWS, (rc + 1) * SB_ROWS)
        z = lax.dot_general(qs_ref[rows, :], kwin, (((1,), (1,)), ((), ())),
                            preferred_element_type=F32)
        log_stay = -(jnp.maximum(z, 0.0) + jnp.log(1.0 + jnp.exp(-jnp.abs(z))))
        if diagonal:
            q_row = (rc * SB_ROWS) % SB_TQ + lax.broadcasted_iota(jnp.int32, z.shape, 0)
            causal = lax.broadcasted_iota(jnp.int32, z.shape, 1) < q_row + (width - SB_TK)
            log_stay = jnp.where(causal, log_stay, 0.0)
        hi = log_stay.astype(BF16)
        lo = (log_stay - hi.astype(F32)).astype(BF16)
        blocks = [slice(b * SB_TK, (b + 1) * SB_TK) for b in range(n_blk)]
        split = jnp.concatenate(
            [jnp.concatenate([hi[:, c], lo[:, c]], axis=1) for c in blocks], axis=0)
        within = jnp.dot(split, tri, preferred_element_type=F32)
        after = None if diagonal else carry_ref[rows, :]
        later = [None] * n_blk
        for b in reversed(range(n_blk)):
            later[b] = within[b * SB_ROWS:(b + 1) * SB_ROWS]
            if after is not None:
                later[b] = later[b] + jnp.concatenate([after] * (SB_TK // LANES), axis=1)
            total = jnp.broadcast_to(jnp.sum(log_stay[:, blocks[b]], axis=-1, keepdims=True),
                                     (SB_ROWS, LANES))
            after = total if after is None else after + total
        a = jnp.exp(z + log_stay + jnp.concatenate(later, axis=1))
        if diagonal:
            a = jnp.where(causal, a, 0.0)
        pv = jnp.dot(a.astype(BF16), vwin, preferred_element_type=F32)
        if diagonal:
            acc_ref[rows, :] = pv
        else:
            acc_ref[rows, :] += pv
        carry_ref[rows, :] = after


def _sb_body(q_ref, k_ref, v_ref, tri_ref, o_ref, qs_ref, acc_ref, carry_ref):
    qi = pl.program_id(2)
    qs_ref[...] = _stack_heads(q_ref[...])
    visit = functools.partial(_sb_visit, qs_ref, k_ref, v_ref, tri_ref, acc_ref, carry_ref)

    @pl.when(qi == 0)
    def _():
        visit(0, 1, True)

    @pl.when(qi > 0)
    def _():
        visit(qi - 1, 2, True)

    def alive(state):
        t, top = state
        return (t <= qi) & (top > SB_DEAD)

    def step(state):
        t, _ = state
        visit(qi - t, 1, False)
        return t + 1, jnp.max(carry_ref[...])

    lax.while_loop(alive, step, (2, jnp.max(carry_ref[...])))
    o_ref[...] = _unstack_heads(acc_ref[...]).astype(BF16)


def _stick_breaking(proj, batch, seq):
    n = proj.shape[0]
    n_pairs = SB_HEADS // PAIR
    per_seq = seq // SB_TQ
    qo = lambda b, p, i: (b * per_seq + i, p)
    return pl.pallas_call(
        _sb_body,
        grid=(batch, n_pairs, per_seq),
        in_specs=[pl.BlockSpec((SB_TQ, LANES), qo),
                  pl.BlockSpec((seq, LANES), lambda b, p, i: (b, n_pairs + p)),
                  pl.BlockSpec((seq, LANES), lambda b, p, i: (b, 2 * n_pairs + p)),
                  _resident((2 * SB_TK, SB_TK))],
        out_specs=pl.BlockSpec((SB_TQ, LANES), qo),
        out_shape=jax.ShapeDtypeStruct((n, SB_WIDTH), BF16),
        scratch_shapes=[pltpu.VMEM((PAIR * SB_TQ, LANES), BF16),
                        pltpu.VMEM((PAIR * SB_TQ, LANES), F32),
                        pltpu.VMEM((PAIR * SB_TQ, LANES), F32)],
        compiler_params=_cparams(3),
        name="stick_breaking",
    )(proj, proj, proj, _suffix_matrix())


HYB_PLAN = (("ret_rot", 1.0), ("ret_rot", HEAD_DIM ** -0.5), ("plain", 1.0), ("plain", 1.0),
            ("dil_rot", HEAD_DIM ** -0.5), ("dil_rot", 1.0), ("plain", 1.0))
SB_PLAN = (("plain", HEAD_DIM ** -0.5),) * 2 + (("plain", 1.0),) * 4


def kernel(x, ffn1_norm, ffn1_w_in, ffn1_w_out, mix_norm, ffn2_norm, ffn2_w_in, ffn2_w_out,
           hyb_w_in, ret_gn, hyb_w_out, sb_w_in, sb_w_out, final_norm):
    batch, seq, _ = x.shape
    depth = ffn1_norm.shape[0]
    tables = (_rotary_tables(seq, HEAD_DIM, RET_ROPE_THETA)
              + _rotary_tables(seq, ROPE_DIM, ROPE_THETA))
    h = x.reshape(batch * seq, D_MODEL)
    for layer in range(depth):
        h = _ffn(h, ffn1_norm[layer], ffn1_w_in[layer], ffn1_w_out[layer], final_norm, False)
        if layer % 2 == 0:
            e = layer // 2
            proj = _proj(h, mix_norm[layer], hyb_w_in[e], HYB_PLAN, tables, seq)
            ret = _retention(proj, ret_gn[e], batch, seq)
            dil = _dilated_attention(proj, batch, seq)
            h = _out_proj(h, [ret, dil], hyb_w_out[e])
        else:
            o = layer // 2
            proj = _proj(h, mix_norm[layer], sb_w_in[o], SB_PLAN, (), seq)
            h = _out_proj(h, [_stick_breaking(proj, batch, seq)], sb_w_out[o])
        h = _ffn(h, ffn2_norm[layer], ffn2_w_in[layer], ffn2_w_out[layer], final_norm,
                 layer == depth - 1)
    return h.reshape(batch, seq, D_MODEL)
```

```python
import functools

import numpy as np
import jax
import jax.numpy as jnp
from jax import lax
from jax.experimental import pallas as pl
from jax.experimental.pallas import tpu as pltpu

F32 = jnp.float32
BF16 = jnp.bfloat16

D_MODEL = 1024
HEAD_DIM = 64
RET_HEADS = 8
DIL_HEADS = 8
SB_HEADS = 16
RET_WIDTH = RET_HEADS * HEAD_DIM
DIL_WIDTH = DIL_HEADS * HEAD_DIM
SB_WIDTH = SB_HEADS * HEAD_DIM
HYB_IN = 4 * RET_WIDTH + 3 * DIL_WIDTH
D_FF = 2816
BLOCK = 128
RET_CHUNK = 128
RET_ROPE_THETA = 10000.0
ROPE_THETA = 500000.0
ROPE_DIM = HEAD_DIM // 4
DILATIONS = (1, 4, 16)
NORM_EPS = 1e-6
GN_EPS = 1e-5

LANES = 128
PAIR = LANES // HEAD_DIM
VMEM_LIMIT = 56 * 1024 * 1024
NEG_BIG = -1e30

TOK_TILE = 512
FF_CHUNK = D_FF // 2
SB_TQ = 256
SB_TK = 256


def _cparams(n_axes):
    return pltpu.CompilerParams(
        dimension_semantics=("arbitrary",) * n_axes,
        vmem_limit_bytes=VMEM_LIMIT)


def _resident(shape, layer=None):
    if layer is None:
        return pl.BlockSpec(shape, lambda *_: (0,) * len(shape), pipeline_mode=pl.Buffered(1))
    return pl.BlockSpec((None,) + tuple(shape), lambda *_: (layer,) + (0,) * len(shape),
                        pipeline_mode=pl.Buffered(1))


def _rms(x, g):
    return x * lax.rsqrt(jnp.mean(x * x, axis=-1, keepdims=True) + NORM_EPS) * g


def _silu(x):
    return x / (1.0 + jnp.exp(-x))


def _lane_is_head0(shape):
    return lax.broadcasted_iota(jnp.int32, shape, len(shape) - 1) < HEAD_DIM


def _stack_heads(q):
    h0 = _lane_is_head0(q.shape)
    zero = jnp.zeros_like(q)
    return jnp.concatenate([jnp.where(h0, q, zero), jnp.where(h0, zero, q)], axis=0)


def _unstack_heads(o2):
    t = o2.shape[0] // 2
    return jnp.where(_lane_is_head0((t, LANES)), o2[:t], o2[t:])


def _ffn_body(x_ref, g_ref, win_ref, wout_ref, fg_ref, o_ref, *, final_norm):
    x = x_ref[...]
    hb = _rms(x, g_ref[...]).astype(BF16)
    y = jnp.zeros_like(x)
    for c in range(D_FF // FF_CHUNK):
        lo = c * FF_CHUNK
        gate = jnp.dot(hb, win_ref[:, lo:lo + FF_CHUNK], preferred_element_type=F32)
        up = jnp.dot(hb, win_ref[:, D_FF + lo:D_FF + lo + FF_CHUNK], preferred_element_type=F32)
        act = (_silu(gate) * up).astype(BF16)
        y = y + jnp.dot(act, wout_ref[lo:lo + FF_CHUNK, :], preferred_element_type=F32)
    y = x + 0.5 * y
    if final_norm:
        y = _rms(y, fg_ref[...])
    o_ref[...] = y


def _ffn(x, g, w_in, w_out, layer, final_g, final_norm):
    n = x.shape[0]
    row = pl.BlockSpec((TOK_TILE, D_MODEL), lambda i: (i, 0))
    return pl.pallas_call(
        functools.partial(_ffn_body, final_norm=final_norm),
        grid=(n // TOK_TILE,),
        in_specs=[row, _resident((1, D_MODEL)), _resident((D_MODEL, 2 * D_FF), layer),
                  _resident((D_FF, D_MODEL), layer), _resident((1, D_MODEL))],
        out_specs=row,
        out_shape=jax.ShapeDtypeStruct((n, D_MODEL), F32),
        compiler_params=_cparams(1),
        name="ffn",
    )(x, g.reshape(1, D_MODEL), w_in, w_out, final_g.reshape(1, D_MODEL))


PROJ_GROUP = 512


def _rotate(y, cos, sin_a, sin_b, half):
    return (y * cos + pltpu.roll(y, LANES - half, 1) * sin_a + pltpu.roll(y, half, 1) * sin_b)


def _proj_body(x_ref, g_ref, w_ref, *rest, plan):
    tables, o_ref = rest[:-1], rest[-1]
    hb = _rms(x_ref[...], g_ref[...]).astype(BF16)
    for gi, (kind, scale) in enumerate(plan):
        lo = gi * PROJ_GROUP
        y = jnp.dot(hb, w_ref[:, lo:lo + PROJ_GROUP], preferred_element_type=F32)
        for s in range(PROJ_GROUP // LANES):
            ys = y[:, s * LANES:(s + 1) * LANES]
            if kind == "ret_rot":
                ys = _rotate(ys, tables[0][...], tables[1][...], tables[2][...], HEAD_DIM // 2)
            elif kind == "dil_rot":
                ys = _rotate(ys, tables[3][...], tables[4][...], tables[5][...], ROPE_DIM // 2)
            if scale != 1.0:
                ys = ys * scale
            o_ref[:, lo + s * LANES:lo + (s + 1) * LANES] = ys.astype(BF16)


def _proj(x, g, w, layer, plan, tables, seq):
    n = x.shape[0]
    width = w.shape[2]
    row = pl.BlockSpec((TOK_TILE, D_MODEL), lambda i: (i, 0))
    per_seq = seq // TOK_TILE
    tab = pl.BlockSpec((TOK_TILE, LANES), lambda i: (i % per_seq, 0))
    return pl.pallas_call(
        functools.partial(_proj_body, plan=plan),
        grid=(n // TOK_TILE,),
        in_specs=[row, _resident((1, D_MODEL)), _resident((D_MODEL, width), layer)]
        + [tab] * len(tables),
        out_specs=pl.BlockSpec((TOK_TILE, width), lambda i: (i, 0)),
        out_shape=jax.ShapeDtypeStruct((n, width), BF16),
        compiler_params=_cparams(1),
        name="proj",
    )(x, g.reshape(1, D_MODEL), w, *tables)


def _rotary_tables(seq, rot_dim, theta):
    half = rot_dim // 2
    lane = np.arange(LANES) % HEAD_DIM
    inv_freq = 1.0 / (theta ** (np.arange(half, dtype=np.float64) / half))
    ang = (np.arange(seq, dtype=np.float64)[:, None] * inv_freq[None, :])[:, lane % half]
    cos, sin = np.cos(ang), np.sin(ang)
    is_x1 = (lane < half)[None, :]
    is_x2 = ((lane >= half) & (lane < rot_dim))[None, :]
    cos_t = np.where(is_x1 | is_x2, cos, 1.0)
    sin_a = np.where(is_x1, -sin, 0.0)
    sin_b = np.where(is_x2, sin, 0.0)
    return tuple(jnp.asarray(t, dtype=F32) for t in (cos_t, sin_a, sin_b))


RET_TOK = 512


def _retention_consts():
    h = np.arange(RET_HEADS, dtype=np.float64)
    log_g = np.log(1.0 - 2.0 ** (-5.0 - h))
    i = np.arange(RET_CHUNK, dtype=np.float64)
    diff = i[:, None] - i[None, :]
    decay_in = np.where(diff >= 0, np.exp(np.maximum(diff, 0.0)[None] * log_g[:, None, None]), 0.0)
    n_pairs = RET_HEADS // PAIR
    decay = decay_in.reshape(n_pairs, PAIR * RET_CHUNK, RET_CHUNK)
    lane_head = np.arange(LANES) // HEAD_DIM
    head_of = (np.arange(n_pairs)[:, None] * PAIR + lane_head[None, :])
    lg = log_g[head_of]
    q_dec = np.exp((i + 1.0)[None, :, None] * lg[:, None, :])
    k_dec = np.exp((RET_CHUNK - 1.0 - i)[None, :, None] * lg[:, None, :])
    same_head = (lane_head[:, None] == lane_head[None, :])
    c_dec = np.exp(RET_CHUNK * lg)[:, :, None] * same_head[None]
    bd = np.broadcast_to(same_head[None], c_dec.shape)
    return tuple(jnp.asarray(t, dtype=F32) for t in (decay, q_dec, k_dec, c_dec, bd))


def _retention_body(q_ref, k_ref, v_ref, g_ref, dec_ref, qd_ref, kd_ref, cd_ref, bd_ref, gn_ref,
                    o_ref, state_ref):
    @pl.when(pl.program_id(1) == 0)
    def _():
        state_ref[...] = jnp.zeros_like(state_ref)

    h0 = _lane_is_head0((RET_CHUNK, LANES))
    inv = 1.0 / HEAD_DIM
    for c in range(RET_TOK // RET_CHUNK):
        rows = slice(c * RET_CHUNK, (c + 1) * RET_CHUNK)
        for p in range(RET_HEADS // PAIR):
            cols = slice(p * LANES, (p + 1) * LANES)
            q, k, v = q_ref[rows, cols], k_ref[rows, cols], v_ref[rows, cols]
            sc = lax.dot_general(_stack_heads(q), k, (((1,), (1,)), ((), ())),
                                 preferred_element_type=F32) * dec_ref[p]
            inner = _unstack_heads(jnp.dot(sc.astype(BF16), v, preferred_element_type=F32))
            state = state_ref[p]
            qd = (q.astype(F32) * qd_ref[p]).astype(BF16)
            o = inner + jnp.dot(qd, state.astype(BF16), preferred_element_type=F32)
            kd = (k.astype(F32) * kd_ref[p]).astype(BF16)
            kv = lax.dot_general(kd, v, (((0,), (0,)), ((), ())), preferred_element_type=F32)
            state_ref[p] = state * cd_ref[p] + kv * bd_ref[p]
            s0 = jnp.sum(jnp.where(h0, o, 0.0), axis=-1, keepdims=True)
            s1 = jnp.sum(jnp.where(h0, 0.0, o), axis=-1, keepdims=True)
            d = o - jnp.where(h0, s0, s1) * inv
            d2 = d * d
            v0 = jnp.sum(jnp.where(h0, d2, 0.0), axis=-1, keepdims=True)
            v1 = jnp.sum(jnp.where(h0, 0.0, d2), axis=-1, keepdims=True)
            y = d * lax.rsqrt(jnp.where(h0, v0, v1) * inv + GN_EPS) * gn_ref[:, cols]
            o_ref[rows, cols] = (y * _silu(g_ref[rows, cols].astype(F32))).astype(BF16)


def _retention(proj, ret_gn, batch, seq):
    n = proj.shape[0]
    per_seq = seq // RET_TOK
    consts = _retention_consts()

    def slab(j):
        return pl.BlockSpec((RET_TOK, RET_WIDTH), lambda b, i, j=j: (b * per_seq + i, j))

    return pl.pallas_call(
        _retention_body,
        grid=(batch, per_seq),
        in_specs=[slab(0), slab(1), slab(2), slab(3)]
        + [_resident(c.shape) for c in consts] + [_resident((1, RET_WIDTH))],
        out_specs=pl.BlockSpec((RET_TOK, RET_WIDTH), lambda b, i: (b * per_seq + i, 0)),
        out_shape=jax.ShapeDtypeStruct((n, RET_WIDTH), BF16),
        scratch_shapes=[pltpu.VMEM((RET_HEADS // PAIR, LANES, LANES), F32)],
        compiler_params=_cparams(2),
        name="retention",
    )(proj, proj, proj, proj, *consts, ret_gn.reshape(1, RET_WIDTH))


DIL_SPAN = BLOCK * max(DILATIONS)


def _band_bias():
    qi = np.arange(PAIR * BLOCK)[:, None] % BLOCK
    kj = np.arange(2 * BLOCK)[None, :]
    return jnp.asarray(np.where((kj >= qi) & (kj <= qi + BLOCK), 0.0, NEG_BIG), dtype=F32)


def _dilated_body(q_ref, kc_ref, kp_ref, vc_ref, vp_ref, bias_ref, out_ref,
                  qf_ref, kf_ref, vf_ref, num_ref, den_ref, max_ref):
    first = pl.program_id(2) == 0
    qf_ref[...] = q_ref[...].astype(F32)
    kf_ref[0:DIL_SPAN, :] = kp_ref[...].astype(F32)
    kf_ref[DIL_SPAN:, :] = kc_ref[...].astype(F32)
    vf_ref[0:DIL_SPAN, :] = vp_ref[...].astype(F32)
    vf_ref[DIL_SPAN:, :] = vc_ref[...].astype(F32)
    k_min = jnp.where(first, BLOCK, 0)
    kj = lax.broadcasted_iota(jnp.int32, (PAIR * BLOCK, 2 * BLOCK), 1)
    ones = jnp.ones((2 * BLOCK, LANES), BF16)
    head0 = _lane_is_head0((BLOCK, LANES))

    for pi, dil in enumerate(DILATIONS):
        for r in range(dil):
            for n in range(DIL_SPAN // (BLOCK * dil)):
                q0 = n * (BLOCK * dil) + r
                k0 = q0 + DIL_SPAN - BLOCK * dil
                q = qf_ref[pl.ds(q0, BLOCK, stride=dil), :].astype(BF16)
                keys = kf_ref[pl.ds(k0, 2 * BLOCK, stride=dil), :].astype(BF16)
                vals = vf_ref[pl.ds(k0, 2 * BLOCK, stride=dil), :].astype(BF16)
                s = lax.dot_general(_stack_heads(q), keys, (((1,), (1,)), ((), ())),
                                    preferred_element_type=F32) + bias_ref[...]
                if n == 0:
                    s = jnp.where(kj >= k_min, s, NEG_BIG)
                m = jnp.max(s, axis=-1, keepdims=True)
                p = jnp.exp(s - m)
                pv = jnp.dot(p.astype(BF16), jnp.concatenate([vals, ones], axis=1),
                             preferred_element_type=F32)
                rows = pl.ds(q0, BLOCK, stride=dil)
                num_ref[pi, rows, :] = _unstack_heads(pv[:, :LANES])
                den_ref[pi, rows, :] = _unstack_heads(pv[:, LANES:])
                max_ref[pi, rows, :] = jnp.where(head0, m[:BLOCK], m[BLOCK:])

    m = jnp.maximum(jnp.maximum(max_ref[0], max_ref[1]), max_ref[2])
    e = [jnp.exp(max_ref[pi] - m) for pi in range(len(DILATIONS))]
    num = e[0] * num_ref[0] + e[1] * num_ref[1] + e[2] * num_ref[2]
    den = e[0] * den_ref[0] + e[1] * den_ref[1] + e[2] * den_ref[2]
    out_ref[...] = (num / den).astype(BF16)


def _dilated_attention(proj, batch, seq):
    n = proj.shape[0]
    q0, k0, v0 = ((4 * RET_WIDTH + t * DIL_WIDTH) // LANES for t in range(3))
    per_seq = seq // DIL_SPAN
    n_pairs = DIL_HEADS // PAIR

    def cur(c0):
        return pl.BlockSpec((DIL_SPAN, LANES), lambda b, p, i: (b * per_seq + i, c0 + p))

    def prev(c0):
        return pl.BlockSpec((DIL_SPAN, LANES),
                            lambda b, p, i: (jnp.maximum(b * per_seq + i - 1, 0), c0 + p))

    return pl.pallas_call(
        _dilated_body,
        grid=(batch, n_pairs, per_seq),
        in_specs=[cur(q0), cur(k0), prev(k0), cur(v0), prev(v0),
                  _resident((PAIR * BLOCK, 2 * BLOCK))],
        out_specs=pl.BlockSpec((DIL_SPAN, LANES), lambda b, p, i: (b * per_seq + i, p)),
        out_shape=jax.ShapeDtypeStruct((n, DIL_WIDTH), BF16),
        scratch_shapes=[pltpu.VMEM((DIL_SPAN, LANES), F32),
                        pltpu.VMEM((2 * DIL_SPAN, LANES), F32),
                        pltpu.VMEM((2 * DIL_SPAN, LANES), F32),
                        pltpu.VMEM((len(DILATIONS), DIL_SPAN, LANES), F32),
                        pltpu.VMEM((len(DILATIONS), DIL_SPAN, LANES), F32),
                        pltpu.VMEM((len(DILATIONS), DIL_SPAN, LANES), F32)],
        compiler_params=_cparams(3),
        name="dilated",
    )(proj, proj, proj, proj, proj, _band_bias())


def _out_body(x_ref, *rest):
    slabs, w_ref, y_ref = rest[:-2], rest[-2], rest[-1]
    y = x_ref[...]
    lo = 0
    for a_ref in slabs:
        width = a_ref.shape[1]
        y = y + jnp.dot(a_ref[...], w_ref[lo:lo + width, :], preferred_element_type=F32)
        lo += width
    y_ref[...] = y


def _out_proj(x, slabs, w, layer):
    n = x.shape[0]
    row = pl.BlockSpec((TOK_TILE, D_MODEL), lambda i: (i, 0))
    return pl.pallas_call(
        _out_body,
        grid=(n // TOK_TILE,),
        in_specs=[row] + [pl.BlockSpec((TOK_TILE, a.shape[1]), lambda i: (i, 0)) for a in slabs]
        + [_resident(w.shape[1:], layer)],
        out_specs=row,
        out_shape=jax.ShapeDtypeStruct((n, D_MODEL), F32),
        compiler_params=_cparams(1),
        name="out_proj",
    )(x, *slabs, w)


SB_ROWS = 256
SB_PAIRS = 2
SB_DEAD = -105.0


def _suffix_matrix():
    tri = np.arange(SB_TK)[:, None] > np.arange(SB_TK)[None, :]
    return jnp.asarray(np.concatenate([tri, tri], axis=0), dtype=BF16)


def _sb_visit(qs_ref, k_ref, v_ref, tri_ref, acc_ref, carry_ref, first_kb, n_blk, diagonal):
    start = pl.multiple_of(first_kb * SB_TK, SB_TK)
    width = n_blk * SB_TK
    tri = tri_ref[...]
    for rc in range(SB_PAIRS * PAIR * SB_TQ // SB_ROWS):
        rows = slice(rc * SB_ROWS, (rc + 1) * SB_ROWS)
        pair = (rc * SB_ROWS) // (PAIR * SB_TQ)
        kwin = k_ref[pl.ds(start, width), pair * LANES:(pair + 1) * LANES]
        vwin = v_ref[pl.ds(start, width), pair * LANES:(pair + 1) * LANES]
        z = lax.dot_general(qs_ref[rows, :], kwin, (((1,), (1,)), ((), ())),
                            preferred_element_type=F32)
        log_stay = -(jnp.maximum(z, 0.0) + jnp.log(1.0 + jnp.exp(-jnp.abs(z))))
        if diagonal:
            q_row = (rc * SB_ROWS) % SB_TQ + lax.broadcasted_iota(jnp.int32, z.shape, 0)
            causal = lax.broadcasted_iota(jnp.int32, z.shape, 1) < q_row + (width - SB_TK)
            log_stay = jnp.where(causal, log_stay, 0.0)
        hi = log_stay.astype(BF16)
        lo = (log_stay - hi.astype(F32)).astype(BF16)
        blocks = [slice(b * SB_TK, (b + 1) * SB_TK) for b in range(n_blk)]
        split = jnp.concatenate(
            [jnp.concatenate([hi[:, c], lo[:, c]], axis=1) for c in blocks], axis=0)
        within = jnp.dot(split, tri, preferred_element_type=F32)
        after = None if diagonal else carry_ref[rows, :]
        later = [None] * n_blk
        for b in reversed(range(n_blk)):
            later[b] = within[b * SB_ROWS:(b + 1) * SB_ROWS]
            if after is not None:
                later[b] = later[b] + jnp.concatenate([after] * (SB_TK // LANES), axis=1)
            total = jnp.broadcast_to(jnp.sum(log_stay[:, blocks[b]], axis=-1, keepdims=True),
                                     (SB_ROWS, LANES))
            after = total if after is None else after + total
        a = jnp.exp(z + log_stay + jnp.concatenate(later, axis=1))
        if diagonal:
            a = jnp.where(causal, a, 0.0)
        pv = jnp.dot(a.astype(BF16), vwin, preferred_element_type=F32)
        if diagonal:
            acc_ref[rows, :] = pv
        else:
            acc_ref[rows, :] += pv
        carry_ref[rows, :] = after


def _sb_body(q_ref, k_ref, v_ref, tri_ref, o_ref, qs_ref, acc_ref, carry_ref):
    qi = pl.program_id(2)
    stacked = PAIR * SB_TQ
    for pair in range(SB_PAIRS):
        qs_ref[pair * stacked:(pair + 1) * stacked, :] = _stack_heads(
            q_ref[:, pair * LANES:(pair + 1) * LANES])
    visit = functools.partial(_sb_visit, qs_ref, k_ref, v_ref, tri_ref, acc_ref, carry_ref)

    @pl.when(qi == 0)
    def _():
        visit(0, 1, True)

    @pl.when(qi > 0)
    def _():
        visit(qi - 1, 2, True)

    def alive(state):
        t, top = state
        return (t <= qi) & (top > SB_DEAD)

    def step(state):
        t, _ = state
        visit(qi - t, 1, False)
        return t + 1, jnp.max(carry_ref[...])

    lax.while_loop(alive, step, (2, jnp.max(carry_ref[...])))
    for pair in range(SB_PAIRS):
        o_ref[:, pair * LANES:(pair + 1) * LANES] = _unstack_heads(
            acc_ref[pair * stacked:(pair + 1) * stacked, :]).astype(BF16)


def _stick_breaking(proj, batch, seq):
    n = proj.shape[0]
    groups = SB_HEADS // PAIR // SB_PAIRS
    per_seq = seq // SB_TQ
    wide = SB_PAIRS * LANES
    qo = lambda b, p, i: (b * per_seq + i, p)
    return pl.pallas_call(
        _sb_body,
        grid=(batch, groups, per_seq),
        in_specs=[pl.BlockSpec((SB_TQ, wide), qo),
                  pl.BlockSpec((seq, wide), lambda b, p, i: (b, groups + p)),
                  pl.BlockSpec((seq, wide), lambda b, p, i: (b, 2 * groups + p)),
                  _resident((2 * SB_TK, SB_TK))],
        out_specs=pl.BlockSpec((SB_TQ, wide), qo),
        out_shape=jax.ShapeDtypeStruct((n, SB_WIDTH), BF16),
        scratch_shapes=[pltpu.VMEM((SB_PAIRS * PAIR * SB_TQ, LANES), BF16),
                        pltpu.VMEM((SB_PAIRS * PAIR * SB_TQ, LANES), F32),
                        pltpu.VMEM((SB_PAIRS * PAIR * SB_TQ, LANES), F32)],
        compiler_params=_cparams(3),
        name="stick_breaking",
    )(proj, proj, proj, _suffix_matrix())


HYB_PLAN = (("ret_rot", 1.0), ("ret_rot", HEAD_DIM ** -0.5), ("plain", 1.0), ("plain", 1.0),
            ("dil_rot", HEAD_DIM ** -0.5), ("dil_rot", 1.0), ("plain", 1.0))
SB_PLAN = (("plain", HEAD_DIM ** -0.5),) * 2 + (("plain", 1.0),) * 4


def kernel(x, ffn1_norm, ffn1_w_in, ffn1_w_out, mix_norm, ffn2_norm, ffn2_w_in, ffn2_w_out,
           hyb_w_in, ret_gn, hyb_w_out, sb_w_in, sb_w_out, final_norm):
    batch, seq, _ = x.shape
    depth = ffn1_norm.shape[0]
    tables = (_rotary_tables(seq, HEAD_DIM, RET_ROPE_THETA)
              + _rotary_tables(seq, ROPE_DIM, ROPE_THETA))
    (ffn1_w_in, ffn1_w_out, ffn2_w_in, ffn2_w_out, hyb_w_in, hyb_w_out, sb_w_in, sb_w_out) = (
        w.astype(BF16) for w in (ffn1_w_in, ffn1_w_out, ffn2_w_in, ffn2_w_out,
                                 hyb_w_in, hyb_w_out, sb_w_in, sb_w_out))
    h = x.reshape(batch * seq, D_MODEL)
    for layer in range(depth):
        h = _ffn(h, ffn1_norm[layer], ffn1_w_in, ffn1_w_out, layer, final_norm, False)
        if layer % 2 == 0:
            e = layer // 2
            proj = _proj(h, mix_norm[layer], hyb_w_in, e, HYB_PLAN, tables, seq)
            ret = _retention(proj, ret_gn[e], batch, seq)
            dil = _dilated_attention(proj, batch, seq)
            h = _out_proj(h, [ret, dil], hyb_w_out, e)
        else:
            o = layer // 2
            proj = _proj(h, mix_norm[layer], sb_w_in, o, SB_PLAN, (), seq)
            h = _out_proj(h, [_stick_breaking(proj, batch, seq)], sb_w_out, o)
        h = _ffn(h, ffn2_norm[layer], ffn2_w_in, ffn2_w_out, layer, final_norm,
                 layer == depth - 1)
    return h.reshape(batch, seq, D_MODEL)
```

```python
import functools

import numpy as np
import jax
import jax.numpy as jnp
from jax import lax
from jax.experimental import pallas as pl
from jax.experimental.pallas import tpu as pltpu

F32 = jnp.float32
BF16 = jnp.bfloat16

D_MODEL = 1024
HEAD_DIM = 64
RET_HEADS = 8
DIL_HEADS = 8
SB_HEADS = 16
RET_WIDTH = RET_HEADS * HEAD_DIM
DIL_WIDTH = DIL_HEADS * HEAD_DIM
SB_WIDTH = SB_HEADS * HEAD_DIM
HYB_IN = 4 * RET_WIDTH + 3 * DIL_WIDTH
D_FF = 2816
BLOCK = 128
RET_CHUNK = 128
RET_ROPE_THETA = 10000.0
ROPE_THETA = 500000.0
ROPE_DIM = HEAD_DIM // 4
DILATIONS = (1, 4, 16)
NORM_EPS = 1e-6
GN_EPS = 1e-5

LANES = 128
PAIR = LANES // HEAD_DIM
VMEM_LIMIT = 56 * 1024 * 1024
NEG_BIG = -1e30

TOK_TILE = 1024
FFN_TILE = 1024
FF_CHUNK = 256
SB_TQ = 256
SB_TK = 256


def _cparams(n_axes):
    return pltpu.CompilerParams(
        dimension_semantics=("arbitrary",) * n_axes,
        vmem_limit_bytes=VMEM_LIMIT)


def _resident(shape, layer=None):
    if layer is None:
        return pl.BlockSpec(shape, lambda *_: (0,) * len(shape), pipeline_mode=pl.Buffered(1))
    return pl.BlockSpec((None,) + tuple(shape), lambda *_: (layer,) + (0,) * len(shape),
                        pipeline_mode=pl.Buffered(1))


def _rms(x, g):
    return x * lax.rsqrt(jnp.mean(x * x, axis=-1, keepdims=True) + NORM_EPS) * g


def _silu(x):
    return x / (1.0 + jnp.exp(-x))


def _lane_is_head0(shape):
    return lax.broadcasted_iota(jnp.int32, shape, len(shape) - 1) < HEAD_DIM


def _stack_heads(q):
    h0 = _lane_is_head0(q.shape)
    zero = jnp.zeros_like(q)
    return jnp.concatenate([jnp.where(h0, q, zero), jnp.where(h0, zero, q)], axis=0)


def _unstack_heads(o2):
    t = o2.shape[0] // 2
    return jnp.where(_lane_is_head0((t, LANES)), o2[:t], o2[t:])


def _ffn_body(*refs, n_slabs, final_norm):
    refs = list(refs)
    x_ref = refs.pop(0)
    slabs = [refs.pop(0) for _ in range(n_slabs)]
    wmix_ref = refs.pop(0) if n_slabs else None
    g_ref, win_ref, wout_ref = refs.pop(0), refs.pop(0), refs.pop(0)
    fg_ref = refs.pop(0) if final_norm else None
    o_ref, act_ref = refs

    x = x_ref[...]
    lo = 0
    for a_ref in slabs:
        width = a_ref.shape[1]
        x = x + jnp.dot(a_ref[...], wmix_ref[lo:lo + width, :], preferred_element_type=F32)
        lo += width
    hb = _rms(x, g_ref[...]).astype(BF16)
    for c in range(D_FF // FF_CHUNK):
        cols = slice(c * FF_CHUNK, (c + 1) * FF_CHUNK)
        up_cols = slice(D_FF + c * FF_CHUNK, D_FF + (c + 1) * FF_CHUNK)
        gate = jnp.dot(hb, win_ref[:, cols], preferred_element_type=F32)
        up = jnp.dot(hb, win_ref[:, up_cols], preferred_element_type=F32)
        act_ref[:, cols] = (_silu(gate) * up).astype(BF16)
    y = x + 0.5 * jnp.dot(act_ref[...], wout_ref[...], preferred_element_type=F32)
    if final_norm:
        y = _rms(y, fg_ref[...])
    o_ref[...] = y


def _ffn(x, g, w_in, w_out, layer, mix=None, final_g=None):
    n = x.shape[0]
    row = pl.BlockSpec((FFN_TILE, D_MODEL), lambda i: (i, 0))
    args, specs = [x], [row]
    slabs = ()
    if mix is not None:
        slabs, w_mix, mix_layer = mix
        args += [*slabs, w_mix]
        specs += [pl.BlockSpec((FFN_TILE, a.shape[1]), lambda i: (i, 0)) for a in slabs]
        specs += [_resident(w_mix.shape[1:], mix_layer)]
    args += [g.reshape(1, D_MODEL), w_in, w_out]
    specs += [_resident((1, D_MODEL)), _resident((D_MODEL, 2 * D_FF), layer),
              _resident((D_FF, D_MODEL), layer)]
    if final_g is not None:
        args.append(final_g.reshape(1, D_MODEL))
        specs.append(_resident((1, D_MODEL)))
    return pl.pallas_call(
        functools.partial(_ffn_body, n_slabs=len(slabs), final_norm=final_g is not None),
        grid=(n // FFN_TILE,),
        in_specs=specs,
        out_specs=row,
        out_shape=jax.ShapeDtypeStruct((n, D_MODEL), F32),
        scratch_shapes=[pltpu.VMEM((FFN_TILE, D_FF), BF16)],
        compiler_params=_cparams(1),
        name="ffn",
    )(*args)


PROJ_GROUP = 512


def _rotate(y, cos, sin_a, sin_b, half):
    return (y * cos + pltpu.roll(y, LANES - half, 1) * sin_a + pltpu.roll(y, half, 1) * sin_b)


def _proj_body(x_ref, g_ref, w_ref, *rest, plan):
    tables, o_ref = rest[:-1], rest[-1]
    hb = _rms(x_ref[...], g_ref[...]).astype(BF16)
    for gi, (kind, scale) in enumerate(plan):
        lo = gi * PROJ_GROUP
        y = jnp.dot(hb, w_ref[:, lo:lo + PROJ_GROUP], preferred_element_type=F32)
        for s in range(PROJ_GROUP // LANES):
            ys = y[:, s * LANES:(s + 1) * LANES]
            if kind == "ret_rot":
                ys = _rotate(ys, tables[0][...], tables[1][...], tables[2][...], HEAD_DIM // 2)
            elif kind == "dil_rot":
                ys = _rotate(ys, tables[3][...], tables[4][...], tables[5][...], ROPE_DIM // 2)
            if scale != 1.0:
                ys = ys * scale
            o_ref[:, lo + s * LANES:lo + (s + 1) * LANES] = ys.astype(BF16)


def _proj(x, g, w, layer, plan, tables, seq):
    n = x.shape[0]
    width = w.shape[2]
    row = pl.BlockSpec((TOK_TILE, D_MODEL), lambda i: (i, 0))
    per_seq = seq // TOK_TILE
    tab = pl.BlockSpec((TOK_TILE, LANES), lambda i: (i % per_seq, 0))
    return pl.pallas_call(
        functools.partial(_proj_body, plan=plan),
        grid=(n // TOK_TILE,),
        in_specs=[row, _resident((1, D_MODEL)), _resident((D_MODEL, width), layer)]
        + [tab] * len(tables),
        out_specs=pl.BlockSpec((TOK_TILE, width), lambda i: (i, 0)),
        out_shape=jax.ShapeDtypeStruct((n, width), BF16),
        compiler_params=_cparams(1),
        name="proj",
    )(x, g.reshape(1, D_MODEL), w, *tables)


def _rotary_tables(seq, rot_dim, theta):
    half = rot_dim // 2
    lane = np.arange(LANES) % HEAD_DIM
    inv_freq = 1.0 / (theta ** (np.arange(half, dtype=np.float64) / half))
    ang = (np.arange(seq, dtype=np.float64)[:, None] * inv_freq[None, :])[:, lane % half]
    cos, sin = np.cos(ang), np.sin(ang)
    is_x1 = (lane < half)[None, :]
    is_x2 = ((lane >= half) & (lane < rot_dim))[None, :]
    cos_t = np.where(is_x1 | is_x2, cos, 1.0)
    sin_a = np.where(is_x1, -sin, 0.0)
    sin_b = np.where(is_x2, sin, 0.0)
    return tuple(jnp.asarray(t, dtype=F32) for t in (cos_t, sin_a, sin_b))


RET_TOK = 512


def _retention_consts():
    h = np.arange(RET_HEADS, dtype=np.float64)
    log_g = np.log(1.0 - 2.0 ** (-5.0 - h))
    i = np.arange(RET_CHUNK, dtype=np.float64)
    diff = i[:, None] - i[None, :]
    decay_in = np.where(diff >= 0, np.exp(np.maximum(diff, 0.0)[None] * log_g[:, None, None]), 0.0)
    n_pairs = RET_HEADS // PAIR
    decay = decay_in.reshape(n_pairs, PAIR * RET_CHUNK, RET_CHUNK)
    lane_head = np.arange(LANES) // HEAD_DIM
    head_of = (np.arange(n_pairs)[:, None] * PAIR + lane_head[None, :])
    lg = log_g[head_of]
    q_dec = np.exp((i + 1.0)[None, :, None] * lg[:, None, :])
    k_dec = np.exp((RET_CHUNK - 1.0 - i)[None, :, None] * lg[:, None, :])
    same_head = (lane_head[:, None] == lane_head[None, :])
    c_dec = np.exp(RET_CHUNK * lg)[:, :, None] * same_head[None]
    bd = np.broadcast_to(same_head[None], c_dec.shape)
    return tuple(jnp.asarray(t, dtype=F32) for t in (decay, q_dec, k_dec, c_dec, bd))


def _retention_body(q_ref, k_ref, v_ref, g_ref, dec_ref, qd_ref, kd_ref, cd_ref, bd_ref, gn_ref,
                    o_ref, state_ref):
    @pl.when(pl.program_id(1) == 0)
    def _():
        state_ref[...] = jnp.zeros_like(state_ref)

    h0 = _lane_is_head0((RET_CHUNK, LANES))
    inv = 1.0 / HEAD_DIM
    for c in range(RET_TOK // RET_CHUNK):
        rows = slice(c * RET_CHUNK, (c + 1) * RET_CHUNK)
        for p in range(RET_HEADS // PAIR):
            cols = slice(p * LANES, (p + 1) * LANES)
            q, k, v = q_ref[rows, cols], k_ref[rows, cols], v_ref[rows, cols]
            sc = lax.dot_general(_stack_heads(q), k, (((1,), (1,)), ((), ())),
                                 preferred_element_type=F32) * dec_ref[p]
            inner = _unstack_heads(jnp.dot(sc.astype(BF16), v, preferred_element_type=F32))
            state = state_ref[p]
            qd = (q.astype(F32) * qd_ref[p]).astype(BF16)
            o = inner + jnp.dot(qd, state.astype(BF16), preferred_element_type=F32)
            kd = (k.astype(F32) * kd_ref[p]).astype(BF16)
            kv = lax.dot_general(kd, v, (((0,), (0,)), ((), ())), preferred_element_type=F32)
            state_ref[p] = state * cd_ref[p] + kv * bd_ref[p]
            s0 = jnp.sum(jnp.where(h0, o, 0.0), axis=-1, keepdims=True)
            s1 = jnp.sum(jnp.where(h0, 0.0, o), axis=-1, keepdims=True)
            d = o - jnp.where(h0, s0, s1) * inv
            d2 = d * d
            v0 = jnp.sum(jnp.where(h0, d2, 0.0), axis=-1, keepdims=True)
            v1 = jnp.sum(jnp.where(h0, 0.0, d2), axis=-1, keepdims=True)
            y = d * lax.rsqrt(jnp.where(h0, v0, v1) * inv + GN_EPS) * gn_ref[:, cols]
            o_ref[rows, cols] = (y * _silu(g_ref[rows, cols].astype(F32))).astype(BF16)


def _retention(proj, ret_gn, batch, seq):
    n = proj.shape[0]
    per_seq = seq // RET_TOK
    consts = _retention_consts()

    def slab(j):
        return pl.BlockSpec((RET_TOK, RET_WIDTH), lambda b, i, j=j: (b * per_seq + i, j))

    return pl.pallas_call(
        _retention_body,
        grid=(batch, per_seq),
        in_specs=[slab(0), slab(1), slab(2), slab(3)]
        + [_resident(c.shape) for c in consts] + [_resident((1, RET_WIDTH))],
        out_specs=pl.BlockSpec((RET_TOK, RET_WIDTH), lambda b, i: (b * per_seq + i, 0)),
        out_shape=jax.ShapeDtypeStruct((n, RET_WIDTH), BF16),
        scratch_shapes=[pltpu.VMEM((RET_HEADS // PAIR, LANES, LANES), F32)],
        compiler_params=_cparams(2),
        name="retention",
    )(proj, proj, proj, proj, *consts, ret_gn.reshape(1, RET_WIDTH))


DIL_SPAN = BLOCK * max(DILATIONS)


def _band_bias():
    qi = np.arange(PAIR * BLOCK)[:, None] % BLOCK
    kj = np.arange(2 * BLOCK)[None, :]
    return jnp.asarray(np.where((kj >= qi) & (kj <= qi + BLOCK), 0.0, NEG_BIG), dtype=F32)


def _dilated_body(q_ref, kc_ref, kp_ref, vc_ref, vp_ref, bias_ref, out_ref,
                  qf_ref, kf_ref, vf_ref, num_ref, den_ref, max_ref):
    first = pl.program_id(2) == 0
    qf_ref[...] = q_ref[...].astype(F32)
    kf_ref[0:DIL_SPAN, :] = kp_ref[...].astype(F32)
    kf_ref[DIL_SPAN:, :] = kc_ref[...].astype(F32)
    vf_ref[0:DIL_SPAN, :] = vp_ref[...].astype(F32)
    vf_ref[DIL_SPAN:, :] = vc_ref[...].astype(F32)
    k_min = jnp.where(first, BLOCK, 0)
    kj = lax.broadcasted_iota(jnp.int32, (PAIR * BLOCK, 2 * BLOCK), 1)
    ones = jnp.ones((2 * BLOCK, LANES), BF16)
    head0 = _lane_is_head0((BLOCK, LANES))

    for pi, dil in enumerate(DILATIONS):
        for r in range(dil):
            for n in range(DIL_SPAN // (BLOCK * dil)):
                q0 = n * (BLOCK * dil) + r
                k0 = q0 + DIL_SPAN - BLOCK * dil
                q = qf_ref[pl.ds(q0, BLOCK, stride=dil), :].astype(BF16)
                keys = kf_ref[pl.ds(k0, 2 * BLOCK, stride=dil), :].astype(BF16)
                vals = vf_ref[pl.ds(k0, 2 * BLOCK, stride=dil), :].astype(BF16)
                s = lax.dot_general(_stack_heads(q), keys, (((1,), (1,)), ((), ())),
                                    preferred_element_type=F32) + bias_ref[...]
                if n == 0:
                    s = jnp.where(kj >= k_min, s, NEG_BIG)
                m = jnp.max(s, axis=-1, keepdims=True)
                p = jnp.exp(s - m)
                pv = jnp.dot(p.astype(BF16), jnp.concatenate([vals, ones], axis=1),
                             preferred_element_type=F32)
                rows = pl.ds(q0, BLOCK, stride=dil)
                num_ref[pi, rows, :] = _unstack_heads(pv[:, :LANES])
                den_ref[pi, rows, :] = _unstack_heads(pv[:, LANES:])
                max_ref[pi, rows, :] = jnp.where(head0, m[:BLOCK], m[BLOCK:])

    m = jnp.maximum(jnp.maximum(max_ref[0], max_ref[1]), max_ref[2])
    e = [jnp.exp(max_ref[pi] - m) for pi in range(len(DILATIONS))]
    num = e[0] * num_ref[0] + e[1] * num_ref[1] + e[2] * num_ref[2]
    den = e[0] * den_ref[0] + e[1] * den_ref[1] + e[2] * den_ref[2]
    out_ref[...] = (num / den).astype(BF16)


def _dilated_attention(proj, batch, seq):
    n = proj.shape[0]
    q0, k0, v0 = ((4 * RET_WIDTH + t * DIL_WIDTH) // LANES for t in range(3))
    per_seq = seq // DIL_SPAN
    n_pairs = DIL_HEADS // PAIR

    def cur(c0):
        return pl.BlockSpec((DIL_SPAN, LANES), lambda b, p, i: (b * per_seq + i, c0 + p))

    def prev(c0):
        return pl.BlockSpec((DIL_SPAN, LANES),
                            lambda b, p, i: (jnp.maximum(b * per_seq + i - 1, 0), c0 + p))

    return pl.pallas_call(
        _dilated_body,
        grid=(batch, n_pairs, per_seq),
        in_specs=[cur(q0), cur(k0), prev(k0), cur(v0), prev(v0),
                  _resident((PAIR * BLOCK, 2 * BLOCK))],
        out_specs=pl.BlockSpec((DIL_SPAN, LANES), lambda b, p, i: (b * per_seq + i, p)),
        out_shape=jax.ShapeDtypeStruct((n, DIL_WIDTH), BF16),
        scratch_shapes=[pltpu.VMEM((DIL_SPAN, LANES), F32),
                        pltpu.VMEM((2 * DIL_SPAN, LANES), F32),
                        pltpu.VMEM((2 * DIL_SPAN, LANES), F32),
                        pltpu.VMEM((len(DILATIONS), DIL_SPAN, LANES), F32),
                        pltpu.VMEM((len(DILATIONS), DIL_SPAN, LANES), F32),
                        pltpu.VMEM((len(DILATIONS), DIL_SPAN, LANES), F32)],
        compiler_params=_cparams(3),
        name="dilated",
    )(proj, proj, proj, proj, proj, _band_bias())


SB_ROWS = 256
SB_PAIRS = 2
SB_DEAD = -105.0


def _suffix_matrix():
    tri = np.arange(SB_TK)[:, None] > np.arange(SB_TK)[None, :]
    return jnp.asarray(np.concatenate([tri, tri], axis=0), dtype=BF16)


def _sb_visit(qs_ref, k_ref, v_ref, tri_ref, acc_ref, carry_ref, first_kb, n_blk, diagonal):
    start = pl.multiple_of(first_kb * SB_TK, SB_TK)
    width = n_blk * SB_TK
    tri = tri_ref[...]
    for rc in range(SB_PAIRS * PAIR * SB_TQ // SB_ROWS):
        rows = slice(rc * SB_ROWS, (rc + 1) * SB_ROWS)
        pair = (rc * SB_ROWS) // (PAIR * SB_TQ)
        kwin = k_ref[pl.ds(start, width), pair * LANES:(pair + 1) * LANES]
        vwin = v_ref[pl.ds(start, width), pair * LANES:(pair + 1) * LANES]
        z = lax.dot_general(qs_ref[rows, :], kwin, (((1,), (1,)), ((), ())),
                            preferred_element_type=F32)
        log_stay = -(jnp.maximum(z, 0.0) + jnp.log(1.0 + jnp.exp(-jnp.abs(z))))
        if diagonal:
            q_row = (rc * SB_ROWS) % SB_TQ + lax.broadcasted_iota(jnp.int32, z.shape, 0)
            causal = lax.broadcasted_iota(jnp.int32, z.shape, 1) < q_row + (width - SB_TK)
            log_stay = jnp.where(causal, log_stay, 0.0)
        hi = log_stay.astype(BF16)
        lo = (log_stay - hi.astype(F32)).astype(BF16)
        blocks = [slice(b * SB_TK, (b + 1) * SB_TK) for b in range(n_blk)]
        split = jnp.concatenate(
            [jnp.concatenate([hi[:, c], lo[:, c]], axis=1) for c in blocks], axis=0)
        within = jnp.dot(split, tri, preferred_element_type=F32)
        after = None if diagonal else carry_ref[rows, :]
        later = [None] * n_blk
        for b in reversed(range(n_blk)):
            later[b] = within[b * SB_ROWS:(b + 1) * SB_ROWS]
            if after is not None:
                later[b] = later[b] + jnp.concatenate([after] * (SB_TK // LANES), axis=1)
            total = jnp.broadcast_to(jnp.sum(log_stay[:, blocks[b]], axis=-1, keepdims=True),
                                     (SB_ROWS, LANES))
            after = total if after is None else after + total
        a = jnp.exp(z + log_stay + jnp.concatenate(later, axis=1))
        if diagonal:
            a = jnp.where(causal, a, 0.0)
        pv = jnp.dot(a.astype(BF16), vwin, preferred_element_type=F32)
        if diagonal:
            acc_ref[rows, :] = pv
        else:
            acc_ref[rows, :] += pv
        carry_ref[rows, :] = after


def _sb_body(q_ref, k_ref, v_ref, tri_ref, o_ref, qs_ref, acc_ref, carry_ref):
    qi = pl.program_id(2)
    stacked = PAIR * SB_TQ
    for pair in range(SB_PAIRS):
        qs_ref[pair * stacked:(pair + 1) * stacked, :] = _stack_heads(
            q_ref[:, pair * LANES:(pair + 1) * LANES])
    visit = functools.partial(_sb_visit, qs_ref, k_ref, v_ref, tri_ref, acc_ref, carry_ref)

    @pl.when(qi == 0)
    def _():
        visit(0, 1, True)

    @pl.when(qi > 0)
    def _():
        visit(qi - 1, 2, True)

    def alive(state):
        t, top = state
        return (t <= qi) & (top > SB_DEAD)

    def step(state):
        t, _ = state
        visit(qi - t, 1, False)
        return t + 1, jnp.max(carry_ref[...])

    lax.while_loop(alive, step, (2, jnp.max(carry_ref[...])))
    for pair in range(SB_PAIRS):
        o_ref[:, pair * LANES:(pair + 1) * LANES] = _unstack_heads(
            acc_ref[pair * stacked:(pair + 1) * stacked, :]).astype(BF16)


def _stick_breaking(proj, batch, seq):
    n = proj.shape[0]
    groups = SB_HEADS // PAIR // SB_PAIRS
    per_seq = seq // SB_TQ
    wide = SB_PAIRS * LANES
    qo = lambda b, p, i: (b * per_seq + i, p)
    return pl.pallas_call(
        _sb_body,
        grid=(batch, groups, per_seq),
        in_specs=[pl.BlockSpec((SB_TQ, wide), qo),
                  pl.BlockSpec((seq, wide), lambda b, p, i: (b, groups + p)),
                  pl.BlockSpec((seq, wide), lambda b, p, i: (b, 2 * groups + p)),
                  _resident((2 * SB_TK, SB_TK))],
        out_specs=pl.BlockSpec((SB_TQ, wide), qo),
        out_shape=jax.ShapeDtypeStruct((n, SB_WIDTH), BF16),
        scratch_shapes=[pltpu.VMEM((SB_PAIRS * PAIR * SB_TQ, LANES), BF16),
                        pltpu.VMEM((SB_PAIRS * PAIR * SB_TQ, LANES), F32),
                        pltpu.VMEM((SB_PAIRS * PAIR * SB_TQ, LANES), F32)],
        compiler_params=_cparams(3),
        name="stick_breaking",
    )(proj, proj, proj, _suffix_matrix())


HYB_PLAN = (("ret_rot", 1.0), ("ret_rot", HEAD_DIM ** -0.5), ("plain", 1.0), ("plain", 1.0),
            ("dil_rot", HEAD_DIM ** -0.5), ("dil_rot", 1.0), ("plain", 1.0))
SB_PLAN = (("plain", HEAD_DIM ** -0.5),) * 2 + (("plain", 1.0),) * 4


def kernel(x, ffn1_norm, ffn1_w_in, ffn1_w_out, mix_norm, ffn2_norm, ffn2_w_in, ffn2_w_out,
           hyb_w_in, ret_gn, hyb_w_out, sb_w_in, sb_w_out, final_norm):
    batch, seq, _ = x.shape
    depth = ffn1_norm.shape[0]
    tables = (_rotary_tables(seq, HEAD_DIM, RET_ROPE_THETA)
              + _rotary_tables(seq, ROPE_DIM, ROPE_THETA))
    (ffn1_w_in, ffn1_w_out, ffn2_w_in, ffn2_w_out, hyb_w_in, hyb_w_out, sb_w_in, sb_w_out) = (
        w.astype(BF16) for w in (ffn1_w_in, ffn1_w_out, ffn2_w_in, ffn2_w_out,
                                 hyb_w_in, hyb_w_out, sb_w_in, sb_w_out))
    h = x.reshape(batch * seq, D_MODEL)
    for layer in range(depth):
        h = _ffn(h, ffn1_norm[layer], ffn1_w_in, ffn1_w_out, layer)
        if layer % 2 == 0:
            e = layer // 2
            proj = _proj(h, mix_norm[layer], hyb_w_in, e, HYB_PLAN, tables, seq)
            mix = ([_retention(proj, ret_gn[e], batch, seq),
                    _dilated_attention(proj, batch, seq)], hyb_w_out, e)
        else:
            o = layer // 2
            proj = _proj(h, mix_norm[layer], sb_w_in, o, SB_PLAN, (), seq)
            mix = ([_stick_breaking(proj, batch, seq)], sb_w_out, o)
        h = _ffn(h, ffn2_norm[layer], ffn2_w_in, ffn2_w_out, layer, mix,
                 final_norm if layer == depth - 1 else None)
    return h.reshape(batch, seq, D_MODEL)
```

```python
import functools

import numpy as np
import jax
import jax.numpy as jnp
from jax import lax
from jax.experimental import pallas as pl
from jax.experimental.pallas import tpu as pltpu

F32 = jnp.float32
BF16 = jnp.bfloat16

D_MODEL = 1024
HEAD_DIM = 64
RET_HEADS = 8
DIL_HEADS = 8
SB_HEADS = 16
RET_WIDTH = RET_HEADS * HEAD_DIM
DIL_WIDTH = DIL_HEADS * HEAD_DIM
SB_WIDTH = SB_HEADS * HEAD_DIM
HYB_IN = 4 * RET_WIDTH + 3 * DIL_WIDTH
D_FF = 2816
BLOCK = 128
RET_CHUNK = 128
RET_ROPE_THETA = 10000.0
ROPE_THETA = 500000.0
ROPE_DIM = HEAD_DIM // 4
DILATIONS = (1, 4, 16)
NORM_EPS = 1e-6
GN_EPS = 1e-5

LANES = 128
PAIR = LANES // HEAD_DIM
VMEM_LIMIT = 56 * 1024 * 1024
NEG_BIG = -1e30

TOK_TILE = 1024
FFN_TILE = 1024
FF_CHUNK = 256
SB_TQ = 256
SB_TK = 256


def _cparams(n_axes):
    return pltpu.CompilerParams(
        dimension_semantics=("arbitrary",) * n_axes,
        vmem_limit_bytes=VMEM_LIMIT)


def _resident(shape, layer=None):
    if layer is None:
        return pl.BlockSpec(shape, lambda *_: (0,) * len(shape), pipeline_mode=pl.Buffered(1))
    return pl.BlockSpec((None,) + tuple(shape), lambda *_: (layer,) + (0,) * len(shape),
                        pipeline_mode=pl.Buffered(1))


def _rms(x, g):
    return x * lax.rsqrt(jnp.mean(x * x, axis=-1, keepdims=True) + NORM_EPS) * g


def _silu(x):
    return x / (1.0 + jnp.exp(-x))


def _lane_is_head0(shape):
    return lax.broadcasted_iota(jnp.int32, shape, len(shape) - 1) < HEAD_DIM


def _stack_heads(q):
    h0 = _lane_is_head0(q.shape)
    zero = jnp.zeros_like(q)
    return jnp.concatenate([jnp.where(h0, q, zero), jnp.where(h0, zero, q)], axis=0)


def _unstack_heads(o2):
    t = o2.shape[0] // 2
    return jnp.where(_lane_is_head0((t, LANES)), o2[:t], o2[t:])


def _ffn_body(*refs, n_slabs, final_norm):
    refs = list(refs)
    x_ref = refs.pop(0)
    slabs = [refs.pop(0) for _ in range(n_slabs)]
    wmix_ref = refs.pop(0) if n_slabs else None
    g_ref, win_ref, wout_ref = refs.pop(0), refs.pop(0), refs.pop(0)
    fg_ref = refs.pop(0) if final_norm else None
    o_ref, act_ref = refs

    x = x_ref[...]
    lo = 0
    for a_ref in slabs:
        width = a_ref.shape[1]
        x = x + jnp.dot(a_ref[...], wmix_ref[lo:lo + width, :], preferred_element_type=F32)
        lo += width
    hb = _rms(x, g_ref[...]).astype(BF16)
    for c in range(D_FF // FF_CHUNK):
        cols = slice(c * FF_CHUNK, (c + 1) * FF_CHUNK)
        up_cols = slice(D_FF + c * FF_CHUNK, D_FF + (c + 1) * FF_CHUNK)
        gate = jnp.dot(hb, win_ref[:, cols], preferred_element_type=F32)
        up = jnp.dot(hb, win_ref[:, up_cols], preferred_element_type=F32)
        act_ref[:, cols] = (_silu(gate) * up).astype(BF16)
    y = x + 0.5 * jnp.dot(act_ref[...], wout_ref[...], preferred_element_type=F32)
    if final_norm:
        y = _rms(y, fg_ref[...])
    o_ref[...] = y


def _ffn(x, g, w_in, w_out, layer, mix=None, final_g=None):
    n = x.shape[0]
    row = pl.BlockSpec((FFN_TILE, D_MODEL), lambda i: (i, 0))
    args, specs = [x], [row]
    slabs = ()
    if mix is not None:
        slabs, w_mix, mix_layer = mix
        args += [*slabs, w_mix]
        specs += [pl.BlockSpec((FFN_TILE, a.shape[1]), lambda i: (i, 0)) for a in slabs]
        specs += [_resident(w_mix.shape[1:], mix_layer)]
    args += [g.reshape(1, D_MODEL), w_in, w_out]
    specs += [_resident((1, D_MODEL)), _resident((D_MODEL, 2 * D_FF), layer),
              _resident((D_FF, D_MODEL), layer)]
    if final_g is not None:
        args.append(final_g.reshape(1, D_MODEL))
        specs.append(_resident((1, D_MODEL)))
    return pl.pallas_call(
        functools.partial(_ffn_body, n_slabs=len(slabs), final_norm=final_g is not None),
        grid=(n // FFN_TILE,),
        in_specs=specs,
        out_specs=row,
        out_shape=jax.ShapeDtypeStruct((n, D_MODEL), F32),
        scratch_shapes=[pltpu.VMEM((FFN_TILE, D_FF), BF16)],
        compiler_params=_cparams(1),
        name="ffn",
    )(*args)


PROJ_GROUP = 512


def _rotate(y, cos, sin_a, sin_b, half):
    return (y * cos + pltpu.roll(y, LANES - half, 1) * sin_a + pltpu.roll(y, half, 1) * sin_b)


def _proj_body(x_ref, g_ref, w_ref, *rest, plan):
    tables, o_ref = rest[:-1], rest[-1]
    hb = _rms(x_ref[...], g_ref[...]).astype(BF16)
    for gi, (kind, scale) in enumerate(plan):
        lo = gi * PROJ_GROUP
        y = jnp.dot(hb, w_ref[:, lo:lo + PROJ_GROUP], preferred_element_type=F32)
        for s in range(PROJ_GROUP // LANES):
            ys = y[:, s * LANES:(s + 1) * LANES]
            if kind == "ret_rot":
                ys = _rotate(ys, tables[0][...], tables[1][...], tables[2][...], HEAD_DIM // 2)
            elif kind == "dil_rot":
                ys = _rotate(ys, tables[3][...], tables[4][...], tables[5][...], ROPE_DIM // 2)
            if scale != 1.0:
                ys = ys * scale
            o_ref[:, lo + s * LANES:lo + (s + 1) * LANES] = ys.astype(BF16)


def _proj(x, g, w, layer, plan, tables, seq):
    n = x.shape[0]
    width = w.shape[2]
    row = pl.BlockSpec((TOK_TILE, D_MODEL), lambda i: (i, 0))
    per_seq = seq // TOK_TILE
    tab = pl.BlockSpec((TOK_TILE, LANES), lambda i: (i % per_seq, 0))
    return pl.pallas_call(
        functools.partial(_proj_body, plan=plan),
        grid=(n // TOK_TILE,),
        in_specs=[row, _resident((1, D_MODEL)), _resident((D_MODEL, width), layer)]
        + [tab] * len(tables),
        out_specs=pl.BlockSpec((TOK_TILE, width), lambda i: (i, 0)),
        out_shape=jax.ShapeDtypeStruct((n, width), BF16),
        compiler_params=_cparams(1),
        name="proj",
    )(x, g.reshape(1, D_MODEL), w, *tables)


def _rotary_tables(seq, rot_dim, theta):
    half = rot_dim // 2
    lane = np.arange(LANES) % HEAD_DIM
    inv_freq = 1.0 / (theta ** (np.arange(half, dtype=np.float64) / half))
    ang = (np.arange(seq, dtype=np.float64)[:, None] * inv_freq[None, :])[:, lane % half]
    cos, sin = np.cos(ang), np.sin(ang)
    is_x1 = (lane < half)[None, :]
    is_x2 = ((lane >= half) & (lane < rot_dim))[None, :]
    cos_t = np.where(is_x1 | is_x2, cos, 1.0)
    sin_a = np.where(is_x1, -sin, 0.0)
    sin_b = np.where(is_x2, sin, 0.0)
    return tuple(jnp.asarray(t, dtype=F32) for t in (cos_t, sin_a, sin_b))


RET_TOK = 512


def _retention_consts():
    h = np.arange(RET_HEADS, dtype=np.float64)
    log_g = np.log(1.0 - 2.0 ** (-5.0 - h))
    i = np.arange(RET_CHUNK, dtype=np.float64)
    diff = i[:, None] - i[None, :]
    decay_in = np.where(diff >= 0, np.exp(np.maximum(diff, 0.0)[None] * log_g[:, None, None]), 0.0)
    n_pairs = RET_HEADS // PAIR
    decay = decay_in.reshape(n_pairs, PAIR * RET_CHUNK, RET_CHUNK)
    lane_head = np.arange(LANES) // HEAD_DIM
    head_of = (np.arange(n_pairs)[:, None] * PAIR + lane_head[None, :])
    lg = log_g[head_of]
    q_dec = np.exp((i + 1.0)[None, :, None] * lg[:, None, :])
    k_dec = np.exp((RET_CHUNK - 1.0 - i)[None, :, None] * lg[:, None, :])
    same_head = (lane_head[:, None] == lane_head[None, :])
    c_dec = np.exp(RET_CHUNK * lg)[:, :, None] * same_head[None]
    bd = np.broadcast_to(same_head[None], c_dec.shape)
    return tuple(jnp.asarray(t, dtype=F32) for t in (decay, q_dec, k_dec, c_dec, bd))


def _retention_body(q_ref, k_ref, v_ref, g_ref, dec_ref, qd_ref, kd_ref, cd_ref, bd_ref, gn_ref,
                    o_ref, state_ref):
    @pl.when(pl.program_id(1) == 0)
    def _():
        state_ref[...] = jnp.zeros_like(state_ref)

    h0 = _lane_is_head0((RET_CHUNK, LANES))
    inv = 1.0 / HEAD_DIM
    for c in range(RET_TOK // RET_CHUNK):
        rows = slice(c * RET_CHUNK, (c + 1) * RET_CHUNK)
        for p in range(RET_HEADS // PAIR):
            cols = slice(p * LANES, (p + 1) * LANES)
            q, k, v = q_ref[rows, cols], k_ref[rows, cols], v_ref[rows, cols]
            sc = lax.dot_general(_stack_heads(q), k, (((1,), (1,)), ((), ())),
                                 preferred_element_type=F32) * dec_ref[p]
            inner = _unstack_heads(jnp.dot(sc.astype(BF16), v, preferred_element_type=F32))
            state = state_ref[p]
            qd = (q.astype(F32) * qd_ref[p]).astype(BF16)
            o = inner + jnp.dot(qd, state.astype(BF16), preferred_element_type=F32)
            kd = (k.astype(F32) * kd_ref[p]).astype(BF16)
            kv = lax.dot_general(kd, v, (((0,), (0,)), ((), ())), preferred_element_type=F32)
            state_ref[p] = state * cd_ref[p] + kv * bd_ref[p]
            s0 = jnp.sum(jnp.where(h0, o, 0.0), axis=-1, keepdims=True)
            s1 = jnp.sum(jnp.where(h0, 0.0, o), axis=-1, keepdims=True)
            d = o - jnp.where(h0, s0, s1) * inv
            d2 = d * d
            v0 = jnp.sum(jnp.where(h0, d2, 0.0), axis=-1, keepdims=True)
            v1 = jnp.sum(jnp.where(h0, 0.0, d2), axis=-1, keepdims=True)
            y = d * lax.rsqrt(jnp.where(h0, v0, v1) * inv + GN_EPS) * gn_ref[:, cols]
            o_ref[rows, cols] = (y * _silu(g_ref[rows, cols].astype(F32))).astype(BF16)


def _retention(proj, ret_gn, batch, seq):
    n = proj.shape[0]
    per_seq = seq // RET_TOK
    consts = _retention_consts()

    def slab(j):
        return pl.BlockSpec((RET_TOK, RET_WIDTH), lambda b, i, j=j: (b * per_seq + i, j))

    return pl.pallas_call(
        _retention_body,
        grid=(batch, per_seq),
        in_specs=[slab(0), slab(1), slab(2), slab(3)]
        + [_resident(c.shape) for c in consts] + [_resident((1, RET_WIDTH))],
        out_specs=pl.BlockSpec((RET_TOK, RET_WIDTH), lambda b, i: (b * per_seq + i, 0)),
        out_shape=jax.ShapeDtypeStruct((n, RET_WIDTH), BF16),
        scratch_shapes=[pltpu.VMEM((RET_HEADS // PAIR, LANES, LANES), F32)],
        compiler_params=_cparams(2),
        name="retention",
    )(proj, proj, proj, proj, *consts, ret_gn.reshape(1, RET_WIDTH))


DIL_SPAN = BLOCK * max(DILATIONS)


def _band_bias():
    qi = np.arange(PAIR * BLOCK)[:, None] % BLOCK
    kj = np.arange(2 * BLOCK)[None, :]
    return jnp.asarray(np.where((kj >= qi) & (kj <= qi + BLOCK), 0.0, NEG_BIG), dtype=F32)


def _dilated_body(q_ref, kc_ref, kp_ref, vc_ref, vp_ref, bias_ref, out_ref,
                  qf_ref, kf_ref, vf_ref, num_ref, den_ref, max_ref):
    first = pl.program_id(2) == 0
    qf_ref[...] = q_ref[...].astype(F32)
    kf_ref[0:DIL_SPAN, :] = kp_ref[...].astype(F32)
    kf_ref[DIL_SPAN:, :] = kc_ref[...].astype(F32)
    vf_ref[0:DIL_SPAN, :] = vp_ref[...].astype(F32)
    vf_ref[DIL_SPAN:, :] = vc_ref[...].astype(F32)
    k_min = jnp.where(first, BLOCK, 0)
    kj = lax.broadcasted_iota(jnp.int32, (PAIR * BLOCK, 2 * BLOCK), 1)
    ones = jnp.ones((2 * BLOCK, LANES), BF16)
    head0 = _lane_is_head0((BLOCK, LANES))

    for pi, dil in enumerate(DILATIONS):
        for r in range(dil):
            for n in range(DIL_SPAN // (BLOCK * dil)):
                q0 = n * (BLOCK * dil) + r
                k0 = q0 + DIL_SPAN - BLOCK * dil
                q = qf_ref[pl.ds(q0, BLOCK, stride=dil), :].astype(BF16)
                keys = kf_ref[pl.ds(k0, 2 * BLOCK, stride=dil), :].astype(BF16)
                vals = vf_ref[pl.ds(k0, 2 * BLOCK, stride=dil), :].astype(BF16)
                s = lax.dot_general(_stack_heads(q), keys, (((1,), (1,)), ((), ())),
                                    preferred_element_type=F32) + bias_ref[...]
                if n == 0:
                    s = jnp.where(kj >= k_min, s, NEG_BIG)
                m = jnp.max(s, axis=-1, keepdims=True)
                p = jnp.exp(s - m)
                pv = jnp.dot(p.astype(BF16), jnp.concatenate([vals, ones], axis=1),
                             preferred_element_type=F32)
                rows = pl.ds(q0, BLOCK, stride=dil)
                num_ref[pi, rows, :] = _unstack_heads(pv[:, :LANES])
                den_ref[pi, rows, :] = _unstack_heads(pv[:, LANES:])
                max_ref[pi, rows, :] = jnp.where(head0, m[:BLOCK], m[BLOCK:])

    m = jnp.maximum(jnp.maximum(max_ref[0], max_ref[1]), max_ref[2])
    e = [jnp.exp(max_ref[pi] - m) for pi in range(len(DILATIONS))]
    num = e[0] * num_ref[0] + e[1] * num_ref[1] + e[2] * num_ref[2]
    den = e[0] * den_ref[0] + e[1] * den_ref[1] + e[2] * den_ref[2]
    out_ref[...] = (num / den).astype(BF16)


def _dilated_attention(proj, batch, seq):
    n = proj.shape[0]
    q0, k0, v0 = ((4 * RET_WIDTH + t * DIL_WIDTH) // LANES for t in range(3))
    per_seq = seq // DIL_SPAN
    n_pairs = DIL_HEADS // PAIR

    def cur(c0):
        return pl.BlockSpec((DIL_SPAN, LANES), lambda b, p, i: (b * per_seq + i, c0 + p))

    def prev(c0):
        return pl.BlockSpec((DIL_SPAN, LANES),
                            lambda b, p, i: (jnp.maximum(b * per_seq + i - 1, 0), c0 + p))

    return pl.pallas_call(
        _dilated_body,
        grid=(batch, n_pairs, per_seq),
        in_specs=[cur(q0), cur(k0), prev(k0), cur(v0), prev(v0),
                  _resident((PAIR * BLOCK, 2 * BLOCK))],
        out_specs=pl.BlockSpec((DIL_SPAN, LANES), lambda b, p, i: (b * per_seq + i, p)),
        out_shape=jax.ShapeDtypeStruct((n, DIL_WIDTH), BF16),
        scratch_shapes=[pltpu.VMEM((DIL_SPAN, LANES), F32),
                        pltpu.VMEM((2 * DIL_SPAN, LANES), F32),
                        pltpu.VMEM((2 * DIL_SPAN, LANES), F32),
                        pltpu.VMEM((len(DILATIONS), DIL_SPAN, LANES), F32),
                        pltpu.VMEM((len(DILATIONS), DIL_SPAN, LANES), F32),
                        pltpu.VMEM((len(DILATIONS), DIL_SPAN, LANES), F32)],
        compiler_params=_cparams(3),
        name="dilated",
    )(proj, proj, proj, proj, proj, _band_bias())


SB_ROWS = 256
SB_PAIRS = 2
SB_TERMS = 1
SB_DEAD = -105.0


def _suffix_matrix():
    tri = np.arange(SB_TK)[:, None] > np.arange(SB_TK)[None, :]
    return jnp.asarray(np.concatenate([tri] * SB_TERMS, axis=0), dtype=BF16)


def _sb_visit(qs_ref, k_ref, v_ref, tri_ref, acc_ref, carry_ref, first_kb, n_blk, diagonal):
    start = pl.multiple_of(first_kb * SB_TK, SB_TK)
    width = n_blk * SB_TK
    tri = tri_ref[...]
    for rc in range(SB_PAIRS * PAIR * SB_TQ // SB_ROWS):
        rows = slice(rc * SB_ROWS, (rc + 1) * SB_ROWS)
        pair = (rc * SB_ROWS) // (PAIR * SB_TQ)
        kwin = k_ref[pl.ds(start, width), pair * LANES:(pair + 1) * LANES]
        vwin = v_ref[pl.ds(start, width), pair * LANES:(pair + 1) * LANES]
        z = lax.dot_general(qs_ref[rows, :], kwin, (((1,), (1,)), ((), ())),
                            preferred_element_type=F32)
        log_stay = -(jnp.maximum(z, 0.0) + jnp.log(1.0 + jnp.exp(-jnp.abs(z))))
        if diagonal:
            q_row = (rc * SB_ROWS) % SB_TQ + lax.broadcasted_iota(jnp.int32, z.shape, 0)
            causal = lax.broadcasted_iota(jnp.int32, z.shape, 1) < q_row + (width - SB_TK)
            log_stay = jnp.where(causal, log_stay, 0.0)
        terms = [log_stay.astype(BF16)]
        for _ in range(SB_TERMS - 1):
            rest = log_stay - sum(t.astype(F32) for t in terms)
            terms.append(rest.astype(BF16))
        blocks = [slice(b * SB_TK, (b + 1) * SB_TK) for b in range(n_blk)]
        split = jnp.concatenate(
            [jnp.concatenate([t[:, c] for t in terms], axis=1) for c in blocks], axis=0)
        within = jnp.dot(split, tri, preferred_element_type=F32)
        after = None if diagonal else carry_ref[rows, :]
        later = [None] * n_blk
        for b in reversed(range(n_blk)):
            later[b] = within[b * SB_ROWS:(b + 1) * SB_ROWS]
            if after is not None:
                later[b] = later[b] + jnp.concatenate([after] * (SB_TK // LANES), axis=1)
            total = jnp.broadcast_to(jnp.sum(log_stay[:, blocks[b]], axis=-1, keepdims=True),
                                     (SB_ROWS, LANES))
            after = total if after is None else after + total
        a = jnp.exp(z + log_stay + jnp.concatenate(later, axis=1))
        if diagonal:
            a = jnp.where(causal, a, 0.0)
        pv = jnp.dot(a.astype(BF16), vwin, preferred_element_type=F32)
        if diagonal:
            acc_ref[rows, :] = pv
        else:
            acc_ref[rows, :] += pv
        carry_ref[rows, :] = after


def _sb_body(q_ref, k_ref, v_ref, tri_ref, o_ref, qs_ref, acc_ref, carry_ref):
    qi = pl.program_id(2)
    stacked = PAIR * SB_TQ
    for pair in range(SB_PAIRS):
        qs_ref[pair * stacked:(pair + 1) * stacked, :] = _stack_heads(
            q_ref[:, pair * LANES:(pair + 1) * LANES])
    visit = functools.partial(_sb_visit, qs_ref, k_ref, v_ref, tri_ref, acc_ref, carry_ref)

    @pl.when(qi == 0)
    def _():
        visit(0, 1, True)

    @pl.when(qi > 0)
    def _():
        visit(qi - 1, 2, True)

    def alive(state):
        t, top = state
        return (t <= qi) & (top > SB_DEAD)

    def step(state):
        t, _ = state
        visit(qi - t, 1, False)
        return t + 1, jnp.max(carry_ref[...])

    lax.while_loop(alive, step, (2, jnp.max(carry_ref[...])))
    for pair in range(SB_PAIRS):
        o_ref[:, pair * LANES:(pair + 1) * LANES] = _unstack_heads(
            acc_ref[pair * stacked:(pair + 1) * stacked, :]).astype(BF16)


def _stick_breaking(proj, batch, seq):
    n = proj.shape[0]
    groups = SB_HEADS // PAIR // SB_PAIRS
    per_seq = seq // SB_TQ
    wide = SB_PAIRS * LANES
    qo = lambda b, p, i: (b * per_seq + i, p)
    return pl.pallas_call(
        _sb_body,
        grid=(batch, groups, per_seq),
        in_specs=[pl.BlockSpec((SB_TQ, wide), qo),
                  pl.BlockSpec((seq, wide), lambda b, p, i: (b, groups + p)),
                  pl.BlockSpec((seq, wide), lambda b, p, i: (b, 2 * groups + p)),
                  _resident((SB_TERMS * SB_TK, SB_TK))],
        out_specs=pl.BlockSpec((SB_TQ, wide), qo),
        out_shape=jax.ShapeDtypeStruct((n, SB_WIDTH), BF16),
        scratch_shapes=[pltpu.VMEM((SB_PAIRS * PAIR * SB_TQ, LANES), BF16),
                        pltpu.VMEM((SB_PAIRS * PAIR * SB_TQ, LANES), F32),
                        pltpu.VMEM((SB_PAIRS * PAIR * SB_TQ, LANES), F32)],
        compiler_params=_cparams(3),
        name="stick_breaking",
    )(proj, proj, proj, _suffix_matrix())


HYB_PLAN = (("ret_rot", 1.0), ("ret_rot", HEAD_DIM ** -0.5), ("plain", 1.0), ("plain", 1.0),
            ("dil_rot", HEAD_DIM ** -0.5), ("dil_rot", 1.0), ("plain", 1.0))
SB_PLAN = (("plain", HEAD_DIM ** -0.5),) * 2 + (("plain", 1.0),) * 4


def kernel(x, ffn1_norm, ffn1_w_in, ffn1_w_out, mix_norm, ffn2_norm, ffn2_w_in, ffn2_w_out,
           hyb_w_in, ret_gn, hyb_w_out, sb_w_in, sb_w_out, final_norm):
    batch, seq, _ = x.shape
    depth = ffn1_norm.shape[0]
    tables = (_rotary_tables(seq, HEAD_DIM, RET_ROPE_THETA)
              + _rotary_tables(seq, ROPE_DIM, ROPE_THETA))
    (ffn1_w_in, ffn1_w_out, ffn2_w_in, ffn2_w_out, hyb_w_in, hyb_w_out, sb_w_in, sb_w_out) = (
        w.astype(BF16) for w in (ffn1_w_in, ffn1_w_out, ffn2_w_in, ffn2_w_out,
                                 hyb_w_in, hyb_w_out, sb_w_in, sb_w_out))
    h = x.reshape(batch * seq, D_MODEL)
    for layer in range(depth):
        h = _ffn(h, ffn1_norm[layer], ffn1_w_in, ffn1_w_out, layer)
        if layer % 2 == 0:
            e = layer // 2
            proj = _proj(h, mix_norm[layer], hyb_w_in, e, HYB_PLAN, tables, seq)
            mix = ([_retention(proj, ret_gn[e], batch, seq),
                    _dilated_attention(proj, batch, seq)], hyb_w_out, e)
        else:
            o = layer // 2
            proj = _proj(h, mix_norm[layer], sb_w_in, o, SB_PLAN, (), seq)
            mix = ([_stick_breaking(proj, batch, seq)], sb_w_out, o)
        h = _ffn(h, ffn2_norm[layer], ffn2_w_in, ffn2_w_out, layer, mix,
                 final_norm if layer == depth - 1 else None)
    return h.reshape(batch, seq, D_MODEL)
```

```python
import functools

import numpy as np
import jax
import jax.numpy as jnp
from jax import lax
from jax.experimental import pallas as pl
from jax.experimental.pallas import tpu as pltpu

F32 = jnp.float32
BF16 = jnp.bfloat16

D_MODEL = 1024
HEAD_DIM = 64
RET_HEADS = 8
DIL_HEADS = 8
SB_HEADS = 16
RET_WIDTH = RET_HEADS * HEAD_DIM
DIL_WIDTH = DIL_HEADS * HEAD_DIM
SB_WIDTH = SB_HEADS * HEAD_DIM
HYB_IN = 4 * RET_WIDTH + 3 * DIL_WIDTH
D_FF = 2816
BLOCK = 128
RET_CHUNK = 128
RET_ROPE_THETA = 10000.0
ROPE_THETA = 500000.0
ROPE_DIM = HEAD_DIM // 4
DILATIONS = (1, 4, 16)
NORM_EPS = 1e-6
GN_EPS = 1e-5

LANES = 128
PAIR = LANES // HEAD_DIM
VMEM_LIMIT = 56 * 1024 * 1024
NEG_BIG = -1e30
LOG2E = 1.4426950408889634

TOK_TILE = 1024
FFN_TILE = 1024
FF_CHUNK = 256
SB_TQ = 256
SB_TK = 256


def _cparams(n_axes):
    return pltpu.CompilerParams(
        dimension_semantics=("arbitrary",) * n_axes,
        vmem_limit_bytes=VMEM_LIMIT)


def _resident(shape, layer=None):
    if layer is None:
        return pl.BlockSpec(shape, lambda *_: (0,) * len(shape), pipeline_mode=pl.Buffered(1))
    return pl.BlockSpec((None,) + tuple(shape), lambda *_: (layer,) + (0,) * len(shape),
                        pipeline_mode=pl.Buffered(1))


def _rms(x, g):
    return x * lax.rsqrt(jnp.mean(x * x, axis=-1, keepdims=True) + NORM_EPS) * g


def _silu(x):
    return x / (1.0 + jnp.exp(-x))


def _lane_is_head0(shape):
    return lax.broadcasted_iota(jnp.int32, shape, len(shape) - 1) < HEAD_DIM


def _stack_heads(q):
    h0 = _lane_is_head0(q.shape)
    zero = jnp.zeros_like(q)
    return jnp.concatenate([jnp.where(h0, q, zero), jnp.where(h0, zero, q)], axis=0)


def _unstack_heads(o2):
    t = o2.shape[0] // 2
    return jnp.where(_lane_is_head0((t, LANES)), o2[:t], o2[t:])


def _ffn_body(*refs, n_slabs, final_norm):
    refs = list(refs)
    x_ref = refs.pop(0)
    slabs = [refs.pop(0) for _ in range(n_slabs)]
    wmix_ref = refs.pop(0) if n_slabs else None
    g_ref, win_ref, wout_ref = refs.pop(0), refs.pop(0), refs.pop(0)
    fg_ref = refs.pop(0) if final_norm else None
    o_ref, act_ref = refs

    x = x_ref[...]
    lo = 0
    for a_ref in slabs:
        width = a_ref.shape[1]
        x = x + jnp.dot(a_ref[...], wmix_ref[lo:lo + width, :], preferred_element_type=F32)
        lo += width
    hb = _rms(x, g_ref[...]).astype(BF16)
    for c in range(D_FF // FF_CHUNK):
        cols = slice(c * FF_CHUNK, (c + 1) * FF_CHUNK)
        up_cols = slice(D_FF + c * FF_CHUNK, D_FF + (c + 1) * FF_CHUNK)
        gate = jnp.dot(hb, win_ref[:, cols], preferred_element_type=F32)
        up = jnp.dot(hb, win_ref[:, up_cols], preferred_element_type=F32)
        act_ref[:, cols] = (_silu(gate) * up).astype(BF16)
    y = x + 0.5 * jnp.dot(act_ref[...], wout_ref[...], preferred_element_type=F32)
    if final_norm:
        y = _rms(y, fg_ref[...])
    o_ref[...] = y


def _ffn(x, g, w_in, w_out, layer, mix=None, final_g=None):
    n = x.shape[0]
    row = pl.BlockSpec((FFN_TILE, D_MODEL), lambda i: (i, 0))
    args, specs = [x], [row]
    slabs = ()
    if mix is not None:
        slabs, w_mix, mix_layer = mix
        args += [*slabs, w_mix]
        specs += [pl.BlockSpec((FFN_TILE, a.shape[1]), lambda i: (i, 0)) for a in slabs]
        specs += [_resident(w_mix.shape[1:], mix_layer)]
    args += [g.reshape(1, D_MODEL), w_in, w_out]
    specs += [_resident((1, D_MODEL)), _resident((D_MODEL, 2 * D_FF), layer),
              _resident((D_FF, D_MODEL), layer)]
    if final_g is not None:
        args.append(final_g.reshape(1, D_MODEL))
        specs.append(_resident((1, D_MODEL)))
    return pl.pallas_call(
        functools.partial(_ffn_body, n_slabs=len(slabs), final_norm=final_g is not None),
        grid=(n // FFN_TILE,),
        in_specs=specs,
        out_specs=row,
        out_shape=jax.ShapeDtypeStruct((n, D_MODEL), F32),
        scratch_shapes=[pltpu.VMEM((FFN_TILE, D_FF), BF16)],
        compiler_params=_cparams(1),
        name="ffn",
    )(*args)


PROJ_GROUP = 512


def _rotate(y, cos, sin_a, sin_b, half):
    return (y * cos + pltpu.roll(y, LANES - half, 1) * sin_a + pltpu.roll(y, half, 1) * sin_b)


def _proj_body(x_ref, g_ref, w_ref, *rest, plan):
    tables, o_ref = rest[:-1], rest[-1]
    hb = _rms(x_ref[...], g_ref[...]).astype(BF16)
    for gi, (kind, scale) in enumerate(plan):
        lo = gi * PROJ_GROUP
        y = jnp.dot(hb, w_ref[:, lo:lo + PROJ_GROUP], preferred_element_type=F32)
        for s in range(PROJ_GROUP // LANES):
            ys = y[:, s * LANES:(s + 1) * LANES]
            if kind == "ret_rot":
                ys = _rotate(ys, tables[0][...], tables[1][...], tables[2][...], HEAD_DIM // 2)
            elif kind == "dil_rot":
                ys = _rotate(ys, tables[3][...], tables[4][...], tables[5][...], ROPE_DIM // 2)
            if scale != 1.0:
                ys = ys * scale
            o_ref[:, lo + s * LANES:lo + (s + 1) * LANES] = ys.astype(BF16)


def _proj(x, g, w, layer, plan, tables, seq):
    n = x.shape[0]
    width = w.shape[2]
    row = pl.BlockSpec((TOK_TILE, D_MODEL), lambda i: (i, 0))
    per_seq = seq // TOK_TILE
    tab = pl.BlockSpec((TOK_TILE, LANES), lambda i: (i % per_seq, 0))
    return pl.pallas_call(
        functools.partial(_proj_body, plan=plan),
        grid=(n // TOK_TILE,),
        in_specs=[row, _resident((1, D_MODEL)), _resident((D_MODEL, width), layer)]
        + [tab] * len(tables),
        out_specs=pl.BlockSpec((TOK_TILE, width), lambda i: (i, 0)),
        out_shape=jax.ShapeDtypeStruct((n, width), BF16),
        compiler_params=_cparams(1),
        name="proj",
    )(x, g.reshape(1, D_MODEL), w, *tables)


def _rotary_tables(seq, rot_dim, theta):
    half = rot_dim // 2
    lane = np.arange(LANES) % HEAD_DIM
    inv_freq = 1.0 / (theta ** (np.arange(half, dtype=np.float64) / half))
    ang = (np.arange(seq, dtype=np.float64)[:, None] * inv_freq[None, :])[:, lane % half]
    cos, sin = np.cos(ang), np.sin(ang)
    is_x1 = (lane < half)[None, :]
    is_x2 = ((lane >= half) & (lane < rot_dim))[None, :]
    cos_t = np.where(is_x1 | is_x2, cos, 1.0)
    sin_a = np.where(is_x1, -sin, 0.0)
    sin_b = np.where(is_x2, sin, 0.0)
    return tuple(jnp.asarray(t, dtype=F32) for t in (cos_t, sin_a, sin_b))


RET_TOK = 512


def _retention_consts():
    h = np.arange(RET_HEADS, dtype=np.float64)
    log_g = np.log(1.0 - 2.0 ** (-5.0 - h))
    i = np.arange(RET_CHUNK, dtype=np.float64)
    diff = i[:, None] - i[None, :]
    decay_in = np.where(diff >= 0, np.exp(np.maximum(diff, 0.0)[None] * log_g[:, None, None]), 0.0)
    n_pairs = RET_HEADS // PAIR
    decay = decay_in.reshape(n_pairs, PAIR * RET_CHUNK, RET_CHUNK)
    lane_head = np.arange(LANES) // HEAD_DIM
    head_of = (np.arange(n_pairs)[:, None] * PAIR + lane_head[None, :])
    lg = log_g[head_of]
    q_dec = np.exp((i + 1.0)[None, :, None] * lg[:, None, :])
    k_dec = np.exp((RET_CHUNK - 1.0 - i)[None, :, None] * lg[:, None, :])
    same_head = (lane_head[:, None] == lane_head[None, :])
    c_dec = np.exp(RET_CHUNK * lg)[:, :, None] * same_head[None]
    bd = np.broadcast_to(same_head[None], c_dec.shape)
    return tuple(jnp.asarray(t, dtype=F32) for t in (decay, q_dec, k_dec, c_dec, bd))


def _retention_body(q_ref, k_ref, v_ref, g_ref, dec_ref, qd_ref, kd_ref, cd_ref, bd_ref, gn_ref,
                    o_ref, state_ref):
    @pl.when(pl.program_id(1) == 0)
    def _():
        state_ref[...] = jnp.zeros_like(state_ref)

    h0 = _lane_is_head0((RET_CHUNK, LANES))
    inv = 1.0 / HEAD_DIM
    for c in range(RET_TOK // RET_CHUNK):
        rows = slice(c * RET_CHUNK, (c + 1) * RET_CHUNK)
        for p in range(RET_HEADS // PAIR):
            cols = slice(p * LANES, (p + 1) * LANES)
            q, k, v = q_ref[rows, cols], k_ref[rows, cols], v_ref[rows, cols]
            sc = lax.dot_general(_stack_heads(q), k, (((1,), (1,)), ((), ())),
                                 preferred_element_type=F32) * dec_ref[p]
            inner = _unstack_heads(jnp.dot(sc.astype(BF16), v, preferred_element_type=F32))
            state = state_ref[p]
            qd = (q.astype(F32) * qd_ref[p]).astype(BF16)
            o = inner + jnp.dot(qd, state.astype(BF16), preferred_element_type=F32)
            kd = (k.astype(F32) * kd_ref[p]).astype(BF16)
            kv = lax.dot_general(kd, v, (((0,), (0,)), ((), ())), preferred_element_type=F32)
            state_ref[p] = state * cd_ref[p] + kv * bd_ref[p]
            s0 = jnp.sum(jnp.where(h0, o, 0.0), axis=-1, keepdims=True)
            s1 = jnp.sum(jnp.where(h0, 0.0, o), axis=-1, keepdims=True)
            d = o - jnp.where(h0, s0, s1) * inv
            d2 = d * d
            v0 = jnp.sum(jnp.where(h0, d2, 0.0), axis=-1, keepdims=True)
            v1 = jnp.sum(jnp.where(h0, 0.0, d2), axis=-1, keepdims=True)
            y = d * lax.rsqrt(jnp.where(h0, v0, v1) * inv + GN_EPS) * gn_ref[:, cols]
            o_ref[rows, cols] = (y * _silu(g_ref[rows, cols].astype(F32))).astype(BF16)


def _retention(proj, ret_gn, batch, seq):
    n = proj.shape[0]
    per_seq = seq // RET_TOK
    consts = _retention_consts()

    def slab(j):
        return pl.BlockSpec((RET_TOK, RET_WIDTH), lambda b, i, j=j: (b * per_seq + i, j))

    return pl.pallas_call(
        _retention_body,
        grid=(batch, per_seq),
        in_specs=[slab(0), slab(1), slab(2), slab(3)]
        + [_resident(c.shape) for c in consts] + [_resident((1, RET_WIDTH))],
        out_specs=pl.BlockSpec((RET_TOK, RET_WIDTH), lambda b, i: (b * per_seq + i, 0)),
        out_shape=jax.ShapeDtypeStruct((n, RET_WIDTH), BF16),
        scratch_shapes=[pltpu.VMEM((RET_HEADS // PAIR, LANES, LANES), F32)],
        compiler_params=_cparams(2),
        name="retention",
    )(proj, proj, proj, proj, *consts, ret_gn.reshape(1, RET_WIDTH))


DIL_SPAN = BLOCK * max(DILATIONS)


def _band_bias():
    qi = np.arange(PAIR * BLOCK)[:, None] % BLOCK
    kj = np.arange(2 * BLOCK)[None, :]
    return jnp.asarray(np.where((kj >= qi) & (kj <= qi + BLOCK), 0.0, NEG_BIG), dtype=F32)


def _dilated_body(q_ref, kc_ref, kp_ref, vc_ref, vp_ref, bias_ref, out_ref,
                  qf_ref, kf_ref, vf_ref, num_ref, den_ref, max_ref):
    first = pl.program_id(2) == 0
    qf_ref[...] = q_ref[...].astype(F32)
    kf_ref[0:DIL_SPAN, :] = kp_ref[...].astype(F32)
    kf_ref[DIL_SPAN:, :] = kc_ref[...].astype(F32)
    vf_ref[0:DIL_SPAN, :] = vp_ref[...].astype(F32)
    vf_ref[DIL_SPAN:, :] = vc_ref[...].astype(F32)
    k_min = jnp.where(first, BLOCK, 0)
    kj = lax.broadcasted_iota(jnp.int32, (PAIR * BLOCK, 2 * BLOCK), 1)
    ones = jnp.ones((2 * BLOCK, LANES), BF16)
    head0 = _lane_is_head0((BLOCK, LANES))

    for pi, dil in enumerate(DILATIONS):
        for r in range(dil):
            for n in range(DIL_SPAN // (BLOCK * dil)):
                q0 = n * (BLOCK * dil) + r
                k0 = q0 + DIL_SPAN - BLOCK * dil
                q = qf_ref[pl.ds(q0, BLOCK, stride=dil), :].astype(BF16)
                keys = kf_ref[pl.ds(k0, 2 * BLOCK, stride=dil), :].astype(BF16)
                vals = vf_ref[pl.ds(k0, 2 * BLOCK, stride=dil), :].astype(BF16)
                s = lax.dot_general(_stack_heads(q), keys, (((1,), (1,)), ((), ())),
                                    preferred_element_type=F32) + bias_ref[...]
                if n == 0:
                    s = jnp.where(kj >= k_min, s, NEG_BIG)
                m = jnp.max(s, axis=-1, keepdims=True)
                p = jnp.exp2(s - m)
                pv = jnp.dot(p.astype(BF16), jnp.concatenate([vals, ones], axis=1),
                             preferred_element_type=F32)
                rows = pl.ds(q0, BLOCK, stride=dil)
                num_ref[pi, rows, :] = _unstack_heads(pv[:, :LANES])
                den_ref[pi, rows, :] = _unstack_heads(pv[:, LANES:])
                max_ref[pi, rows, :] = jnp.where(head0, m[:BLOCK], m[BLOCK:])

    m = jnp.maximum(jnp.maximum(max_ref[0], max_ref[1]), max_ref[2])
    e = [jnp.exp2(max_ref[pi] - m) for pi in range(len(DILATIONS))]
    num = e[0] * num_ref[0] + e[1] * num_ref[1] + e[2] * num_ref[2]
    den = e[0] * den_ref[0] + e[1] * den_ref[1] + e[2] * den_ref[2]
    out_ref[...] = (num / den).astype(BF16)


def _dilated_attention(proj, batch, seq):
    n = proj.shape[0]
    q0, k0, v0 = ((4 * RET_WIDTH + t * DIL_WIDTH) // LANES for t in range(3))
    per_seq = seq // DIL_SPAN
    n_pairs = DIL_HEADS // PAIR

    def cur(c0):
        return pl.BlockSpec((DIL_SPAN, LANES), lambda b, p, i: (b * per_seq + i, c0 + p))

    def prev(c0):
        return pl.BlockSpec((DIL_SPAN, LANES),
                            lambda b, p, i: (jnp.maximum(b * per_seq + i - 1, 0), c0 + p))

    return pl.pallas_call(
        _dilated_body,
        grid=(batch, n_pairs, per_seq),
        in_specs=[cur(q0), cur(k0), prev(k0), cur(v0), prev(v0),
                  _resident((PAIR * BLOCK, 2 * BLOCK))],
        out_specs=pl.BlockSpec((DIL_SPAN, LANES), lambda b, p, i: (b * per_seq + i, p)),
        out_shape=jax.ShapeDtypeStruct((n, DIL_WIDTH), BF16),
        scratch_shapes=[pltpu.VMEM((DIL_SPAN, LANES), F32),
                        pltpu.VMEM((2 * DIL_SPAN, LANES), F32),
                        pltpu.VMEM((2 * DIL_SPAN, LANES), F32),
                        pltpu.VMEM((len(DILATIONS), DIL_SPAN, LANES), F32),
                        pltpu.VMEM((len(DILATIONS), DIL_SPAN, LANES), F32),
                        pltpu.VMEM((len(DILATIONS), DIL_SPAN, LANES), F32)],
        compiler_params=_cparams(3),
        name="dilated",
    )(proj, proj, proj, proj, proj, _band_bias())


SB_ROWS = 512
SB_PAIRS = 4
SB_TERMS = 1
SB_DEAD = -105.0


def _suffix_matrix():
    tri = np.arange(SB_TK)[:, None] > np.arange(SB_TK)[None, :]
    return jnp.asarray(np.concatenate([tri] * SB_TERMS, axis=0), dtype=BF16)


def _sb_visit(qs_ref, k_ref, v_ref, tri_ref, acc_ref, carry_ref, first_kb, n_blk, diagonal):
    start = pl.multiple_of(first_kb * SB_TK, SB_TK)
    width = n_blk * SB_TK
    tri = tri_ref[...]
    for rc in range(SB_PAIRS * PAIR * SB_TQ // SB_ROWS):
        rows = slice(rc * SB_ROWS, (rc + 1) * SB_ROWS)
        pair = (rc * SB_ROWS) // (PAIR * SB_TQ)
        kwin = k_ref[pl.ds(start, width), pair * LANES:(pair + 1) * LANES]
        vwin = v_ref[pl.ds(start, width), pair * LANES:(pair + 1) * LANES]
        z = lax.dot_general(qs_ref[rows, :], kwin, (((1,), (1,)), ((), ())),
                            preferred_element_type=F32)
        log_go = jnp.minimum(z, 0.0) - jnp.log(1.0 + jnp.exp2(jnp.abs(z) * -LOG2E))
        log_stay = log_go - z
        if diagonal:
            q_row = (rc * SB_ROWS + lax.broadcasted_iota(jnp.int32, z.shape, 0)) & (SB_TQ - 1)
            causal = lax.broadcasted_iota(jnp.int32, z.shape, 1) < q_row + (width - SB_TK)
            log_stay = jnp.where(causal, log_stay, 0.0)
        terms = [log_stay.astype(BF16)]
        for _ in range(SB_TERMS - 1):
            rest = log_stay - sum(t.astype(F32) for t in terms)
            terms.append(rest.astype(BF16))
        blocks = [slice(b * SB_TK, (b + 1) * SB_TK) for b in range(n_blk)]
        split = jnp.concatenate(
            [jnp.concatenate([t[:, c] for t in terms], axis=1) for c in blocks], axis=0)
        within = jnp.dot(split, tri, preferred_element_type=F32)
        after = None if diagonal else carry_ref[rows, :]
        later = [None] * n_blk
        for b in reversed(range(n_blk)):
            later[b] = within[b * SB_ROWS:(b + 1) * SB_ROWS]
            if after is not None:
                later[b] = later[b] + jnp.concatenate([after] * (SB_TK // LANES), axis=1)
            total = jnp.broadcast_to(jnp.sum(log_stay[:, blocks[b]], axis=-1, keepdims=True),
                                     (SB_ROWS, LANES))
            after = total if after is None else after + total
        a = jnp.exp(log_go + jnp.concatenate(later, axis=1))
        if diagonal:
            a = jnp.where(causal, a, 0.0)
        pv = jnp.dot(a.astype(BF16), vwin, preferred_element_type=F32)
        if diagonal:
            acc_ref[rows, :] = pv
        else:
            acc_ref[rows, :] += pv
        carry_ref[rows, :] = after


def _sb_body(q_ref, k_ref, v_ref, tri_ref, o_ref, qs_ref, acc_ref, carry_ref):
    qi = pl.program_id(2)
    stacked = PAIR * SB_TQ
    for pair in range(SB_PAIRS):
        qs_ref[pair * stacked:(pair + 1) * stacked, :] = _stack_heads(
            q_ref[:, pair * LANES:(pair + 1) * LANES])
    visit = functools.partial(_sb_visit, qs_ref, k_ref, v_ref, tri_ref, acc_ref, carry_ref)

    @pl.when(qi == 0)
    def _():
        visit(0, 1, True)

    @pl.when(qi > 0)
    def _():
        visit(qi - 1, 2, True)

    def alive(state):
        t, top = state
        return (t <= qi) & (top > SB_DEAD)

    def step(state):
        t, _ = state
        visit(qi - t, 1, False)
        return t + 1, jnp.max(carry_ref[...])

    lax.while_loop(alive, step, (2, jnp.max(carry_ref[...])))
    for pair in range(SB_PAIRS):
        o_ref[:, pair * LANES:(pair + 1) * LANES] = _unstack_heads(
            acc_ref[pair * stacked:(pair + 1) * stacked, :]).astype(BF16)


def _stick_breaking(proj, batch, seq):
    n = proj.shape[0]
    groups = SB_HEADS // PAIR // SB_PAIRS
    per_seq = seq // SB_TQ
    wide = SB_PAIRS * LANES
    qo = lambda b, p, i: (b * per_seq + i, p)
    return pl.pallas_call(
        _sb_body,
        grid=(batch, groups, per_seq),
        in_specs=[pl.BlockSpec((SB_TQ, wide), qo),
                  pl.BlockSpec((seq, wide), lambda b, p, i: (b, groups + p)),
                  pl.BlockSpec((seq, wide), lambda b, p, i: (b, 2 * groups + p)),
                  _resident((SB_TERMS * SB_TK, SB_TK))],
        out_specs=pl.BlockSpec((SB_TQ, wide), qo),
        out_shape=jax.ShapeDtypeStruct((n, SB_WIDTH), BF16),
        scratch_shapes=[pltpu.VMEM((SB_PAIRS * PAIR * SB_TQ, LANES), BF16),
                        pltpu.VMEM((SB_PAIRS * PAIR * SB_TQ, LANES), F32),
                        pltpu.VMEM((SB_PAIRS * PAIR * SB_TQ, LANES), F32)],
        compiler_params=_cparams(3),
        name="stick_breaking",
    )(proj, proj, proj, _suffix_matrix())


HYB_PLAN = (("ret_rot", 1.0), ("ret_rot", HEAD_DIM ** -0.5), ("plain", 1.0), ("plain", 1.0),
            ("dil_rot", HEAD_DIM ** -0.5 * LOG2E), ("dil_rot", 1.0), ("plain", 1.0))
SB_PLAN = (("plain", HEAD_DIM ** -0.5),) * 2 + (("plain", 1.0),) * 4


def kernel(x, ffn1_norm, ffn1_w_in, ffn1_w_out, mix_norm, ffn2_norm, ffn2_w_in, ffn2_w_out,
           hyb_w_in, ret_gn, hyb_w_out, sb_w_in, sb_w_out, final_norm):
    batch, seq, _ = x.shape
    depth = ffn1_norm.shape[0]
    tables = (_rotary_tables(seq, HEAD_DIM, RET_ROPE_THETA)
              + _rotary_tables(seq, ROPE_DIM, ROPE_THETA))
    (ffn1_w_in, ffn1_w_out, ffn2_w_in, ffn2_w_out, hyb_w_in, hyb_w_out, sb_w_in, sb_w_out) = (
        w.astype(BF16) for w in (ffn1_w_in, ffn1_w_out, ffn2_w_in, ffn2_w_out,
                                 hyb_w_in, hyb_w_out, sb_w_in, sb_w_out))
    h = x.reshape(batch * seq, D_MODEL)
    for layer in range(depth):
        h = _ffn(h, ffn1_norm[layer], ffn1_w_in, ffn1_w_out, layer)
        if layer % 2 == 0:
            e = layer // 2
            proj = _proj(h, mix_norm[layer], hyb_w_in, e, HYB_PLAN, tables, seq)
            mix = ([_retention(proj, ret_gn[e], batch, seq),
                    _dilated_attention(proj, batch, seq)], hyb_w_out, e)
        else:
            o = layer // 2
            proj = _proj(h, mix_norm[layer], sb_w_in, o, SB_PLAN, (), seq)
            mix = ([_stick_breaking(proj, batch, seq)], sb_w_out, o)
        h = _ffn(h, ffn2_norm[layer], ffn2_w_in, ffn2_w_out, layer, mix,
                 final_norm if layer == depth - 1 else None)
    return h.reshape(batch, seq, D_MODEL)
```

```python
import functools

import numpy as np
import jax
import jax.numpy as jnp
from jax import lax
from jax.experimental import pallas as pl
from jax.experimental.pallas import tpu as pltpu

F32 = jnp.float32
BF16 = jnp.bfloat16

D_MODEL = 1024
HEAD_DIM = 64
RET_HEADS = 8
DIL_HEADS = 8
SB_HEADS = 16
RET_WIDTH = RET_HEADS * HEAD_DIM
DIL_WIDTH = DIL_HEADS * HEAD_DIM
SB_WIDTH = SB_HEADS * HEAD_DIM
HYB_IN = 4 * RET_WIDTH + 3 * DIL_WIDTH
D_FF = 2816
BLOCK = 128
RET_CHUNK = 128
RET_ROPE_THETA = 10000.0
ROPE_THETA = 500000.0
ROPE_DIM = HEAD_DIM // 4
DILATIONS = (1, 4, 16)
NORM_EPS = 1e-6
GN_EPS = 1e-5

LANES = 128
PAIR = LANES // HEAD_DIM
VMEM_LIMIT = 56 * 1024 * 1024
NEG_BIG = -1e30
LOG2E = 1.4426950408889634

TOK_TILE = 1024
FFN_TILE = 1024
FF_CHUNK = 256
SB_TQ = 256
SB_TK = 256


def _cparams(n_axes):
    return pltpu.CompilerParams(
        dimension_semantics=("arbitrary",) * n_axes,
        vmem_limit_bytes=VMEM_LIMIT)


def _resident(shape, layer=None):
    if layer is None:
        return pl.BlockSpec(shape, lambda *_: (0,) * len(shape), pipeline_mode=pl.Buffered(1))
    return pl.BlockSpec((None,) + tuple(shape), lambda *_: (layer,) + (0,) * len(shape),
                        pipeline_mode=pl.Buffered(1))


def _rms(x, g):
    return x * lax.rsqrt(jnp.mean(x * x, axis=-1, keepdims=True) + NORM_EPS) * g


def _silu(x):
    return x / (1.0 + jnp.exp(-x))


def _lane_is_head0(shape):
    return lax.broadcasted_iota(jnp.int32, shape, len(shape) - 1) < HEAD_DIM


def _stack_heads(q):
    h0 = _lane_is_head0(q.shape)
    zero = jnp.zeros_like(q)
    return jnp.concatenate([jnp.where(h0, q, zero), jnp.where(h0, zero, q)], axis=0)


def _unstack_heads(o2):
    t = o2.shape[0] // 2
    return jnp.where(_lane_is_head0((t, LANES)), o2[:t], o2[t:])


def _ffn_body(*refs, n_slabs, final_norm):
    refs = list(refs)
    x_ref = refs.pop(0)
    slabs = [refs.pop(0) for _ in range(n_slabs)]
    wmix_ref = refs.pop(0) if n_slabs else None
    g_ref, win_ref, wout_ref = refs.pop(0), refs.pop(0), refs.pop(0)
    fg_ref = refs.pop(0) if final_norm else None
    o_ref, act_ref = refs

    x = x_ref[...]
    lo = 0
    for a_ref in slabs:
        width = a_ref.shape[1]
        x = x + jnp.dot(a_ref[...], wmix_ref[lo:lo + width, :], preferred_element_type=F32)
        lo += width
    hb = _rms(x, g_ref[...]).astype(BF16)
    for c in range(D_FF // FF_CHUNK):
        cols = slice(c * FF_CHUNK, (c + 1) * FF_CHUNK)
        up_cols = slice(D_FF + c * FF_CHUNK, D_FF + (c + 1) * FF_CHUNK)
        gate = jnp.dot(hb, win_ref[:, cols], preferred_element_type=F32)
        up = jnp.dot(hb, win_ref[:, up_cols], preferred_element_type=F32)
        act_ref[:, cols] = (_silu(gate) * up).astype(BF16)
    y = x + 0.5 * jnp.dot(act_ref[...], wout_ref[...], preferred_element_type=F32)
    if final_norm:
        y = _rms(y, fg_ref[...])
    o_ref[...] = y


def _ffn(x, g, w_in, w_out, layer, mix=None, final_g=None):
    n = x.shape[0]
    row = pl.BlockSpec((FFN_TILE, D_MODEL), lambda i: (i, 0))
    args, specs = [x], [row]
    slabs = ()
    if mix is not None:
        slabs, w_mix, mix_layer = mix
        args += [*slabs, w_mix]
        specs += [pl.BlockSpec((FFN_TILE, a.shape[1]), lambda i: (i, 0)) for a in slabs]
        specs += [_resident(w_mix.shape[1:], mix_layer)]
    args += [g.reshape(1, D_MODEL), w_in, w_out]
    specs += [_resident((1, D_MODEL)), _resident((D_MODEL, 2 * D_FF), layer),
              _resident((D_FF, D_MODEL), layer)]
    if final_g is not None:
        args.append(final_g.reshape(1, D_MODEL))
        specs.append(_resident((1, D_MODEL)))
    return pl.pallas_call(
        functools.partial(_ffn_body, n_slabs=len(slabs), final_norm=final_g is not None),
        grid=(n // FFN_TILE,),
        in_specs=specs,
        out_specs=row,
        out_shape=jax.ShapeDtypeStruct((n, D_MODEL), F32),
        scratch_shapes=[pltpu.VMEM((FFN_TILE, D_FF), BF16)],
        compiler_params=_cparams(1),
        name="ffn",
    )(*args)


PROJ_GROUP = 512


def _rotate(y, cos, sin_a, sin_b, half):
    return (y * cos + pltpu.roll(y, LANES - half, 1) * sin_a + pltpu.roll(y, half, 1) * sin_b)


def _proj_body(x_ref, g_ref, w_ref, *rest, plan):
    tables, o_ref = rest[:-1], rest[-1]
    hb = _rms(x_ref[...], g_ref[...]).astype(BF16)
    for gi, (kind, scale) in enumerate(plan):
        lo = gi * PROJ_GROUP
        y = jnp.dot(hb, w_ref[:, lo:lo + PROJ_GROUP], preferred_element_type=F32)
        for s in range(PROJ_GROUP // LANES):
            ys = y[:, s * LANES:(s + 1) * LANES]
            if kind == "ret_rot":
                ys = _rotate(ys, tables[0][...], tables[1][...], tables[2][...], HEAD_DIM // 2)
            elif kind == "dil_rot":
                ys = _rotate(ys, tables[3][...], tables[4][...], tables[5][...], ROPE_DIM // 2)
            if scale != 1.0:
                ys = ys * scale
            o_ref[:, lo + s * LANES:lo + (s + 1) * LANES] = ys.astype(BF16)


def _proj(x, g, w, layer, plan, tables, seq):
    n = x.shape[0]
    width = w.shape[2]
    row = pl.BlockSpec((TOK_TILE, D_MODEL), lambda i: (i, 0))
    per_seq = seq // TOK_TILE
    tab = pl.BlockSpec((TOK_TILE, LANES), lambda i: (i % per_seq, 0))
    return pl.pallas_call(
        functools.partial(_proj_body, plan=plan),
        grid=(n // TOK_TILE,),
        in_specs=[row, _resident((1, D_MODEL)), _resident((D_MODEL, width), layer)]
        + [tab] * len(tables),
        out_specs=pl.BlockSpec((TOK_TILE, width), lambda i: (i, 0)),
        out_shape=jax.ShapeDtypeStruct((n, width), BF16),
        compiler_params=_cparams(1),
        name="proj",
    )(x, g.reshape(1, D_MODEL), w, *tables)


def _rotary_tables(seq, rot_dim, theta):
    half = rot_dim // 2
    lane = np.arange(LANES) % HEAD_DIM
    inv_freq = 1.0 / (theta ** (np.arange(half, dtype=np.float64) / half))
    ang = (np.arange(seq, dtype=np.float64)[:, None] * inv_freq[None, :])[:, lane % half]
    cos, sin = np.cos(ang), np.sin(ang)
    is_x1 = (lane < half)[None, :]
    is_x2 = ((lane >= half) & (lane < rot_dim))[None, :]
    cos_t = np.where(is_x1 | is_x2, cos, 1.0)
    sin_a = np.where(is_x1, -sin, 0.0)
    sin_b = np.where(is_x2, sin, 0.0)
    return tuple(jnp.asarray(t, dtype=F32) for t in (cos_t, sin_a, sin_b))


RET_TOK = 1024


def _retention_consts():
    h = np.arange(RET_HEADS, dtype=np.float64)
    log_g = np.log(1.0 - 2.0 ** (-5.0 - h))
    i = np.arange(RET_CHUNK, dtype=np.float64)
    diff = i[:, None] - i[None, :]
    decay_in = np.where(diff >= 0, np.exp(np.maximum(diff, 0.0)[None] * log_g[:, None, None]), 0.0)
    n_pairs = RET_HEADS // PAIR
    decay = decay_in.reshape(n_pairs, PAIR * RET_CHUNK, RET_CHUNK)
    lane_head = np.arange(LANES) // HEAD_DIM
    head_of = (np.arange(n_pairs)[:, None] * PAIR + lane_head[None, :])
    lg = log_g[head_of]
    q_dec = np.exp((i + 1.0)[None, :, None] * lg[:, None, :])
    k_dec = np.exp((RET_CHUNK - 1.0 - i)[None, :, None] * lg[:, None, :])
    same_head = (lane_head[:, None] == lane_head[None, :])
    c_dec = np.exp(RET_CHUNK * lg)[:, :, None] * same_head[None]
    bd = np.broadcast_to(same_head[None], c_dec.shape)
    return tuple(jnp.asarray(t, dtype=F32) for t in (decay, q_dec, k_dec, c_dec, bd))


def _retention_body(q_ref, k_ref, v_ref, g_ref, dec_ref, qd_ref, kd_ref, cd_ref, bd_ref, gn_ref,
                    o_ref, state_ref):
    @pl.when(pl.program_id(1) == 0)
    def _():
        state_ref[...] = jnp.zeros_like(state_ref)

    h0 = _lane_is_head0((RET_CHUNK, LANES))
    inv = 1.0 / HEAD_DIM
    for p in range(RET_HEADS // PAIR):
        cols = slice(p * LANES, (p + 1) * LANES)
        state = state_ref[p]
        for c in range(RET_TOK // RET_CHUNK):
            rows = slice(c * RET_CHUNK, (c + 1) * RET_CHUNK)
            q, k, v = q_ref[rows, cols], k_ref[rows, cols], v_ref[rows, cols]
            sc = lax.dot_general(_stack_heads(q), k, (((1,), (1,)), ((), ())),
                                 preferred_element_type=F32) * dec_ref[p]
            inner = _unstack_heads(jnp.dot(sc.astype(BF16), v, preferred_element_type=F32))
            qd = (q.astype(F32) * qd_ref[p]).astype(BF16)
            o = inner + jnp.dot(qd, state.astype(BF16), preferred_element_type=F32)
            kd = (k.astype(F32) * kd_ref[p]).astype(BF16)
            kv = lax.dot_general(kd, v, (((0,), (0,)), ((), ())), preferred_element_type=F32)
            state = state * cd_ref[p] + kv * bd_ref[p]
            s0 = jnp.sum(jnp.where(h0, o, 0.0), axis=-1, keepdims=True)
            s1 = jnp.sum(jnp.where(h0, 0.0, o), axis=-1, keepdims=True)
            d = o - jnp.where(h0, s0, s1) * inv
            d2 = d * d
            v0 = jnp.sum(jnp.where(h0, d2, 0.0), axis=-1, keepdims=True)
            v1 = jnp.sum(jnp.where(h0, 0.0, d2), axis=-1, keepdims=True)
            y = d * lax.rsqrt(jnp.where(h0, v0, v1) * inv + GN_EPS) * gn_ref[:, cols]
            o_ref[rows, cols] = (y * _silu(g_ref[rows, cols].astype(F32))).astype(BF16)
        state_ref[p] = state


def _retention(proj, ret_gn, batch, seq):
    n = proj.shape[0]
    per_seq = seq // RET_TOK
    consts = _retention_consts()

    def slab(j):
        return pl.BlockSpec((RET_TOK, RET_WIDTH), lambda b, i, j=j: (b * per_seq + i, j))

    return pl.pallas_call(
        _retention_body,
        grid=(batch, per_seq),
        in_specs=[slab(0), slab(1), slab(2), slab(3)]
        + [_resident(c.shape) for c in consts] + [_resident((1, RET_WIDTH))],
        out_specs=pl.BlockSpec((RET_TOK, RET_WIDTH), lambda b, i: (b * per_seq + i, 0)),
        out_shape=jax.ShapeDtypeStruct((n, RET_WIDTH), BF16),
        scratch_shapes=[pltpu.VMEM((RET_HEADS // PAIR, LANES, LANES), F32)],
        compiler_params=_cparams(2),
        name="retention",
    )(proj, proj, proj, proj, *consts, ret_gn.reshape(1, RET_WIDTH))


DIL_SPAN = BLOCK * max(DILATIONS)


def _band_bias():
    qi = np.arange(PAIR * BLOCK)[:, None] % BLOCK
    kj = np.arange(2 * BLOCK)[None, :]
    return jnp.asarray(np.where((kj >= qi) & (kj <= qi + BLOCK), 0.0, NEG_BIG), dtype=F32)


def _dilated_body(q_ref, kc_ref, kp_ref, vc_ref, vp_ref, bias_ref, out_ref,
                  qf_ref, kf_ref, vf_ref, num_ref, den_ref, max_ref):
    first = pl.program_id(2) == 0
    qf_ref[...] = q_ref[...].astype(F32)
    kf_ref[0:DIL_SPAN, :] = kp_ref[...].astype(F32)
    kf_ref[DIL_SPAN:, :] = kc_ref[...].astype(F32)
    vf_ref[0:DIL_SPAN, :] = vp_ref[...].astype(F32)
    vf_ref[DIL_SPAN:, :] = vc_ref[...].astype(F32)
    k_min = jnp.where(first, BLOCK, 0)
    kj = lax.broadcasted_iota(jnp.int32, (PAIR * BLOCK, 2 * BLOCK), 1)
    ones = jnp.ones((2 * BLOCK, LANES), BF16)
    head0 = _lane_is_head0((BLOCK, LANES))

    for pi, dil in enumerate(DILATIONS):
        for r in range(dil):
            for n in range(DIL_SPAN // (BLOCK * dil)):
                q0 = n * (BLOCK * dil) + r
                k0 = q0 + DIL_SPAN - BLOCK * dil
                q = qf_ref[pl.ds(q0, BLOCK, stride=dil), :].astype(BF16)
                keys = kf_ref[pl.ds(k0, 2 * BLOCK, stride=dil), :].astype(BF16)
                vals = vf_ref[pl.ds(k0, 2 * BLOCK, stride=dil), :].astype(BF16)
                s = lax.dot_general(_stack_heads(q), keys, (((1,), (1,)), ((), ())),
                                    preferred_element_type=F32) + bias_ref[...]
                if n == 0:
                    s = jnp.where(kj >= k_min, s, NEG_BIG)
                m = jnp.max(s, axis=-1, keepdims=True)
                p = jnp.exp2(s - m)
                pv = jnp.dot(p.astype(BF16), jnp.concatenate([vals, ones], axis=1),
                             preferred_element_type=F32)
                rows = pl.ds(q0, BLOCK, stride=dil)
                num_ref[pi, rows, :] = _unstack_heads(pv[:, :LANES])
                den_ref[pi, rows, :] = _unstack_heads(pv[:, LANES:])
                max_ref[pi, rows, :] = jnp.where(head0, m[:BLOCK], m[BLOCK:])

    m = jnp.maximum(jnp.maximum(max_ref[0], max_ref[1]), max_ref[2])
    e = [jnp.exp2(max_ref[pi] - m) for pi in range(len(DILATIONS))]
    num = e[0] * num_ref[0] + e[1] * num_ref[1] + e[2] * num_ref[2]
    den = e[0] * den_ref[0] + e[1] * den_ref[1] + e[2] * den_ref[2]
    out_ref[...] = (num / den).astype(BF16)


def _dilated_attention(proj, batch, seq):
    n = proj.shape[0]
    q0, k0, v0 = ((4 * RET_WIDTH + t * DIL_WIDTH) // LANES for t in range(3))
    per_seq = seq // DIL_SPAN
    n_pairs = DIL_HEADS // PAIR

    def cur(c0):
        return pl.BlockSpec((DIL_SPAN, LANES), lambda b, p, i: (b * per_seq + i, c0 + p))

    def prev(c0):
        return pl.BlockSpec((DIL_SPAN, LANES),
                            lambda b, p, i: (jnp.maximum(b * per_seq + i - 1, 0), c0 + p))

    return pl.pallas_call(
        _dilated_body,
        grid=(batch, n_pairs, per_seq),
        in_specs=[cur(q0), cur(k0), prev(k0), cur(v0), prev(v0),
                  _resident((PAIR * BLOCK, 2 * BLOCK))],
        out_specs=pl.BlockSpec((DIL_SPAN, LANES), lambda b, p, i: (b * per_seq + i, p)),
        out_shape=jax.ShapeDtypeStruct((n, DIL_WIDTH), BF16),
        scratch_shapes=[pltpu.VMEM((DIL_SPAN, LANES), F32),
                        pltpu.VMEM((2 * DIL_SPAN, LANES), F32),
                        pltpu.VMEM((2 * DIL_SPAN, LANES), F32),
                        pltpu.VMEM((len(DILATIONS), DIL_SPAN, LANES), F32),
                        pltpu.VMEM((len(DILATIONS), DIL_SPAN, LANES), F32),
                        pltpu.VMEM((len(DILATIONS), DIL_SPAN, LANES), F32)],
        compiler_params=_cparams(3),
        name="dilated",
    )(proj, proj, proj, proj, proj, _band_bias())


SB_ROWS = 512
SB_PAIRS = 4
SB_TERMS = 1
SB_QBLOCKS = 2
SB_SLOT_ROWS = SB_PAIRS * PAIR * SB_TQ
SB_DEAD = -105.0


def _suffix_matrix():
    tri = np.arange(SB_TK)[:, None] > np.arange(SB_TK)[None, :]
    return jnp.asarray(np.concatenate([tri] * SB_TERMS, axis=0), dtype=BF16)


def _sb_visit(qs_ref, k_ref, v_ref, tri_ref, acc_ref, carry_ref, slot, first_kb, n_blk, diagonal):
    start = pl.multiple_of(first_kb * SB_TK, SB_TK)
    width = n_blk * SB_TK
    tri = tri_ref[...]
    for rc in range(SB_SLOT_ROWS // SB_ROWS):
        rows = slice(slot * SB_SLOT_ROWS + rc * SB_ROWS, slot * SB_SLOT_ROWS + (rc + 1) * SB_ROWS)
        pair = (rc * SB_ROWS) // (PAIR * SB_TQ)
        kwin = k_ref[pl.ds(start, width), pair * LANES:(pair + 1) * LANES]
        vwin = v_ref[pl.ds(start, width), pair * LANES:(pair + 1) * LANES]
        z = lax.dot_general(qs_ref[rows, :], kwin, (((1,), (1,)), ((), ())),
                            preferred_element_type=F32)
        log_go = jnp.minimum(z, 0.0) - jnp.log(1.0 + jnp.exp2(jnp.abs(z) * -LOG2E))
        log_stay = log_go - z
        if diagonal:
            q_row = (rc * SB_ROWS + lax.broadcasted_iota(jnp.int32, z.shape, 0)) & (SB_TQ - 1)
            causal = lax.broadcasted_iota(jnp.int32, z.shape, 1) < q_row + (width - SB_TK)
            log_stay = jnp.where(causal, log_stay, 0.0)
        terms = [log_stay.astype(BF16)]
        for _ in range(SB_TERMS - 1):
            rest = log_stay - sum(t.astype(F32) for t in terms)
            terms.append(rest.astype(BF16))
        blocks = [slice(b * SB_TK, (b + 1) * SB_TK) for b in range(n_blk)]
        split = jnp.concatenate(
            [jnp.concatenate([t[:, c] for t in terms], axis=1) for c in blocks], axis=0)
        within = jnp.dot(split, tri, preferred_element_type=F32)
        after = None if diagonal else carry_ref[rows, :]
        later = [None] * n_blk
        for b in reversed(range(n_blk)):
            later[b] = within[b * SB_ROWS:(b + 1) * SB_ROWS]
            if after is not None:
                later[b] = later[b] + jnp.concatenate([after] * (SB_TK // LANES), axis=1)
            total = jnp.broadcast_to(jnp.sum(log_stay[:, blocks[b]], axis=-1, keepdims=True),
                                     (SB_ROWS, LANES))
            after = total if after is None else after + total
        a = jnp.exp(log_go + jnp.concatenate(later, axis=1))
        if diagonal:
            a = jnp.where(causal, a, 0.0)
        pv = jnp.dot(a.astype(BF16), vwin, preferred_element_type=F32)
        if diagonal:
            acc_ref[rows, :] = pv
        else:
            acc_ref[rows, :] += pv
        carry_ref[rows, :] = after


def _sb_body(q_ref, k_ref, v_ref, tri_ref, o_ref, qs_ref, acc_ref, carry_ref):
    step_i = pl.program_id(2)
    stacked = PAIR * SB_TQ
    units = [(slot, slice(slot * SB_TQ, (slot + 1) * SB_TQ), slice(pair * LANES, (pair + 1) * LANES),
              slice(slot * SB_SLOT_ROWS + pair * stacked, slot * SB_SLOT_ROWS + (pair + 1) * stacked))
             for slot in range(SB_QBLOCKS) for pair in range(SB_PAIRS)]
    for _, q_rows, lanes, rows in units:
        qs_ref[rows, :] = _stack_heads(q_ref[q_rows, lanes])
    visit = functools.partial(_sb_visit, qs_ref, k_ref, v_ref, tri_ref, acc_ref, carry_ref)
    q_blk = [step_i * SB_QBLOCKS + slot for slot in range(SB_QBLOCKS)]

    @pl.when(step_i == 0)
    def _():
        visit(0, 0, 1, True)
        for slot in range(1, SB_QBLOCKS):
            visit(slot, slot - 1, 2, True)

    @pl.when(step_i > 0)
    def _():
        for slot in range(SB_QBLOCKS):
            visit(slot, q_blk[slot] - 1, 2, True)

    for slot in range(SB_QBLOCKS):
        slot_rows = slice(slot * SB_SLOT_ROWS, (slot + 1) * SB_SLOT_ROWS)

        def alive(state, slot=slot):
            t, top = state
            return (t <= q_blk[slot]) & (top > SB_DEAD)

        def step(state, slot=slot, slot_rows=slot_rows):
            t, _ = state
            visit(slot, q_blk[slot] - t, 1, False)
            return t + 1, jnp.max(carry_ref[slot_rows, :])

        lax.while_loop(alive, step, (2, jnp.max(carry_ref[slot_rows, :])))
    for _, q_rows, lanes, rows in units:
        o_ref[q_rows, lanes] = _unstack_heads(acc_ref[rows, :]).astype(BF16)


def _stick_breaking(proj, batch, seq):
    n = proj.shape[0]
    groups = SB_HEADS // PAIR // SB_PAIRS
    per_seq = seq // (SB_QBLOCKS * SB_TQ)
    wide = SB_PAIRS * LANES
    qo = lambda b, p, i: (b * per_seq + i, p)
    return pl.pallas_call(
        _sb_body,
        grid=(batch, groups, per_seq),
        in_specs=[pl.BlockSpec((SB_QBLOCKS * SB_TQ, wide), qo),
                  pl.BlockSpec((seq, wide), lambda b, p, i: (b, groups + p)),
                  pl.BlockSpec((seq, wide), lambda b, p, i: (b, 2 * groups + p)),
                  _resident((SB_TERMS * SB_TK, SB_TK))],
        out_specs=pl.BlockSpec((SB_QBLOCKS * SB_TQ, wide), qo),
        out_shape=jax.ShapeDtypeStruct((n, SB_WIDTH), BF16),
        scratch_shapes=[pltpu.VMEM((SB_QBLOCKS * SB_SLOT_ROWS, LANES), BF16),
                        pltpu.VMEM((SB_QBLOCKS * SB_SLOT_ROWS, LANES), F32),
                        pltpu.VMEM((SB_QBLOCKS * SB_SLOT_ROWS, LANES), F32)],
        compiler_params=_cparams(3),
        name="stick_breaking",
    )(proj, proj, proj, _suffix_matrix())


HYB_PLAN = (("ret_rot", 1.0), ("ret_rot", HEAD_DIM ** -0.5), ("plain", 1.0), ("plain", 1.0),
            ("dil_rot", HEAD_DIM ** -0.5 * LOG2E), ("dil_rot", 1.0), ("plain", 1.0))
SB_PLAN = (("plain", HEAD_DIM ** -0.5),) * 2 + (("plain", 1.0),) * 4


def kernel(x, ffn1_norm, ffn1_w_in, ffn1_w_out, mix_norm, ffn2_norm, ffn2_w_in, ffn2_w_out,
           hyb_w_in, ret_gn, hyb_w_out, sb_w_in, sb_w_out, final_norm):
    batch, seq, _ = x.shape
    depth = ffn1_norm.shape[0]
    tables = (_rotary_tables(seq, HEAD_DIM, RET_ROPE_THETA)
              + _rotary_tables(seq, ROPE_DIM, ROPE_THETA))
    (ffn1_w_in, ffn1_w_out, ffn2_w_in, ffn2_w_out, hyb_w_in, hyb_w_out, sb_w_in, sb_w_out) = (
        w.astype(BF16) for w in (ffn1_w_in, ffn1_w_out, ffn2_w_in, ffn2_w_out,
                                 hyb_w_in, hyb_w_out, sb_w_in, sb_w_out))
    h = x.reshape(batch * seq, D_MODEL)
    for layer in range(depth):
        h = _ffn(h, ffn1_norm[layer], ffn1_w_in, ffn1_w_out, layer)
        if layer % 2 == 0:
            e = layer // 2
            proj = _proj(h, mix_norm[layer], hyb_w_in, e, HYB_PLAN, tables, seq)
            mix = ([_retention(proj, ret_gn[e], batch, seq),
                    _dilated_attention(proj, batch, seq)], hyb_w_out, e)
        else:
            o = layer // 2
            proj = _proj(h, mix_norm[layer], sb_w_in, o, SB_PLAN, (), seq)
            mix = ([_stick_breaking(proj, batch, seq)], sb_w_out, o)
        h = _ffn(h, ffn2_norm[layer], ffn2_w_in, ffn2_w_out, layer, mix,
                 final_norm if layer == depth - 1 else None)
    return h.reshape(batch, seq, D_MODEL)
```

```python
import functools

import numpy as np
import jax
import jax.numpy as jnp
from jax import lax
from jax.experimental import pallas as pl
from jax.experimental.pallas import tpu as pltpu

F32 = jnp.float32
BF16 = jnp.bfloat16

D_MODEL = 1024
HEAD_DIM = 64
RET_HEADS = 8
DIL_HEADS = 8
SB_HEADS = 16
RET_WIDTH = RET_HEADS * HEAD_DIM
DIL_WIDTH = DIL_HEADS * HEAD_DIM
SB_WIDTH = SB_HEADS * HEAD_DIM
HYB_IN = 4 * RET_WIDTH + 3 * DIL_WIDTH
D_FF = 2816
BLOCK = 128
RET_CHUNK = 128
RET_ROPE_THETA = 10000.0
ROPE_THETA = 500000.0
ROPE_DIM = HEAD_DIM // 4
DILATIONS = (1, 4, 16)
NORM_EPS = 1e-6
GN_EPS = 1e-5

LANES = 128
PAIR = LANES // HEAD_DIM
VMEM_LIMIT = 56 * 1024 * 1024
NEG_BIG = -1e30
LOG2E = 1.4426950408889634

TOK_TILE = 1024
FFN_TILE = 1024
FF_CHUNK = 256
SB_TQ = 256
SB_TK = 256


def _cparams(n_axes):
    return pltpu.CompilerParams(
        dimension_semantics=("arbitrary",) * n_axes,
        vmem_limit_bytes=VMEM_LIMIT)


def _resident(shape, layer=None):
    if layer is None:
        return pl.BlockSpec(shape, lambda *_: (0,) * len(shape), pipeline_mode=pl.Buffered(1))
    return pl.BlockSpec((None,) + tuple(shape), lambda *_: (layer,) + (0,) * len(shape),
                        pipeline_mode=pl.Buffered(1))


def _rms(x, g):
    return x * lax.rsqrt(jnp.mean(x * x, axis=-1, keepdims=True) + NORM_EPS) * g


def _silu(x):
    return x / (1.0 + jnp.exp(-x))


def _lane_is_head0(shape):
    return lax.broadcasted_iota(jnp.int32, shape, len(shape) - 1) < HEAD_DIM


def _stack_heads(q):
    h0 = _lane_is_head0(q.shape)
    zero = jnp.zeros_like(q)
    return jnp.concatenate([jnp.where(h0, q, zero), jnp.where(h0, zero, q)], axis=0)


def _unstack_heads(o2):
    t = o2.shape[0] // 2
    return jnp.where(_lane_is_head0((t, LANES)), o2[:t], o2[t:])


def _ffn_body(*refs, n_slabs, final_norm):
    refs = list(refs)
    x_ref = refs.pop(0)
    slabs = [refs.pop(0) for _ in range(n_slabs)]
    wmix_ref = refs.pop(0) if n_slabs else None
    g_ref, win_ref, wout_ref = refs.pop(0), refs.pop(0), refs.pop(0)
    fg_ref = refs.pop(0) if final_norm else None
    o_ref, act_ref = refs

    x = x_ref[...]
    lo = 0
    for a_ref in slabs:
        width = a_ref.shape[1]
        x = x + jnp.dot(a_ref[...], wmix_ref[lo:lo + width, :], preferred_element_type=F32)
        lo += width
    hb = _rms(x, g_ref[...]).astype(BF16)
    for c in range(D_FF // FF_CHUNK):
        cols = slice(c * FF_CHUNK, (c + 1) * FF_CHUNK)
        up_cols = slice(D_FF + c * FF_CHUNK, D_FF + (c + 1) * FF_CHUNK)
        gate = jnp.dot(hb, win_ref[:, cols], preferred_element_type=F32)
        up = jnp.dot(hb, win_ref[:, up_cols], preferred_element_type=F32)
        act_ref[:, cols] = (_silu(gate) * up).astype(BF16)
    y = x + 0.5 * jnp.dot(act_ref[...], wout_ref[...], preferred_element_type=F32)
    if final_norm:
        y = _rms(y, fg_ref[...])
    o_ref[...] = y


def _ffn(x, g, w_in, w_out, layer, mix=None, final_g=None):
    n = x.shape[0]
    row = pl.BlockSpec((FFN_TILE, D_MODEL), lambda i: (i, 0))
    args, specs = [x], [row]
    slabs = ()
    if mix is not None:
        slabs, w_mix, mix_layer = mix
        args += [*slabs, w_mix]
        specs += [pl.BlockSpec((FFN_TILE, a.shape[1]), lambda i: (i, 0)) for a in slabs]
        specs += [_resident(w_mix.shape[1:], mix_layer)]
    args += [g.reshape(1, D_MODEL), w_in, w_out]
    specs += [_resident((1, D_MODEL)), _resident((D_MODEL, 2 * D_FF), layer),
              _resident((D_FF, D_MODEL), layer)]
    if final_g is not None:
        args.append(final_g.reshape(1, D_MODEL))
        specs.append(_resident((1, D_MODEL)))
    return pl.pallas_call(
        functools.partial(_ffn_body, n_slabs=len(slabs), final_norm=final_g is not None),
        grid=(n // FFN_TILE,),
        in_specs=specs,
        out_specs=row,
        out_shape=jax.ShapeDtypeStruct((n, D_MODEL), F32),
        scratch_shapes=[pltpu.VMEM((FFN_TILE, D_FF), BF16)],
        compiler_params=_cparams(1),
        name="ffn",
    )(*args)


PROJ_GROUP = 512


def _rotate(y, cos, sin_a, sin_b, half):
    return (y * cos + pltpu.roll(y, LANES - half, 1) * sin_a + pltpu.roll(y, half, 1) * sin_b)


def _proj_body(x_ref, g_ref, w_ref, *rest, plan):
    tables, o_ref = rest[:-1], rest[-1]
    hb = _rms(x_ref[...], g_ref[...]).astype(BF16)
    for gi, (kind, scale) in enumerate(plan):
        lo = gi * PROJ_GROUP
        y = jnp.dot(hb, w_ref[:, lo:lo + PROJ_GROUP], preferred_element_type=F32)
        for s in range(PROJ_GROUP // LANES):
            ys = y[:, s * LANES:(s + 1) * LANES]
            if kind == "ret_rot":
                ys = _rotate(ys, tables[0][...], tables[1][...], tables[2][...], HEAD_DIM // 2)
            elif kind == "dil_rot":
                ys = _rotate(ys, tables[3][...], tables[4][...], tables[5][...], ROPE_DIM // 2)
            if scale != 1.0:
                ys = ys * scale
            o_ref[:, lo + s * LANES:lo + (s + 1) * LANES] = ys.astype(BF16)


def _proj(x, g, w, layer, plan, tables, seq):
    n = x.shape[0]
    width = w.shape[2]
    row = pl.BlockSpec((TOK_TILE, D_MODEL), lambda i: (i, 0))
    per_seq = seq // TOK_TILE
    tab = pl.BlockSpec((TOK_TILE, LANES), lambda i: (i % per_seq, 0))
    return pl.pallas_call(
        functools.partial(_proj_body, plan=plan),
        grid=(n // TOK_TILE,),
        in_specs=[row, _resident((1, D_MODEL)), _resident((D_MODEL, width), layer)]
        + [tab] * len(tables),
        out_specs=pl.BlockSpec((TOK_TILE, width), lambda i: (i, 0)),
        out_shape=jax.ShapeDtypeStruct((n, width), BF16),
        compiler_params=_cparams(1),
        name="proj",
    )(x, g.reshape(1, D_MODEL), w, *tables)


def _rotary_tables(seq, rot_dim, theta):
    half = rot_dim // 2
    lane = np.arange(LANES) % HEAD_DIM
    inv_freq = 1.0 / (theta ** (np.arange(half, dtype=np.float64) / half))
    ang = (np.arange(seq, dtype=np.float64)[:, None] * inv_freq[None, :])[:, lane % half]
    cos, sin = np.cos(ang), np.sin(ang)
    is_x1 = (lane < half)[None, :]
    is_x2 = ((lane >= half) & (lane < rot_dim))[None, :]
    cos_t = np.where(is_x1 | is_x2, cos, 1.0)
    sin_a = np.where(is_x1, -sin, 0.0)
    sin_b = np.where(is_x2, sin, 0.0)
    return tuple(jnp.asarray(t, dtype=F32) for t in (cos_t, sin_a, sin_b))


RET_TOK = 1024


def _retention_consts():
    h = np.arange(RET_HEADS, dtype=np.float64)
    log_g = np.log(1.0 - 2.0 ** (-5.0 - h))
    i = np.arange(RET_CHUNK, dtype=np.float64)
    diff = i[:, None] - i[None, :]
    decay_in = np.where(diff >= 0, np.exp(np.maximum(diff, 0.0)[None] * log_g[:, None, None]), 0.0)
    n_pairs = RET_HEADS // PAIR
    decay = decay_in.reshape(n_pairs, PAIR * RET_CHUNK, RET_CHUNK)
    lane_head = np.arange(LANES) // HEAD_DIM
    head_of = (np.arange(n_pairs)[:, None] * PAIR + lane_head[None, :])
    lg = log_g[head_of]
    q_dec = np.exp((i + 1.0)[None, :, None] * lg[:, None, :])
    k_dec = np.exp((RET_CHUNK - 1.0 - i)[None, :, None] * lg[:, None, :])
    same_head = (lane_head[:, None] == lane_head[None, :])
    c_dec = np.exp(RET_CHUNK * lg)[:, :, None] * same_head[None]
    bd = np.broadcast_to(same_head[None], c_dec.shape)
    return tuple(jnp.asarray(t, dtype=F32) for t in (decay, q_dec, k_dec, c_dec, bd))


def _retention_chunk(q, k, v, g, state, dec, q_dec, k_dec, c_dec, bd, gn):
    h0 = _lane_is_head0((RET_CHUNK, LANES))
    inv = 1.0 / HEAD_DIM
    sc = lax.dot_general(_stack_heads(q), k, (((1,), (1,)), ((), ())),
                         preferred_element_type=F32) * dec
    inner = _unstack_heads(jnp.dot(sc.astype(BF16), v, preferred_element_type=F32))
    qd = (q.astype(F32) * q_dec).astype(BF16)
    o = inner + jnp.dot(qd, state.astype(BF16), preferred_element_type=F32)
    kd = (k.astype(F32) * k_dec).astype(BF16)
    kv = lax.dot_general(kd, v, (((0,), (0,)), ((), ())), preferred_element_type=F32)
    state = state * c_dec + kv * bd
    s0 = jnp.sum(jnp.where(h0, o, 0.0), axis=-1, keepdims=True)
    s1 = jnp.sum(jnp.where(h0, 0.0, o), axis=-1, keepdims=True)
    d = o - jnp.where(h0, s0, s1) * inv
    d2 = d * d
    v0 = jnp.sum(jnp.where(h0, d2, 0.0), axis=-1, keepdims=True)
    v1 = jnp.sum(jnp.where(h0, 0.0, d2), axis=-1, keepdims=True)
    y = d * lax.rsqrt(jnp.where(h0, v0, v1) * inv + GN_EPS) * gn
    return (y * _silu(g.astype(F32))).astype(BF16), state


def _retention_body(q_ref, k_ref, v_ref, g_ref, dec_ref, qd_ref, kd_ref, cd_ref, bd_ref, gn_ref,
                    o_ref, state_ref):
    @pl.when(pl.program_id(1) == 0)
    def _():
        state_ref[...] = jnp.zeros_like(state_ref)

    for p in range(RET_HEADS // PAIR):
        cols = slice(p * LANES, (p + 1) * LANES)
        state = state_ref[p]
        for c in range(RET_TOK // RET_CHUNK):
            rows = slice(c * RET_CHUNK, (c + 1) * RET_CHUNK)
            o_ref[rows, cols], state = _retention_chunk(
                q_ref[rows, cols], k_ref[rows, cols], v_ref[rows, cols], g_ref[rows, cols], state,
                dec_ref[p], qd_ref[p], kd_ref[p], cd_ref[p], bd_ref[p], gn_ref[:, cols])
        state_ref[p] = state


def _retention(proj, ret_gn, batch, seq):
    n = proj.shape[0]
    per_seq = seq // RET_TOK
    consts = _retention_consts()

    def slab(j):
        return pl.BlockSpec((RET_TOK, RET_WIDTH), lambda b, i, j=j: (b * per_seq + i, j))

    return pl.pallas_call(
        _retention_body,
        grid=(batch, per_seq),
        in_specs=[slab(0), slab(1), slab(2), slab(3)]
        + [_resident(c.shape) for c in consts] + [_resident((1, RET_WIDTH))],
        out_specs=pl.BlockSpec((RET_TOK, RET_WIDTH), lambda b, i: (b * per_seq + i, 0)),
        out_shape=jax.ShapeDtypeStruct((n, RET_WIDTH), BF16),
        scratch_shapes=[pltpu.VMEM((RET_HEADS // PAIR, LANES, LANES), F32)],
        compiler_params=_cparams(2),
        name="retention",
    )(proj, proj, proj, proj, *consts, ret_gn.reshape(1, RET_WIDTH))


DIL_SPAN = BLOCK * max(DILATIONS)


def _band_bias():
    qi = np.arange(PAIR * BLOCK)[:, None] % BLOCK
    kj = np.arange(2 * BLOCK)[None, :]
    return jnp.asarray(np.where((kj >= qi) & (kj <= qi + BLOCK), 0.0, NEG_BIG), dtype=F32)


def _dilated_body(q_ref, kc_ref, kp_ref, vc_ref, vp_ref, bias_ref, out_ref,
                  qf_ref, kf_ref, vf_ref, num_ref, den_ref, max_ref):
    first = pl.program_id(2) == 0
    qf_ref[...] = q_ref[...].astype(F32)
    kf_ref[0:DIL_SPAN, :] = kp_ref[...].astype(F32)
    kf_ref[DIL_SPAN:, :] = kc_ref[...].astype(F32)
    vf_ref[0:DIL_SPAN, :] = vp_ref[...].astype(F32)
    vf_ref[DIL_SPAN:, :] = vc_ref[...].astype(F32)
    k_min = jnp.where(first, BLOCK, 0)
    kj = lax.broadcasted_iota(jnp.int32, (PAIR * BLOCK, 2 * BLOCK), 1)
    ones = jnp.ones((2 * BLOCK, LANES), BF16)
    head0 = _lane_is_head0((BLOCK, LANES))

    for pi, dil in enumerate(DILATIONS):
        for r in range(dil):
            for n in range(DIL_SPAN // (BLOCK * dil)):
                q0 = n * (BLOCK * dil) + r
                k0 = q0 + DIL_SPAN - BLOCK * dil
                q = qf_ref[pl.ds(q0, BLOCK, stride=dil), :].astype(BF16)
                keys = kf_ref[pl.ds(k0, 2 * BLOCK, stride=dil), :].astype(BF16)
                vals = vf_ref[pl.ds(k0, 2 * BLOCK, stride=dil), :].astype(BF16)
                s = lax.dot_general(_stack_heads(q), keys, (((1,), (1,)), ((), ())),
                                    preferred_element_type=F32) + bias_ref[...]
                if n == 0:
                    s = jnp.where(kj >= k_min, s, NEG_BIG)
                m = jnp.max(s, axis=-1, keepdims=True)
                p = jnp.exp2(s - m)
                pv = jnp.dot(p.astype(BF16), jnp.concatenate([vals, ones], axis=1),
                             preferred_element_type=F32)
                rows = pl.ds(q0, BLOCK, stride=dil)
                num_ref[pi, rows, :] = _unstack_heads(pv[:, :LANES])
                den_ref[pi, rows, :] = _unstack_heads(pv[:, LANES:])
                max_ref[pi, rows, :] = jnp.where(head0, m[:BLOCK], m[BLOCK:])

    m = jnp.maximum(jnp.maximum(max_ref[0], max_ref[1]), max_ref[2])
    e = [jnp.exp2(max_ref[pi] - m) for pi in range(len(DILATIONS))]
    num = e[0] * num_ref[0] + e[1] * num_ref[1] + e[2] * num_ref[2]
    den = e[0] * den_ref[0] + e[1] * den_ref[1] + e[2] * den_ref[2]
    out_ref[...] = (num / den).astype(BF16)


def _dilated_attention(proj, batch, seq):
    n = proj.shape[0]
    q0, k0, v0 = ((4 * RET_WIDTH + t * DIL_WIDTH) // LANES for t in range(3))
    per_seq = seq // DIL_SPAN
    n_pairs = DIL_HEADS // PAIR

    def cur(c0):
        return pl.BlockSpec((DIL_SPAN, LANES), lambda b, p, i: (b * per_seq + i, c0 + p))

    def prev(c0):
        return pl.BlockSpec((DIL_SPAN, LANES),
                            lambda b, p, i: (jnp.maximum(b * per_seq + i - 1, 0), c0 + p))

    return pl.pallas_call(
        _dilated_body,
        grid=(batch, n_pairs, per_seq),
        in_specs=[cur(q0), cur(k0), prev(k0), cur(v0), prev(v0),
                  _resident((PAIR * BLOCK, 2 * BLOCK))],
        out_specs=pl.BlockSpec((DIL_SPAN, LANES), lambda b, p, i: (b * per_seq + i, p)),
        out_shape=jax.ShapeDtypeStruct((n, DIL_WIDTH), BF16),
        scratch_shapes=[pltpu.VMEM((DIL_SPAN, LANES), F32),
                        pltpu.VMEM((2 * DIL_SPAN, LANES), F32),
                        pltpu.VMEM((2 * DIL_SPAN, LANES), F32),
                        pltpu.VMEM((len(DILATIONS), DIL_SPAN, LANES), F32),
                        pltpu.VMEM((len(DILATIONS), DIL_SPAN, LANES), F32),
                        pltpu.VMEM((len(DILATIONS), DIL_SPAN, LANES), F32)],
        compiler_params=_cparams(3),
        name="dilated",
    )(proj, proj, proj, proj, proj, _band_bias())


SB_ROWS = 512
SB_PAIRS = 4
SB_TERMS = 1
SB_QBLOCKS = 2
SB_SLOT_ROWS = SB_PAIRS * PAIR * SB_TQ
SB_DEAD = -105.0


def _suffix_matrix():
    tri = np.arange(SB_TK)[:, None] > np.arange(SB_TK)[None, :]
    return jnp.asarray(np.concatenate([tri] * SB_TERMS, axis=0), dtype=BF16)


def _sb_visit(qs_ref, k_ref, v_ref, tri_ref, acc_ref, carry_ref, slot, first_kb, n_blk, diagonal):
    start = pl.multiple_of(first_kb * SB_TK, SB_TK)
    width = n_blk * SB_TK
    tri = tri_ref[...]
    for rc in range(SB_SLOT_ROWS // SB_ROWS):
        rows = slice(slot * SB_SLOT_ROWS + rc * SB_ROWS, slot * SB_SLOT_ROWS + (rc + 1) * SB_ROWS)
        pair = (rc * SB_ROWS) // (PAIR * SB_TQ)
        kwin = k_ref[pl.ds(start, width), pair * LANES:(pair + 1) * LANES]
        vwin = v_ref[pl.ds(start, width), pair * LANES:(pair + 1) * LANES]
        z = lax.dot_general(qs_ref[rows, :], kwin, (((1,), (1,)), ((), ())),
                            preferred_element_type=F32)
        if diagonal:
            own = (SB_ROWS, SB_TK)
            q_row = (rc * SB_ROWS + lax.broadcasted_iota(jnp.int32, own, 0)) & (SB_TQ - 1)
            causal = lax.broadcasted_iota(jnp.int32, own, 1) < q_row
            masked = jnp.where(causal, z[:, width - SB_TK:], NEG_BIG)
            z = masked if width == SB_TK else jnp.concatenate([z[:, :width - SB_TK], masked], axis=1)
        log_go = jnp.minimum(z, 0.0) - jnp.log(1.0 + jnp.exp2(jnp.abs(z) * -LOG2E))
        log_stay = log_go - z
        terms = [log_stay.astype(BF16)]
        for _ in range(SB_TERMS - 1):
            rest = log_stay - sum(t.astype(F32) for t in terms)
            terms.append(rest.astype(BF16))
        blocks = [slice(b * SB_TK, (b + 1) * SB_TK) for b in range(n_blk)]
        split = jnp.concatenate(
            [jnp.concatenate([t[:, c] for t in terms], axis=1) for c in blocks], axis=0)
        within = jnp.dot(split, tri, preferred_element_type=F32)
        after = None if diagonal else carry_ref[rows, :]
        later = [None] * n_blk
        for b in reversed(range(n_blk)):
            later[b] = within[b * SB_ROWS:(b + 1) * SB_ROWS]
            if after is not None:
                later[b] = later[b] + jnp.concatenate([after] * (SB_TK // LANES), axis=1)
            total = jnp.broadcast_to(jnp.sum(log_stay[:, blocks[b]], axis=-1, keepdims=True),
                                     (SB_ROWS, LANES))
            after = total if after is None else after + total
        a = jnp.exp(log_go + jnp.concatenate(later, axis=1))
        pv = jnp.dot(a.astype(BF16), vwin, preferred_element_type=F32)
        if diagonal:
            acc_ref[rows, :] = pv
        else:
            acc_ref[rows, :] += pv
        carry_ref[rows, :] = after


def _sb_body(q_ref, k_ref, v_ref, tri_ref, o_ref, qs_ref, acc_ref, carry_ref):
    step_i = pl.program_id(2)
    stacked = PAIR * SB_TQ
    units = [(slot, slice(slot * SB_TQ, (slot + 1) * SB_TQ), slice(pair * LANES, (pair + 1) * LANES),
              slice(slot * SB_SLOT_ROWS + pair * stacked, slot * SB_SLOT_ROWS + (pair + 1) * stacked))
             for slot in range(SB_QBLOCKS) for pair in range(SB_PAIRS)]
    for _, q_rows, lanes, rows in units:
        qs_ref[rows, :] = _stack_heads(q_ref[q_rows, lanes])
    visit = functools.partial(_sb_visit, qs_ref, k_ref, v_ref, tri_ref, acc_ref, carry_ref)
    q_blk = [step_i * SB_QBLOCKS + slot for slot in range(SB_QBLOCKS)]

    @pl.when(step_i == 0)
    def _():
        visit(0, 0, 1, True)
        for slot in range(1, SB_QBLOCKS):
            visit(slot, slot - 1, 2, True)

    @pl.when(step_i > 0)
    def _():
        for slot in range(SB_QBLOCKS):
            visit(slot, q_blk[slot] - 1, 2, True)

    for slot in range(SB_QBLOCKS):
        slot_rows = slice(slot * SB_SLOT_ROWS, (slot + 1) * SB_SLOT_ROWS)

        def alive(state, slot=slot):
            t, top = state
            return (t <= q_blk[slot]) & (top > SB_DEAD)

        def step(state, slot=slot, slot_rows=slot_rows):
            t, _ = state
            visit(slot, q_blk[slot] - t, 1, False)
            return t + 1, jnp.max(carry_ref[slot_rows, :])

        lax.while_loop(alive, step, (2, jnp.max(carry_ref[slot_rows, :])))
    for _, q_rows, lanes, rows in units:
        o_ref[q_rows, lanes] = _unstack_heads(acc_ref[rows, :]).astype(BF16)


def _stick_breaking(proj, batch, seq):
    n = proj.shape[0]
    groups = SB_HEADS // PAIR // SB_PAIRS
    per_seq = seq // (SB_QBLOCKS * SB_TQ)
    wide = SB_PAIRS * LANES
    qo = lambda b, p, i: (b * per_seq + i, p)
    return pl.pallas_call(
        _sb_body,
        grid=(batch, groups, per_seq),
        in_specs=[pl.BlockSpec((SB_QBLOCKS * SB_TQ, wide), qo),
                  pl.BlockSpec((seq, wide), lambda b, p, i: (b, groups + p)),
                  pl.BlockSpec((seq, wide), lambda b, p, i: (b, 2 * groups + p)),
                  _resident((SB_TERMS * SB_TK, SB_TK))],
        out_specs=pl.BlockSpec((SB_QBLOCKS * SB_TQ, wide), qo),
        out_shape=jax.ShapeDtypeStruct((n, SB_WIDTH), BF16),
        scratch_shapes=[pltpu.VMEM((SB_QBLOCKS * SB_SLOT_ROWS, LANES), BF16),
                        pltpu.VMEM((SB_QBLOCKS * SB_SLOT_ROWS, LANES), F32),
                        pltpu.VMEM((SB_QBLOCKS * SB_SLOT_ROWS, LANES), F32)],
        compiler_params=_cparams(3),
        name="stick_breaking",
    )(proj, proj, proj, _suffix_matrix())


HYB_PLAN = (("ret_rot", 1.0), ("ret_rot", HEAD_DIM ** -0.5), ("plain", 1.0), ("plain", 1.0),
            ("dil_rot", HEAD_DIM ** -0.5 * LOG2E), ("dil_rot", 1.0), ("plain", 1.0))
SB_PLAN = (("plain", HEAD_DIM ** -0.5),) * 2 + (("plain", 1.0),) * 4


def kernel(x, ffn1_norm, ffn1_w_in, ffn1_w_out, mix_norm, ffn2_norm, ffn2_w_in, ffn2_w_out,
           hyb_w_in, ret_gn, hyb_w_out, sb_w_in, sb_w_out, final_norm):
    batch, seq, _ = x.shape
    depth = ffn1_norm.shape[0]
    tables = (_rotary_tables(seq, HEAD_DIM, RET_ROPE_THETA)
              + _rotary_tables(seq, ROPE_DIM, ROPE_THETA))
    (ffn1_w_in, ffn1_w_out, ffn2_w_in, ffn2_w_out, hyb_w_in, hyb_w_out, sb_w_in, sb_w_out) = (
        w.astype(BF16) for w in (ffn1_w_in, ffn1_w_out, ffn2_w_in, ffn2_w_out,
                                 hyb_w_in, hyb_w_out, sb_w_in, sb_w_out))
    h = x.reshape(batch * seq, D_MODEL)
    for layer in range(depth):
        h = _ffn(h, ffn1_norm[layer], ffn1_w_in, ffn1_w_out, layer)
        if layer % 2 == 0:
            e = layer // 2
            proj = _proj(h, mix_norm[layer], hyb_w_in, e, HYB_PLAN, tables, seq)
            mix = ([_retention(proj, ret_gn[e], batch, seq),
                    _dilated_attention(proj, batch, seq)], hyb_w_out, e)
        else:
            o = layer // 2
            proj = _proj(h, mix_norm[layer], sb_w_in, o, SB_PLAN, (), seq)
            mix = ([_stick_breaking(proj, batch, seq)], sb_w_out, o)
        h = _ffn(h, ffn2_norm[layer], ffn2_w_in, ffn2_w_out, layer, mix,
                 final_norm if layer == depth - 1 else None)
    return h.reshape(batch, seq, D_MODEL)
```

```python
import functools

import numpy as np
import jax
import jax.numpy as jnp
from jax import lax
from jax.experimental import pallas as pl
from jax.experimental.pallas import tpu as pltpu

F32 = jnp.float32
BF16 = jnp.bfloat16

D_MODEL = 1024
HEAD_DIM = 64
RET_HEADS = 8
DIL_HEADS = 8
SB_HEADS = 16
RET_WIDTH = RET_HEADS * HEAD_DIM
DIL_WIDTH = DIL_HEADS * HEAD_DIM
SB_WIDTH = SB_HEADS * HEAD_DIM
D_FF = 2816
BLOCK = 128
RET_CHUNK = 128
RET_ROPE_THETA = 10000.0
ROPE_THETA = 500000.0
ROPE_DIM = HEAD_DIM // 4
DILATIONS = (1, 4, 16)
NORM_EPS = 1e-6
GN_EPS = 1e-5

LANES = 128
PAIR = LANES // HEAD_DIM
VMEM_LIMIT = 56 * 1024 * 1024
NEG_BIG = -1e30
LOG2E = 1.4426950408889634

TOK_TILE = 1024
FFN_TILE = 1024
FF_CHUNK = 256
SB_TQ = 256
SB_TK = 256


def _cparams(n_axes):
    return pltpu.CompilerParams(
        dimension_semantics=("arbitrary",) * n_axes,
        vmem_limit_bytes=VMEM_LIMIT)


def _resident(shape, layer=None):
    if layer is None:
        return pl.BlockSpec(shape, lambda *_: (0,) * len(shape), pipeline_mode=pl.Buffered(1))
    return pl.BlockSpec((None,) + tuple(shape), lambda *_: (layer,) + (0,) * len(shape),
                        pipeline_mode=pl.Buffered(1))


def _rms(x, g):
    return x * lax.rsqrt(jnp.mean(x * x, axis=-1, keepdims=True) + NORM_EPS) * g


def _silu(x):
    return x / (1.0 + jnp.exp(-x))


def _lane_is_head0(shape):
    return lax.broadcasted_iota(jnp.int32, shape, len(shape) - 1) < HEAD_DIM


def _stack_heads(q):
    h0 = _lane_is_head0(q.shape)
    zero = jnp.zeros_like(q)
    return jnp.concatenate([jnp.where(h0, q, zero), jnp.where(h0, zero, q)], axis=0)


def _unstack_heads(o2):
    t = o2.shape[0] // 2
    return jnp.where(_lane_is_head0((t, LANES)), o2[:t], o2[t:])


def _ffn_body(*refs, n_slabs, final_norm):
    refs = list(refs)
    x_ref = refs.pop(0)
    slabs = [refs.pop(0) for _ in range(n_slabs)]
    wmix_ref = refs.pop(0) if n_slabs else None
    g_ref, win_ref, wout_ref = refs.pop(0), refs.pop(0), refs.pop(0)
    fg_ref = refs.pop(0) if final_norm else None
    o_ref, act_ref = refs

    x = x_ref[...]
    lo = 0
    for a_ref in slabs:
        width = a_ref.shape[1]
        x = x + jnp.dot(a_ref[...], wmix_ref[lo:lo + width, :], preferred_element_type=F32)
        lo += width
    hb = _rms(x, g_ref[...]).astype(BF16)
    for c in range(D_FF // FF_CHUNK):
        cols = slice(c * FF_CHUNK, (c + 1) * FF_CHUNK)
        up_cols = slice(D_FF + c * FF_CHUNK, D_FF + (c + 1) * FF_CHUNK)
        gate = jnp.dot(hb, win_ref[:, cols], preferred_element_type=F32)
        up = jnp.dot(hb, win_ref[:, up_cols], preferred_element_type=F32)
        act_ref[:, cols] = (_silu(gate) * up).astype(BF16)
    y = x + 0.5 * jnp.dot(act_ref[...], wout_ref[...], preferred_element_type=F32)
    if final_norm:
        y = _rms(y, fg_ref[...])
    o_ref[...] = y


def _ffn(x, g, w_in, w_out, layer, mix=None, final_g=None):
    n = x.shape[0]
    row = pl.BlockSpec((FFN_TILE, D_MODEL), lambda i: (i, 0))
    args, specs = [x], [row]
    slabs = ()
    if mix is not None:
        slabs, w_mix, mix_layer = mix
        args += [*slabs, w_mix]
        specs += [pl.BlockSpec((FFN_TILE, a.shape[1]), lambda i: (i, 0)) for a in slabs]
        specs += [_resident(w_mix.shape[1:], mix_layer)]
    args += [g.reshape(1, D_MODEL), w_in, w_out]
    specs += [_resident((1, D_MODEL)), _resident((D_MODEL, 2 * D_FF), layer),
              _resident((D_FF, D_MODEL), layer)]
    if final_g is not None:
        args.append(final_g.reshape(1, D_MODEL))
        specs.append(_resident((1, D_MODEL)))
    return pl.pallas_call(
        functools.partial(_ffn_body, n_slabs=len(slabs), final_norm=final_g is not None),
        grid=(n // FFN_TILE,),
        in_specs=specs,
        out_specs=row,
        out_shape=jax.ShapeDtypeStruct((n, D_MODEL), F32),
        scratch_shapes=[pltpu.VMEM((FFN_TILE, D_FF), BF16)],
        compiler_params=_cparams(1),
        name="ffn",
    )(*args)


PROJ_GROUP = 512


def _rotate(y, cos, sin_a, sin_b, half):
    return (y * cos + pltpu.roll(y, LANES - half, 1) * sin_a + pltpu.roll(y, half, 1) * sin_b)


def _proj_body(x_ref, g_ref, w_ref, *rest, plan):
    tables, o_ref = rest[:-1], rest[-1]
    hb = _rms(x_ref[...], g_ref[...]).astype(BF16)
    for gi, (kind, scale) in enumerate(plan):
        lo = gi * PROJ_GROUP
        y = jnp.dot(hb, w_ref[:, lo:lo + PROJ_GROUP], preferred_element_type=F32)
        for s in range(PROJ_GROUP // LANES):
            ys = y[:, s * LANES:(s + 1) * LANES]
            if kind == "ret_rot":
                ys = _rotate(ys, tables[0][...], tables[1][...], tables[2][...], HEAD_DIM // 2)
            elif kind == "dil_rot":
                ys = _rotate(ys, tables[3][...], tables[4][...], tables[5][...], ROPE_DIM // 2)
            if scale != 1.0:
                ys = ys * scale
            o_ref[:, lo + s * LANES:lo + (s + 1) * LANES] = ys.astype(BF16)


def _proj(x, g, w, layer, plan, tables, seq):
    n = x.shape[0]
    width = w.shape[2]
    row = pl.BlockSpec((TOK_TILE, D_MODEL), lambda i: (i, 0))
    per_seq = seq // TOK_TILE
    tab = pl.BlockSpec((TOK_TILE, LANES), lambda i: (i % per_seq, 0))
    return pl.pallas_call(
        functools.partial(_proj_body, plan=plan),
        grid=(n // TOK_TILE,),
        in_specs=[row, _resident((1, D_MODEL)), _resident((D_MODEL, width), layer)]
        + [tab] * len(tables),
        out_specs=pl.BlockSpec((TOK_TILE, width), lambda i: (i, 0)),
        out_shape=jax.ShapeDtypeStruct((n, width), BF16),
        compiler_params=_cparams(1),
        name="proj",
    )(x, g.reshape(1, D_MODEL), w, *tables)


def _rotary_tables(seq, rot_dim, theta):
    half = rot_dim // 2
    lane = np.arange(LANES) % HEAD_DIM
    inv_freq = 1.0 / (theta ** (np.arange(half, dtype=np.float64) / half))
    ang = (np.arange(seq, dtype=np.float64)[:, None] * inv_freq[None, :])[:, lane % half]
    cos, sin = np.cos(ang), np.sin(ang)
    is_x1 = (lane < half)[None, :]
    is_x2 = ((lane >= half) & (lane < rot_dim))[None, :]
    cos_t = np.where(is_x1 | is_x2, cos, 1.0)
    sin_a = np.where(is_x1, -sin, 0.0)
    sin_b = np.where(is_x2, sin, 0.0)
    return tuple(jnp.asarray(t, dtype=F32) for t in (cos_t, sin_a, sin_b))


RET_TOK = 2048


def _retention_consts():
    h = np.arange(RET_HEADS, dtype=np.float64)
    log_g = np.log(1.0 - 2.0 ** (-5.0 - h))
    i = np.arange(RET_CHUNK, dtype=np.float64)
    diff = i[:, None] - i[None, :]
    decay_in = np.where(diff >= 0, np.exp(np.maximum(diff, 0.0)[None] * log_g[:, None, None]), 0.0)
    n_pairs = RET_HEADS // PAIR
    decay = decay_in.reshape(n_pairs, PAIR * RET_CHUNK, RET_CHUNK)
    lane_head = np.arange(LANES) // HEAD_DIM
    head_of = (np.arange(n_pairs)[:, None] * PAIR + lane_head[None, :])
    lg = log_g[head_of]
    q_dec = np.exp((i + 1.0)[None, :, None] * lg[:, None, :])
    k_dec = np.exp((RET_CHUNK - 1.0 - i)[None, :, None] * lg[:, None, :])
    same_head = (lane_head[:, None] == lane_head[None, :])
    c_dec = np.exp(RET_CHUNK * lg)[:, :, None] * same_head[None]
    bd = np.broadcast_to(same_head[None], c_dec.shape)
    return tuple(jnp.asarray(t, dtype=F32) for t in (decay, q_dec, k_dec, c_dec, bd))


def _retention_chunk(q, k, v, g, state, dec, q_dec, k_dec, c_dec, bd, gn):
    h0 = _lane_is_head0((RET_CHUNK, LANES))
    inv = 1.0 / HEAD_DIM
    sc = lax.dot_general(_stack_heads(q), k, (((1,), (1,)), ((), ())),
                         preferred_element_type=F32) * dec
    inner = _unstack_heads(jnp.dot(sc.astype(BF16), v, preferred_element_type=F32))
    qd = (q.astype(F32) * q_dec).astype(BF16)
    o = inner + jnp.dot(qd, state.astype(BF16), preferred_element_type=F32)
    kd = (k.astype(F32) * k_dec).astype(BF16)
    kv = lax.dot_general(kd, v, (((0,), (0,)), ((), ())), preferred_element_type=F32)
    state = state * c_dec + kv * bd
    s0 = jnp.sum(jnp.where(h0, o, 0.0), axis=-1, keepdims=True)
    s1 = jnp.sum(jnp.where(h0, 0.0, o), axis=-1, keepdims=True)
    d = o - jnp.where(h0, s0, s1) * inv
    d2 = d * d
    v0 = jnp.sum(jnp.where(h0, d2, 0.0), axis=-1, keepdims=True)
    v1 = jnp.sum(jnp.where(h0, 0.0, d2), axis=-1, keepdims=True)
    y = d * lax.rsqrt(jnp.where(h0, v0, v1) * inv + GN_EPS) * gn
    return (y * _silu(g.astype(F32))).astype(BF16), state


def _retention_body(q_ref, k_ref, v_ref, g_ref, dec_ref, qd_ref, kd_ref, cd_ref, bd_ref, gn_ref,
                    o_ref, state_ref):
    @pl.when(pl.program_id(1) == 0)
    def _():
        state_ref[...] = jnp.zeros_like(state_ref)

    for p in range(RET_HEADS // PAIR):
        cols = slice(p * LANES, (p + 1) * LANES)
        state = state_ref[p]
        for c in range(RET_TOK // RET_CHUNK):
            rows = slice(c * RET_CHUNK, (c + 1) * RET_CHUNK)
            o_ref[rows, cols], state = _retention_chunk(
                q_ref[rows, cols], k_ref[rows, cols], v_ref[rows, cols], g_ref[rows, cols], state,
                dec_ref[p], qd_ref[p], kd_ref[p], cd_ref[p], bd_ref[p], gn_ref[:, cols])
        state_ref[p] = state


def _retention(proj, ret_gn, batch, seq):
    n = proj.shape[0]
    per_seq = seq // RET_TOK
    consts = _retention_consts()

    def slab(j):
        return pl.BlockSpec((RET_TOK, RET_WIDTH), lambda b, i, j=j: (b * per_seq + i, j))

    return pl.pallas_call(
        _retention_body,
        grid=(batch, per_seq),
        in_specs=[slab(0), slab(1), slab(2), slab(3)]
        + [_resident(c.shape) for c in consts] + [_resident((1, RET_WIDTH))],
        out_specs=pl.BlockSpec((RET_TOK, RET_WIDTH), lambda b, i: (b * per_seq + i, 0)),
        out_shape=jax.ShapeDtypeStruct((n, RET_WIDTH), BF16),
        scratch_shapes=[pltpu.VMEM((RET_HEADS // PAIR, LANES, LANES), F32)],
        compiler_params=_cparams(2),
        name="retention",
    )(proj, proj, proj, proj, *consts, ret_gn.reshape(1, RET_WIDTH))


DIL_SPAN = BLOCK * max(DILATIONS)


def _band_bias():
    qi = np.arange(PAIR * BLOCK)[:, None] % BLOCK
    kj = np.arange(2 * BLOCK)[None, :]
    return jnp.asarray(np.where((kj >= qi) & (kj <= qi + BLOCK), 0.0, NEG_BIG), dtype=F32)


def _dilated_body(q_ref, kc_ref, kp_ref, vc_ref, vp_ref, bias_ref, out_ref,
                  qf_ref, kf_ref, vf_ref, num_ref, den_ref, max_ref):
    first = pl.program_id(2) == 0
    qf_ref[...] = q_ref[...].astype(F32)
    kf_ref[0:DIL_SPAN, :] = kp_ref[...].astype(F32)
    kf_ref[DIL_SPAN:, :] = kc_ref[...].astype(F32)
    vf_ref[0:DIL_SPAN, :] = vp_ref[...].astype(F32)
    vf_ref[DIL_SPAN:, :] = vc_ref[...].astype(F32)
    k_min = jnp.where(first, BLOCK, 0)
    kj = lax.broadcasted_iota(jnp.int32, (PAIR * BLOCK, 2 * BLOCK), 1)
    ones = jnp.ones((2 * BLOCK, LANES), BF16)
    head0 = _lane_is_head0((BLOCK, LANES))

    for pi, dil in enumerate(DILATIONS):
        for r in range(dil):
            for n in range(DIL_SPAN // (BLOCK * dil)):
                q0 = n * (BLOCK * dil) + r
                k0 = q0 + DIL_SPAN - BLOCK * dil
                q = qf_ref[pl.ds(q0, BLOCK, stride=dil), :].astype(BF16)
                keys = kf_ref[pl.ds(k0, 2 * BLOCK, stride=dil), :].astype(BF16)
                vals = vf_ref[pl.ds(k0, 2 * BLOCK, stride=dil), :].astype(BF16)
                s = lax.dot_general(_stack_heads(q), keys, (((1,), (1,)), ((), ())),
                                    preferred_element_type=F32) + bias_ref[...]
                if n == 0:
                    s = jnp.where(kj >= k_min, s, NEG_BIG)
                m = jnp.max(s, axis=-1, keepdims=True)
                p = jnp.exp2(s - m)
                pv = jnp.dot(p.astype(BF16), jnp.concatenate([vals, ones], axis=1),
                             preferred_element_type=F32)
                rows = pl.ds(q0, BLOCK, stride=dil)
                num_ref[pi, rows, :] = _unstack_heads(pv[:, :LANES])
                den_ref[pi, rows, :] = _unstack_heads(pv[:, LANES:])
                max_ref[pi, rows, :] = jnp.where(head0, m[:BLOCK], m[BLOCK:])

    patterns = range(len(DILATIONS))
    m = functools.reduce(jnp.maximum, [max_ref[pi] for pi in patterns])
    e = [jnp.exp2(max_ref[pi] - m) for pi in patterns]
    num = functools.reduce(lambda a, b: a + b, [e[pi] * num_ref[pi] for pi in patterns])
    den = functools.reduce(lambda a, b: a + b, [e[pi] * den_ref[pi] for pi in patterns])
    out_ref[...] = (num / den).astype(BF16)


def _dilated_attention(proj, batch, seq):
    n = proj.shape[0]
    q0, k0, v0 = ((4 * RET_WIDTH + t * DIL_WIDTH) // LANES for t in range(3))
    per_seq = seq // DIL_SPAN
    n_pairs = DIL_HEADS // PAIR

    def cur(c0):
        return pl.BlockSpec((DIL_SPAN, LANES), lambda b, p, i: (b * per_seq + i, c0 + p))

    def prev(c0):
        return pl.BlockSpec((DIL_SPAN, LANES),
                            lambda b, p, i: (jnp.maximum(b * per_seq + i - 1, 0), c0 + p))

    return pl.pallas_call(
        _dilated_body,
        grid=(batch, n_pairs, per_seq),
        in_specs=[cur(q0), cur(k0), prev(k0), cur(v0), prev(v0),
                  _resident((PAIR * BLOCK, 2 * BLOCK))],
        out_specs=pl.BlockSpec((DIL_SPAN, LANES), lambda b, p, i: (b * per_seq + i, p)),
        out_shape=jax.ShapeDtypeStruct((n, DIL_WIDTH), BF16),
        scratch_shapes=[pltpu.VMEM((DIL_SPAN, LANES), F32),
                        pltpu.VMEM((2 * DIL_SPAN, LANES), F32),
                        pltpu.VMEM((2 * DIL_SPAN, LANES), F32),
                        pltpu.VMEM((len(DILATIONS), DIL_SPAN, LANES), F32),
                        pltpu.VMEM((len(DILATIONS), DIL_SPAN, LANES), F32),
                        pltpu.VMEM((len(DILATIONS), DIL_SPAN, LANES), F32)],
        compiler_params=_cparams(3),
        name="dilated",
    )(proj, proj, proj, proj, proj, _band_bias())


SB_ROWS = 512
SB_PAIRS = 4
SB_QBLOCKS = 2
SB_SLOT_ROWS = SB_PAIRS * PAIR * SB_TQ
SB_DEAD = -105.0


def _suffix_matrix():
    return jnp.asarray(np.arange(SB_TK)[:, None] > np.arange(SB_TK)[None, :], dtype=BF16)


def _sb_visit(qs_ref, k_ref, v_ref, tri_ref, acc_ref, carry_ref, slot, first_kb, n_blk, diagonal):
    start = pl.multiple_of(first_kb * SB_TK, SB_TK)
    width = n_blk * SB_TK
    tri = tri_ref[...]
    for rc in range(SB_SLOT_ROWS // SB_ROWS):
        rows = slice(slot * SB_SLOT_ROWS + rc * SB_ROWS, slot * SB_SLOT_ROWS + (rc + 1) * SB_ROWS)
        pair = (rc * SB_ROWS) // (PAIR * SB_TQ)
        kwin = k_ref[pl.ds(start, width), pair * LANES:(pair + 1) * LANES]
        vwin = v_ref[pl.ds(start, width), pair * LANES:(pair + 1) * LANES]
        z = lax.dot_general(qs_ref[rows, :], kwin, (((1,), (1,)), ((), ())),
                            preferred_element_type=F32)
        if diagonal:
            own = (SB_ROWS, SB_TK)
            q_row = (rc * SB_ROWS + lax.broadcasted_iota(jnp.int32, own, 0)) & (SB_TQ - 1)
            causal = lax.broadcasted_iota(jnp.int32, own, 1) < q_row
            masked = jnp.where(causal, z[:, width - SB_TK:], NEG_BIG)
            z = masked if width == SB_TK else jnp.concatenate([z[:, :width - SB_TK], masked], axis=1)
        log_go = jnp.minimum(z, 0.0) - jnp.log(1.0 + jnp.exp2(jnp.abs(z) * -LOG2E))
        log_stay = log_go - z
        blocks = [slice(b * SB_TK, (b + 1) * SB_TK) for b in range(n_blk)]
        stay16 = log_stay.astype(BF16)
        within = jnp.dot(jnp.concatenate([stay16[:, c] for c in blocks], axis=0), tri,
                         preferred_element_type=F32)
        after = None if diagonal else carry_ref[rows, :]
        later = [None] * n_blk
        for b in reversed(range(n_blk)):
            later[b] = within[b * SB_ROWS:(b + 1) * SB_ROWS]
            if after is not None:
                later[b] = later[b] + jnp.concatenate([after] * (SB_TK // LANES), axis=1)
            total = jnp.broadcast_to(jnp.sum(log_stay[:, blocks[b]], axis=-1, keepdims=True),
                                     (SB_ROWS, LANES))
            after = total if after is None else after + total
        a = jnp.exp(log_go + jnp.concatenate(later, axis=1))
        pv = jnp.dot(a.astype(BF16), vwin, preferred_element_type=F32)
        if diagonal:
            acc_ref[rows, :] = pv
        else:
            acc_ref[rows, :] += pv
        carry_ref[rows, :] = after


def _sb_body(q_ref, k_ref, v_ref, tri_ref, o_ref, qs_ref, acc_ref, carry_ref):
    step_i = pl.program_id(2)
    stacked = PAIR * SB_TQ
    units = [(slot, slice(slot * SB_TQ, (slot + 1) * SB_TQ), slice(pair * LANES, (pair + 1) * LANES),
              slice(slot * SB_SLOT_ROWS + pair * stacked, slot * SB_SLOT_ROWS + (pair + 1) * stacked))
             for slot in range(SB_QBLOCKS) for pair in range(SB_PAIRS)]
    for _, q_rows, lanes, rows in units:
        qs_ref[rows, :] = _stack_heads(q_ref[q_rows, lanes])
    visit = functools.partial(_sb_visit, qs_ref, k_ref, v_ref, tri_ref, acc_ref, carry_ref)
    q_blk = [step_i * SB_QBLOCKS + slot for slot in range(SB_QBLOCKS)]

    @pl.when(step_i == 0)
    def _():
        visit(0, 0, 1, True)
        for slot in range(1, SB_QBLOCKS):
            visit(slot, slot - 1, 2, True)

    @pl.when(step_i > 0)
    def _():
        for slot in range(SB_QBLOCKS):
            visit(slot, q_blk[slot] - 1, 2, True)

    for slot in range(SB_QBLOCKS):
        slot_rows = slice(slot * SB_SLOT_ROWS, (slot + 1) * SB_SLOT_ROWS)

        def alive(state, slot=slot):
            t, top = state
            return (t <= q_blk[slot]) & (top > SB_DEAD)

        def step(state, slot=slot, slot_rows=slot_rows):
            t, _ = state
            visit(slot, q_blk[slot] - t, 1, False)
            return t + 1, jnp.max(carry_ref[slot_rows, :])

        lax.while_loop(alive, step, (2, jnp.max(carry_ref[slot_rows, :])))
    for _, q_rows, lanes, rows in units:
        o_ref[q_rows, lanes] = _unstack_heads(acc_ref[rows, :]).astype(BF16)


def _stick_breaking(proj, batch, seq):
    n = proj.shape[0]
    groups = SB_HEADS // PAIR // SB_PAIRS
    per_seq = seq // (SB_QBLOCKS * SB_TQ)
    wide = SB_PAIRS * LANES
    qo = lambda b, p, i: (b * per_seq + i, p)
    return pl.pallas_call(
        _sb_body,
        grid=(batch, groups, per_seq),
        in_specs=[pl.BlockSpec((SB_QBLOCKS * SB_TQ, wide), qo),
                  pl.BlockSpec((seq, wide), lambda b, p, i: (b, groups + p)),
                  pl.BlockSpec((seq, wide), lambda b, p, i: (b, 2 * groups + p)),
                  _resident((SB_TK, SB_TK))],
        out_specs=pl.BlockSpec((SB_QBLOCKS * SB_TQ, wide), qo),
        out_shape=jax.ShapeDtypeStruct((n, SB_WIDTH), BF16),
        scratch_shapes=[pltpu.VMEM((SB_QBLOCKS * SB_SLOT_ROWS, LANES), BF16),
                        pltpu.VMEM((SB_QBLOCKS * SB_SLOT_ROWS, LANES), F32),
                        pltpu.VMEM((SB_QBLOCKS * SB_SLOT_ROWS, LANES), F32)],
        compiler_params=_cparams(3),
        name="stick_breaking",
    )(proj, proj, proj, _suffix_matrix())


HYB_PLAN = (("ret_rot", 1.0), ("ret_rot", HEAD_DIM ** -0.5), ("plain", 1.0), ("plain", 1.0),
            ("dil_rot", HEAD_DIM ** -0.5 * LOG2E), ("dil_rot", 1.0), ("plain", 1.0))
SB_PLAN = (("plain", HEAD_DIM ** -0.5),) * 2 + (("plain", 1.0),) * 4


def kernel(x, ffn1_norm, ffn1_w_in, ffn1_w_out, mix_norm, ffn2_norm, ffn2_w_in, ffn2_w_out,
           hyb_w_in, ret_gn, hyb_w_out, sb_w_in, sb_w_out, final_norm):
    batch, seq, d_model = x.shape
    depth = ffn1_norm.shape[0]
    assert d_model == D_MODEL and x.dtype == F32
    assert seq % max(TOK_TILE, FFN_TILE, RET_TOK, DIL_SPAN, SB_QBLOCKS * SB_TQ) == 0
    tables = (_rotary_tables(seq, HEAD_DIM, RET_ROPE_THETA)
              + _rotary_tables(seq, ROPE_DIM, ROPE_THETA))
    (ffn1_w_in, ffn1_w_out, ffn2_w_in, ffn2_w_out, hyb_w_in, hyb_w_out, sb_w_in, sb_w_out) = (
        w.astype(BF16) for w in (ffn1_w_in, ffn1_w_out, ffn2_w_in, ffn2_w_out,
                                 hyb_w_in, hyb_w_out, sb_w_in, sb_w_out))
    h = x.reshape(batch * seq, D_MODEL)
    for layer in range(depth):
        h = _ffn(h, ffn1_norm[layer], ffn1_w_in, ffn1_w_out, layer)
        if layer % 2 == 0:
            e = layer // 2
            proj = _proj(h, mix_norm[layer], hyb_w_in, e, HYB_PLAN, tables, seq)
            mix = ([_retention(proj, ret_gn[e], batch, seq),
                    _dilated_attention(proj, batch, seq)], hyb_w_out, e)
        else:
            o = layer // 2
            proj = _proj(h, mix_norm[layer], sb_w_in, o, SB_PLAN, (), seq)
            mix = ([_stick_breaking(proj, batch, seq)], sb_w_out, o)
        h = _ffn(h, ffn2_norm[layer], ffn2_w_in, ffn2_w_out, layer, mix,
                 final_norm if layer == depth - 1 else None)
    return h.reshape(batch, seq, D_MODEL)
```

```python
import functools

import numpy as np
import jax
import jax.numpy as jnp
from jax import lax
from jax.experimental import pallas as pl
from jax.experimental.pallas import tpu as pltpu

F32 = jnp.float32
BF16 = jnp.bfloat16

D_MODEL = 1024
HEAD_DIM = 64
RET_HEADS = 8
DIL_HEADS = 8
SB_HEADS = 16
RET_WIDTH = RET_HEADS * HEAD_DIM
DIL_WIDTH = DIL_HEADS * HEAD_DIM
SB_WIDTH = SB_HEADS * HEAD_DIM
D_FF = 2816
BLOCK = 128
RET_CHUNK = 128
RET_ROPE_THETA = 10000.0
ROPE_THETA = 500000.0
ROPE_DIM = HEAD_DIM // 4
DILATIONS = (1, 4, 16)
NORM_EPS = 1e-6
GN_EPS = 1e-5

LANES = 128
PAIR = LANES // HEAD_DIM
VMEM_LIMIT = 56 * 1024 * 1024
NEG_BIG = -1e30
LOG2E = 1.4426950408889634

TOK_TILE = 1024
FFN_TILE = 1024
FF_CHUNK = 256
SB_TQ = 256
SB_TK = 256


def _cparams(n_axes):
    return pltpu.CompilerParams(
        dimension_semantics=("arbitrary",) * n_axes,
        vmem_limit_bytes=VMEM_LIMIT)


def _resident(shape, layer=None):
    if layer is None:
        return pl.BlockSpec(shape, lambda *_: (0,) * len(shape), pipeline_mode=pl.Buffered(1))
    return pl.BlockSpec((None,) + tuple(shape), lambda *_: (layer,) + (0,) * len(shape),
                        pipeline_mode=pl.Buffered(1))


def _rms(x, g):
    return x * lax.rsqrt(jnp.mean(x * x, axis=-1, keepdims=True) + NORM_EPS) * g


def _silu(x):
    return x / (1.0 + jnp.exp(-x))


def _lane_is_head0(shape):
    return lax.broadcasted_iota(jnp.int32, shape, len(shape) - 1) < HEAD_DIM


def _stack_heads(q):
    h0 = _lane_is_head0(q.shape)
    zero = jnp.zeros_like(q)
    return jnp.concatenate([jnp.where(h0, q, zero), jnp.where(h0, zero, q)], axis=0)


def _unstack_heads(o2):
    t = o2.shape[0] // 2
    return jnp.where(_lane_is_head0((t, LANES)), o2[:t], o2[t:])


def _ffn_body(*refs, n_slabs, final_norm):
    refs = list(refs)
    x_ref = refs.pop(0)
    slabs = [refs.pop(0) for _ in range(n_slabs)]
    wmix_ref = refs.pop(0) if n_slabs else None
    g_ref, win_ref, wout_ref = refs.pop(0), refs.pop(0), refs.pop(0)
    fg_ref = refs.pop(0) if final_norm else None
    o_ref, act_ref = refs

    x = x_ref[...]
    lo = 0
    for a_ref in slabs:
        width = a_ref.shape[1]
        x = x + jnp.dot(a_ref[...], wmix_ref[lo:lo + width, :], preferred_element_type=F32)
        lo += width
    hb = _rms(x, g_ref[...]).astype(BF16)
    for c in range(D_FF // FF_CHUNK):
        cols = slice(c * FF_CHUNK, (c + 1) * FF_CHUNK)
        up_cols = slice(D_FF + c * FF_CHUNK, D_FF + (c + 1) * FF_CHUNK)
        gate = jnp.dot(hb, win_ref[:, cols], preferred_element_type=F32)
        up = jnp.dot(hb, win_ref[:, up_cols], preferred_element_type=F32)
        act_ref[:, cols] = (_silu(gate) * up).astype(BF16)
    y = x + 0.5 * jnp.dot(act_ref[...], wout_ref[...], preferred_element_type=F32)
    if final_norm:
        y = _rms(y, fg_ref[...])
    o_ref[...] = y


def _ffn(x, g, w_in, w_out, layer, mix=None, final_g=None):
    n = x.shape[0]
    row = pl.BlockSpec((FFN_TILE, D_MODEL), lambda i: (i, 0))
    args, specs = [x], [row]
    slabs = ()
    if mix is not None:
        slabs, w_mix, mix_layer = mix
        args += [*slabs, w_mix]
        specs += [pl.BlockSpec((FFN_TILE, a.shape[1]), lambda i: (i, 0)) for a in slabs]
        specs += [_resident(w_mix.shape[1:], mix_layer)]
    args += [g.reshape(1, D_MODEL), w_in, w_out]
    specs += [_resident((1, D_MODEL)), _resident((D_MODEL, 2 * D_FF), layer),
              _resident((D_FF, D_MODEL), layer)]
    if final_g is not None:
        args.append(final_g.reshape(1, D_MODEL))
        specs.append(_resident((1, D_MODEL)))
    return pl.pallas_call(
        functools.partial(_ffn_body, n_slabs=len(slabs), final_norm=final_g is not None),
        grid=(n // FFN_TILE,),
        in_specs=specs,
        out_specs=row,
        out_shape=jax.ShapeDtypeStruct((n, D_MODEL), F32),
        scratch_shapes=[pltpu.VMEM((FFN_TILE, D_FF), BF16)],
        compiler_params=_cparams(1),
        name="ffn",
    )(*args)


PROJ_GROUP = 512


def _rotate(y, cos, sin_a, sin_b, half):
    return (y * cos + pltpu.roll(y, LANES - half, 1) * sin_a + pltpu.roll(y, half, 1) * sin_b)


def _proj_body(x_ref, g_ref, w_ref, *rest, plan):
    tables, o_ref = rest[:-1], rest[-1]
    hb = _rms(x_ref[...], g_ref[...]).astype(BF16)
    for gi, (kind, scale) in enumerate(plan):
        lo = gi * PROJ_GROUP
        y = jnp.dot(hb, w_ref[:, lo:lo + PROJ_GROUP], preferred_element_type=F32)
        for s in range(PROJ_GROUP // LANES):
            ys = y[:, s * LANES:(s + 1) * LANES]
            if kind == "ret_rot":
                ys = _rotate(ys, tables[0][...], tables[1][...], tables[2][...], HEAD_DIM // 2)
            elif kind == "dil_rot":
                ys = _rotate(ys, tables[3][...], tables[4][...], tables[5][...], ROPE_DIM // 2)
            if scale != 1.0:
                ys = ys * scale
            o_ref[:, lo + s * LANES:lo + (s + 1) * LANES] = ys.astype(BF16)


def _proj(x, g, w, layer, plan, tables, seq):
    n = x.shape[0]
    width = w.shape[2]
    row = pl.BlockSpec((TOK_TILE, D_MODEL), lambda i: (i, 0))
    per_seq = seq // TOK_TILE
    tab = pl.BlockSpec((TOK_TILE, LANES), lambda i: (i % per_seq, 0))
    return pl.pallas_call(
        functools.partial(_proj_body, plan=plan),
        grid=(n // TOK_TILE,),
        in_specs=[row, _resident((1, D_MODEL)), _resident((D_MODEL, width), layer)]
        + [tab] * len(tables),
        out_specs=pl.BlockSpec((TOK_TILE, width), lambda i: (i, 0)),
        out_shape=jax.ShapeDtypeStruct((n, width), BF16),
        compiler_params=_cparams(1),
        name="proj",
    )(x, g.reshape(1, D_MODEL), w, *tables)


def _rotary_tables(seq, rot_dim, theta):
    half = rot_dim // 2
    lane = np.arange(LANES) % HEAD_DIM
    inv_freq = 1.0 / (theta ** (np.arange(half, dtype=np.float64) / half))
    ang = (np.arange(seq, dtype=np.float64)[:, None] * inv_freq[None, :])[:, lane % half]
    cos, sin = np.cos(ang), np.sin(ang)
    is_x1 = (lane < half)[None, :]
    is_x2 = ((lane >= half) & (lane < rot_dim))[None, :]
    cos_t = np.where(is_x1 | is_x2, cos, 1.0)
    sin_a = np.where(is_x1, -sin, 0.0)
    sin_b = np.where(is_x2, sin, 0.0)
    return tuple(jnp.asarray(t, dtype=F32) for t in (cos_t, sin_a, sin_b))


RET_TOK = 1024


def _retention_consts():
    h = np.arange(RET_HEADS, dtype=np.float64)
    log_g = np.log(1.0 - 2.0 ** (-5.0 - h))
    i = np.arange(RET_CHUNK, dtype=np.float64)
    diff = i[:, None] - i[None, :]
    decay_in = np.where(diff >= 0, np.exp(np.maximum(diff, 0.0)[None] * log_g[:, None, None]), 0.0)
    n_pairs = RET_HEADS // PAIR
    decay = decay_in.reshape(n_pairs, PAIR * RET_CHUNK, RET_CHUNK)
    lane_head = np.arange(LANES) // HEAD_DIM
    head_of = (np.arange(n_pairs)[:, None] * PAIR + lane_head[None, :])
    lg = log_g[head_of]
    q_dec = np.exp((i + 1.0)[None, :, None] * lg[:, None, :])
    k_dec = np.exp((RET_CHUNK - 1.0 - i)[None, :, None] * lg[:, None, :])
    same_head = (lane_head[:, None] == lane_head[None, :])
    c_dec = np.exp(RET_CHUNK * lg)[:, :, None] * same_head[None]
    bd = np.broadcast_to(same_head[None], c_dec.shape)
    return tuple(jnp.asarray(t, dtype=F32) for t in (decay, q_dec, k_dec, c_dec, bd))


def _retention_chunk(q, k, v, g, state, dec, q_dec, k_dec, c_dec, bd, gn):
    h0 = _lane_is_head0((RET_CHUNK, LANES))
    inv = 1.0 / HEAD_DIM
    sc = lax.dot_general(_stack_heads(q), k, (((1,), (1,)), ((), ())),
                         preferred_element_type=F32) * dec
    inner = _unstack_heads(jnp.dot(sc.astype(BF16), v, preferred_element_type=F32))
    qd = (q.astype(F32) * q_dec).astype(BF16)
    o = inner + jnp.dot(qd, state.astype(BF16), preferred_element_type=F32)
    kd = (k.astype(F32) * k_dec).astype(BF16)
    kv = lax.dot_general(kd, v, (((0,), (0,)), ((), ())), preferred_element_type=F32)
    state = state * c_dec + kv * bd
    s0 = jnp.sum(jnp.where(h0, o, 0.0), axis=-1, keepdims=True)
    s1 = jnp.sum(jnp.where(h0, 0.0, o), axis=-1, keepdims=True)
    d = o - jnp.where(h0, s0, s1) * inv
    d2 = d * d
    v0 = jnp.sum(jnp.where(h0, d2, 0.0), axis=-1, keepdims=True)
    v1 = jnp.sum(jnp.where(h0, 0.0, d2), axis=-1, keepdims=True)
    y = d * lax.rsqrt(jnp.where(h0, v0, v1) * inv + GN_EPS) * gn
    return (y * _silu(g.astype(F32))).astype(BF16), state


def _retention_body(q_ref, k_ref, v_ref, g_ref, dec_ref, qd_ref, kd_ref, cd_ref, bd_ref, gn_ref,
                    o_ref, state_ref):
    @pl.when(pl.program_id(1) == 0)
    def _():
        state_ref[...] = jnp.zeros_like(state_ref)

    for p in range(RET_HEADS // PAIR):
        cols = slice(p * LANES, (p + 1) * LANES)
        state = state_ref[p]
        for c in range(RET_TOK // RET_CHUNK):
            rows = slice(c * RET_CHUNK, (c + 1) * RET_CHUNK)
            o_ref[rows, cols], state = _retention_chunk(
                q_ref[rows, cols], k_ref[rows, cols], v_ref[rows, cols], g_ref[rows, cols], state,
                dec_ref[p], qd_ref[p], kd_ref[p], cd_ref[p], bd_ref[p], gn_ref[:, cols])
        state_ref[p] = state


def _retention(proj, ret_gn, batch, seq):
    n = proj.shape[0]
    per_seq = seq // RET_TOK
    consts = _retention_consts()

    def slab(j):
        return pl.BlockSpec((RET_TOK, RET_WIDTH), lambda b, i, j=j: (b * per_seq + i, j))

    return pl.pallas_call(
        _retention_body,
        grid=(batch, per_seq),
        in_specs=[slab(0), slab(1), slab(2), slab(3)]
        + [_resident(c.shape) for c in consts] + [_resident((1, RET_WIDTH))],
        out_specs=pl.BlockSpec((RET_TOK, RET_WIDTH), lambda b, i: (b * per_seq + i, 0)),
        out_shape=jax.ShapeDtypeStruct((n, RET_WIDTH), BF16),
        scratch_shapes=[pltpu.VMEM((RET_HEADS // PAIR, LANES, LANES), F32)],
        compiler_params=_cparams(2),
        name="retention",
    )(proj, proj, proj, proj, *consts, ret_gn.reshape(1, RET_WIDTH))


DIL_SPAN = BLOCK * max(DILATIONS)


def _band_bias():
    qi = np.arange(PAIR * BLOCK)[:, None] % BLOCK
    kj = np.arange(2 * BLOCK)[None, :]
    return jnp.asarray(np.where((kj >= qi) & (kj <= qi + BLOCK), 0.0, NEG_BIG), dtype=F32)


def _dilated_body(q_ref, kc_ref, kp_ref, vc_ref, vp_ref, bias_ref, out_ref,
                  qf_ref, kf_ref, vf_ref, num_ref, den_ref, max_ref):
    first = pl.program_id(2) == 0
    qf_ref[...] = q_ref[...].astype(F32)
    kf_ref[0:DIL_SPAN, :] = kp_ref[...].astype(F32)
    kf_ref[DIL_SPAN:, :] = kc_ref[...].astype(F32)
    vf_ref[0:DIL_SPAN, :] = vp_ref[...].astype(F32)
    vf_ref[DIL_SPAN:, :] = vc_ref[...].astype(F32)
    k_min = jnp.where(first, BLOCK, 0)
    kj = lax.broadcasted_iota(jnp.int32, (PAIR * BLOCK, 2 * BLOCK), 1)
    ones = jnp.ones((2 * BLOCK, LANES), BF16)
    head0 = _lane_is_head0((BLOCK, LANES))

    for pi, dil in enumerate(DILATIONS):
        for r in range(dil):
            for n in range(DIL_SPAN // (BLOCK * dil)):
                q0 = n * (BLOCK * dil) + r
                k0 = q0 + DIL_SPAN - BLOCK * dil
                q = qf_ref[pl.ds(q0, BLOCK, stride=dil), :].astype(BF16)
                keys = kf_ref[pl.ds(k0, 2 * BLOCK, stride=dil), :].astype(BF16)
                vals = vf_ref[pl.ds(k0, 2 * BLOCK, stride=dil), :].astype(BF16)
                s = lax.dot_general(_stack_heads(q), keys, (((1,), (1,)), ((), ())),
                                    preferred_element_type=F32) + bias_ref[...]
                if n == 0:
                    s = jnp.where(kj >= k_min, s, NEG_BIG)
                m = jnp.max(s, axis=-1, keepdims=True)
                p = jnp.exp2(s - m)
                pv = jnp.dot(p.astype(BF16), jnp.concatenate([vals, ones], axis=1),
                             preferred_element_type=F32)
                rows = pl.ds(q0, BLOCK, stride=dil)
                num_ref[pi, rows, :] = _unstack_heads(pv[:, :LANES])
                den_ref[pi, rows, :] = _unstack_heads(pv[:, LANES:])
                max_ref[pi, rows, :] = jnp.where(head0, m[:BLOCK], m[BLOCK:])

    patterns = range(len(DILATIONS))
    m = functools.reduce(jnp.maximum, [max_ref[pi] for pi in patterns])
    e = [jnp.exp2(max_ref[pi] - m) for pi in patterns]
    num = functools.reduce(lambda a, b: a + b, [e[pi] * num_ref[pi] for pi in patterns])
    den = functools.reduce(lambda a, b: a + b, [e[pi] * den_ref[pi] for pi in patterns])
    out_ref[...] = (num / den).astype(BF16)


def _dilated_attention(proj, batch, seq):
    n = proj.shape[0]
    q0, k0, v0 = ((4 * RET_WIDTH + t * DIL_WIDTH) // LANES for t in range(3))
    per_seq = seq // DIL_SPAN
    n_pairs = DIL_HEADS // PAIR

    def cur(c0):
        return pl.BlockSpec((DIL_SPAN, LANES), lambda b, p, i: (b * per_seq + i, c0 + p))

    def prev(c0):
        return pl.BlockSpec((DIL_SPAN, LANES),
                            lambda b, p, i: (jnp.maximum(b * per_seq + i - 1, 0), c0 + p))

    return pl.pallas_call(
        _dilated_body,
        grid=(batch, n_pairs, per_seq),
        in_specs=[cur(q0), cur(k0), prev(k0), cur(v0), prev(v0),
                  _resident((PAIR * BLOCK, 2 * BLOCK))],
        out_specs=pl.BlockSpec((DIL_SPAN, LANES), lambda b, p, i: (b * per_seq + i, p)),
        out_shape=jax.ShapeDtypeStruct((n, DIL_WIDTH), BF16),
        scratch_shapes=[pltpu.VMEM((DIL_SPAN, LANES), F32),
                        pltpu.VMEM((2 * DIL_SPAN, LANES), F32),
                        pltpu.VMEM((2 * DIL_SPAN, LANES), F32),
                        pltpu.VMEM((len(DILATIONS), DIL_SPAN, LANES), F32),
                        pltpu.VMEM((len(DILATIONS), DIL_SPAN, LANES), F32),
                        pltpu.VMEM((len(DILATIONS), DIL_SPAN, LANES), F32)],
        compiler_params=_cparams(3),
        name="dilated",
    )(proj, proj, proj, proj, proj, _band_bias())


SB_ROWS = 512
SB_PAIRS = 4
SB_QBLOCKS = 2
SB_SLOT_ROWS = SB_PAIRS * PAIR * SB_TQ
SB_DEAD = -105.0


def _suffix_matrix():
    return jnp.asarray(np.arange(SB_TK)[:, None] > np.arange(SB_TK)[None, :], dtype=BF16)


def _sb_visit(qs_ref, k_ref, v_ref, tri_ref, acc_ref, carry_ref, slot, first_kb, n_blk, diagonal):
    start = pl.multiple_of(first_kb * SB_TK, SB_TK)
    width = n_blk * SB_TK
    tri = tri_ref[...]
    for rc in range(SB_SLOT_ROWS // SB_ROWS):
        rows = slice(slot * SB_SLOT_ROWS + rc * SB_ROWS, slot * SB_SLOT_ROWS + (rc + 1) * SB_ROWS)
        pair = (rc * SB_ROWS) // (PAIR * SB_TQ)
        kwin = k_ref[pl.ds(start, width), pair * LANES:(pair + 1) * LANES]
        vwin = v_ref[pl.ds(start, width), pair * LANES:(pair + 1) * LANES]
        z = lax.dot_general(qs_ref[rows, :], kwin, (((1,), (1,)), ((), ())),
                            preferred_element_type=F32)
        if diagonal:
            own = (SB_ROWS, SB_TK)
            q_row = (rc * SB_ROWS + lax.broadcasted_iota(jnp.int32, own, 0)) & (SB_TQ - 1)
            causal = lax.broadcasted_iota(jnp.int32, own, 1) < q_row
            masked = jnp.where(causal, z[:, width - SB_TK:], NEG_BIG)
            z = masked if width == SB_TK else jnp.concatenate([z[:, :width - SB_TK], masked], axis=1)
        log_go = jnp.minimum(z, 0.0) - jnp.log(1.0 + jnp.exp2(jnp.abs(z) * -LOG2E))
        log_stay = log_go - z
        blocks = [slice(b * SB_TK, (b + 1) * SB_TK) for b in range(n_blk)]
        stay16 = log_stay.astype(BF16)
        within = jnp.dot(jnp.concatenate([stay16[:, c] for c in blocks], axis=0), tri,
                         preferred_element_type=F32)
        after = None if diagonal else carry_ref[rows, :]
        later = [None] * n_blk
        for b in reversed(range(n_blk)):
            later[b] = within[b * SB_ROWS:(b + 1) * SB_ROWS]
            if after is not None:
                later[b] = later[b] + jnp.concatenate([after] * (SB_TK // LANES), axis=1)
            total = jnp.broadcast_to(jnp.sum(log_stay[:, blocks[b]], axis=-1, keepdims=True),
                                     (SB_ROWS, LANES))
            after = total if after is None else after + total
        a = jnp.exp(log_go + jnp.concatenate(later, axis=1))
        pv = jnp.dot(a.astype(BF16), vwin, preferred_element_type=F32)
        if diagonal:
            acc_ref[rows, :] = pv
        else:
            acc_ref[rows, :] += pv
        carry_ref[rows, :] = after


def _sb_body(q_ref, k_ref, v_ref, tri_ref, o_ref, qs_ref, acc_ref, carry_ref):
    step_i = pl.program_id(2)
    stacked = PAIR * SB_TQ
    units = [(slot, slice(slot * SB_TQ, (slot + 1) * SB_TQ), slice(pair * LANES, (pair + 1) * LANES),
              slice(slot * SB_SLOT_ROWS + pair * stacked, slot * SB_SLOT_ROWS + (pair + 1) * stacked))
             for slot in range(SB_QBLOCKS) for pair in range(SB_PAIRS)]
    for _, q_rows, lanes, rows in units:
        qs_ref[rows, :] = _stack_heads(q_ref[q_rows, lanes])
    visit = functools.partial(_sb_visit, qs_ref, k_ref, v_ref, tri_ref, acc_ref, carry_ref)
    q_blk = [step_i * SB_QBLOCKS + slot for slot in range(SB_QBLOCKS)]

    @pl.when(step_i == 0)
    def _():
        visit(0, 0, 1, True)
        for slot in range(1, SB_QBLOCKS):
            visit(slot, slot - 1, 2, True)

    @pl.when(step_i > 0)
    def _():
        for slot in range(SB_QBLOCKS):
            visit(slot, q_blk[slot] - 1, 2, True)

    for slot in range(SB_QBLOCKS):
        slot_rows = slice(slot * SB_SLOT_ROWS, (slot + 1) * SB_SLOT_ROWS)

        def alive(state, slot=slot):
            t, top = state
            return (t <= q_blk[slot]) & (top > SB_DEAD)

        def step(state, slot=slot, slot_rows=slot_rows):
            t, _ = state
            visit(slot, q_blk[slot] - t, 1, False)
            return t + 1, jnp.max(carry_ref[slot_rows, :])

        lax.while_loop(alive, step, (2, jnp.max(carry_ref[slot_rows, :])))
    for _, q_rows, lanes, rows in units:
        o_ref[q_rows, lanes] = _unstack_heads(acc_ref[rows, :]).astype(BF16)


def _stick_breaking(proj, batch, seq):
    n = proj.shape[0]
    groups = SB_HEADS // PAIR // SB_PAIRS
    per_seq = seq // (SB_QBLOCKS * SB_TQ)
    wide = SB_PAIRS * LANES
    qo = lambda b, p, i: (b * per_seq + i, p)
    return pl.pallas_call(
        _sb_body,
        grid=(batch, groups, per_seq),
        in_specs=[pl.BlockSpec((SB_QBLOCKS * SB_TQ, wide), qo),
                  pl.BlockSpec((seq, wide), lambda b, p, i: (b, groups + p)),
                  pl.BlockSpec((seq, wide), lambda b, p, i: (b, 2 * groups + p)),
                  _resident((SB_TK, SB_TK))],
        out_specs=pl.BlockSpec((SB_QBLOCKS * SB_TQ, wide), qo),
        out_shape=jax.ShapeDtypeStruct((n, SB_WIDTH), BF16),
        scratch_shapes=[pltpu.VMEM((SB_QBLOCKS * SB_SLOT_ROWS, LANES), BF16),
                        pltpu.VMEM((SB_QBLOCKS * SB_SLOT_ROWS, LANES), F32),
                        pltpu.VMEM((SB_QBLOCKS * SB_SLOT_ROWS, LANES), F32)],
        compiler_params=_cparams(3),
        name="stick_breaking",
    )(proj, proj, proj, _suffix_matrix())


HYB_PLAN = (("ret_rot", 1.0), ("ret_rot", HEAD_DIM ** -0.5), ("plain", 1.0), ("plain", 1.0),
            ("dil_rot", HEAD_DIM ** -0.5 * LOG2E), ("dil_rot", 1.0), ("plain", 1.0))
SB_PLAN = (("plain", HEAD_DIM ** -0.5),) * 2 + (("plain", 1.0),) * 4


def kernel(x, ffn1_norm, ffn1_w_in, ffn1_w_out, mix_norm, ffn2_norm, ffn2_w_in, ffn2_w_out,
           hyb_w_in, ret_gn, hyb_w_out, sb_w_in, sb_w_out, final_norm):
    batch, seq, d_model = x.shape
    depth = ffn1_norm.shape[0]
    assert d_model == D_MODEL and x.dtype == F32
    assert seq % max(TOK_TILE, FFN_TILE, RET_TOK, DIL_SPAN, SB_QBLOCKS * SB_TQ) == 0
    tables = (_rotary_tables(seq, HEAD_DIM, RET_ROPE_THETA)
              + _rotary_tables(seq, ROPE_DIM, ROPE_THETA))
    (ffn1_w_in, ffn1_w_out, ffn2_w_in, ffn2_w_out, hyb_w_in, hyb_w_out, sb_w_in, sb_w_out) = (
        w.astype(BF16) for w in (ffn1_w_in, ffn1_w_out, ffn2_w_in, ffn2_w_out,
                                 hyb_w_in, hyb_w_out, sb_w_in, sb_w_out))
    h = x.reshape(batch * seq, D_MODEL)
    for layer in range(depth):
        h = _ffn(h, ffn1_norm[layer], ffn1_w_in, ffn1_w_out, layer)
        if layer % 2 == 0:
            e = layer // 2
            proj = _proj(h, mix_norm[layer], hyb_w_in, e, HYB_PLAN, tables, seq)
            mix = ([_retention(proj, ret_gn[e], batch, seq),
                    _dilated_attention(proj, batch, seq)], hyb_w_out, e)
        else:
            o = layer // 2
            proj = _proj(h, mix_norm[layer], sb_w_in, o, SB_PLAN, (), seq)
            mix = ([_stick_breaking(proj, batch, seq)], sb_w_out, o)
        h = _ffn(h, ffn2_norm[layer], ffn2_w_in, ffn2_w_out, layer, mix,
                 final_norm if layer == depth - 1 else None)
    return h.reshape(batch, seq, D_MODEL)
```

```python
import functools

import numpy as np
import jax
import jax.numpy as jnp
from jax import lax
from jax.experimental import pallas as pl
from jax.experimental.pallas import tpu as pltpu

F32 = jnp.float32
BF16 = jnp.bfloat16

D_MODEL = 1024
HEAD_DIM = 64
RET_HEADS = 8
DIL_HEADS = 8
SB_HEADS = 16
RET_WIDTH = RET_HEADS * HEAD_DIM
DIL_WIDTH = DIL_HEADS * HEAD_DIM
SB_WIDTH = SB_HEADS * HEAD_DIM
D_FF = 2816
BLOCK = 128
RET_CHUNK = 256
RET_ROPE_THETA = 10000.0
ROPE_THETA = 500000.0
ROPE_DIM = HEAD_DIM // 4
DILATIONS = (1, 4, 16)
NORM_EPS = 1e-6
GN_EPS = 1e-5

LANES = 128
PAIR = LANES // HEAD_DIM
VMEM_LIMIT = 56 * 1024 * 1024
NEG_BIG = -1e30
LOG2E = 1.4426950408889634

TOK_TILE = 1024
FFN_TILE = 1024
FF_CHUNK = 256
SB_TQ = 256
SB_TK = 256


def _cparams(n_axes):
    return pltpu.CompilerParams(
        dimension_semantics=("arbitrary",) * n_axes,
        vmem_limit_bytes=VMEM_LIMIT)


def _resident(shape, layer=None):
    if layer is None:
        return pl.BlockSpec(shape, lambda *_: (0,) * len(shape), pipeline_mode=pl.Buffered(1))
    return pl.BlockSpec((None,) + tuple(shape), lambda *_: (layer,) + (0,) * len(shape),
                        pipeline_mode=pl.Buffered(1))


def _rms(x, g):
    return x * lax.rsqrt(jnp.mean(x * x, axis=-1, keepdims=True) + NORM_EPS) * g


def _silu(x):
    return x / (1.0 + jnp.exp(-x))


def _lane_is_head0(shape):
    return lax.broadcasted_iota(jnp.int32, shape, len(shape) - 1) < HEAD_DIM


def _stack_heads(q):
    h0 = _lane_is_head0(q.shape)
    zero = jnp.zeros_like(q)
    return jnp.concatenate([jnp.where(h0, q, zero), jnp.where(h0, zero, q)], axis=0)


def _unstack_heads(o2):
    t = o2.shape[0] // 2
    return jnp.where(_lane_is_head0((t, LANES)), o2[:t], o2[t:])


def _ffn_body(*refs, n_slabs, final_norm):
    refs = list(refs)
    x_ref = refs.pop(0)
    slabs = [refs.pop(0) for _ in range(n_slabs)]
    wmix_ref = refs.pop(0) if n_slabs else None
    g_ref, win_ref, wout_ref = refs.pop(0), refs.pop(0), refs.pop(0)
    fg_ref = refs.pop(0) if final_norm else None
    o_ref, act_ref = refs

    x = x_ref[...]
    lo = 0
    for a_ref in slabs:
        width = a_ref.shape[1]
        x = x + jnp.dot(a_ref[...], wmix_ref[lo:lo + width, :], preferred_element_type=F32)
        lo += width
    hb = _rms(x, g_ref[...]).astype(BF16)
    for c in range(D_FF // FF_CHUNK):
        cols = slice(c * FF_CHUNK, (c + 1) * FF_CHUNK)
        up_cols = slice(D_FF + c * FF_CHUNK, D_FF + (c + 1) * FF_CHUNK)
        gate = jnp.dot(hb, win_ref[:, cols], preferred_element_type=F32)
        up = jnp.dot(hb, win_ref[:, up_cols], preferred_element_type=F32)
        act_ref[:, cols] = (_silu(gate) * up).astype(BF16)
    y = x + 0.5 * jnp.dot(act_ref[...], wout_ref[...], preferred_element_type=F32)
    if final_norm:
        y = _rms(y, fg_ref[...])
    o_ref[...] = y


def _ffn(x, g, w_in, w_out, layer, mix=None, final_g=None):
    n = x.shape[0]
    row = pl.BlockSpec((FFN_TILE, D_MODEL), lambda i: (i, 0))
    args, specs = [x], [row]
    slabs = ()
    if mix is not None:
        slabs, w_mix, mix_layer = mix
        args += [*slabs, w_mix]
        specs += [pl.BlockSpec((FFN_TILE, a.shape[1]), lambda i: (i, 0)) for a in slabs]
        specs += [_resident(w_mix.shape[1:], mix_layer)]
    args += [g.reshape(1, D_MODEL), w_in, w_out]
    specs += [_resident((1, D_MODEL)), _resident((D_MODEL, 2 * D_FF), layer),
              _resident((D_FF, D_MODEL), layer)]
    if final_g is not None:
        args.append(final_g.reshape(1, D_MODEL))
        specs.append(_resident((1, D_MODEL)))
    return pl.pallas_call(
        functools.partial(_ffn_body, n_slabs=len(slabs), final_norm=final_g is not None),
        grid=(n // FFN_TILE,),
        in_specs=specs,
        out_specs=row,
        out_shape=jax.ShapeDtypeStruct((n, D_MODEL), F32),
        scratch_shapes=[pltpu.VMEM((FFN_TILE, D_FF), BF16)],
        compiler_params=_cparams(1),
        name="ffn",
    )(*args)


PROJ_GROUP = 512


def _rotate(y, cos, sin_a, sin_b, half):
    return (y * cos + pltpu.roll(y, LANES - half, 1) * sin_a + pltpu.roll(y, half, 1) * sin_b)


def _proj_body(x_ref, g_ref, w_ref, *rest, plan):
    tables, o_ref = rest[:-1], rest[-1]
    hb = _rms(x_ref[...], g_ref[...]).astype(BF16)
    for gi, (kind, scale) in enumerate(plan):
        lo = gi * PROJ_GROUP
        y = jnp.dot(hb, w_ref[:, lo:lo + PROJ_GROUP], preferred_element_type=F32)
        for s in range(PROJ_GROUP // LANES):
            ys = y[:, s * LANES:(s + 1) * LANES]
            if kind == "ret_rot":
                ys = _rotate(ys, tables[0][...], tables[1][...], tables[2][...], HEAD_DIM // 2)
            elif kind == "dil_rot":
                ys = _rotate(ys, tables[3][...], tables[4][...], tables[5][...], ROPE_DIM // 2)
            if scale != 1.0:
                ys = ys * scale
            o_ref[:, lo + s * LANES:lo + (s + 1) * LANES] = ys.astype(BF16)


def _proj(x, g, w, layer, plan, tables, seq):
    n = x.shape[0]
    width = w.shape[2]
    row = pl.BlockSpec((TOK_TILE, D_MODEL), lambda i: (i, 0))
    per_seq = seq // TOK_TILE
    tab = pl.BlockSpec((TOK_TILE, LANES), lambda i: (i % per_seq, 0))
    return pl.pallas_call(
        functools.partial(_proj_body, plan=plan),
        grid=(n // TOK_TILE,),
        in_specs=[row, _resident((1, D_MODEL)), _resident((D_MODEL, width), layer)]
        + [tab] * len(tables),
        out_specs=pl.BlockSpec((TOK_TILE, width), lambda i: (i, 0)),
        out_shape=jax.ShapeDtypeStruct((n, width), BF16),
        compiler_params=_cparams(1),
        name="proj",
    )(x, g.reshape(1, D_MODEL), w, *tables)


def _rotary_tables(seq, rot_dim, theta):
    half = rot_dim // 2
    lane = np.arange(LANES) % HEAD_DIM
    inv_freq = 1.0 / (theta ** (np.arange(half, dtype=np.float64) / half))
    ang = (np.arange(seq, dtype=np.float64)[:, None] * inv_freq[None, :])[:, lane % half]
    cos, sin = np.cos(ang), np.sin(ang)
    is_x1 = (lane < half)[None, :]
    is_x2 = ((lane >= half) & (lane < rot_dim))[None, :]
    cos_t = np.where(is_x1 | is_x2, cos, 1.0)
    sin_a = np.where(is_x1, -sin, 0.0)
    sin_b = np.where(is_x2, sin, 0.0)
    return tuple(jnp.asarray(t, dtype=F32) for t in (cos_t, sin_a, sin_b))


RET_TOK = 1024


def _retention_consts():
    h = np.arange(RET_HEADS, dtype=np.float64)
    log_g = np.log(1.0 - 2.0 ** (-5.0 - h))
    i = np.arange(RET_CHUNK, dtype=np.float64)
    diff = i[:, None] - i[None, :]
    decay_in = np.where(diff >= 0, np.exp(np.maximum(diff, 0.0)[None] * log_g[:, None, None]), 0.0)
    n_pairs = RET_HEADS // PAIR
    decay = decay_in.reshape(n_pairs, PAIR * RET_CHUNK, RET_CHUNK)
    lane_head = np.arange(LANES) // HEAD_DIM
    head_of = (np.arange(n_pairs)[:, None] * PAIR + lane_head[None, :])
    lg = log_g[head_of]
    q_dec = np.exp((i + 1.0)[None, :, None] * lg[:, None, :])
    k_dec = np.exp((RET_CHUNK - 1.0 - i)[None, :, None] * lg[:, None, :])
    same_head = (lane_head[:, None] == lane_head[None, :])
    c_dec = np.exp(RET_CHUNK * lg)[:, :, None] * same_head[None]
    bd = np.broadcast_to(same_head[None], c_dec.shape)
    return tuple(jnp.asarray(t, dtype=F32) for t in (decay, q_dec, k_dec, c_dec, bd))


def _retention_chunk(q, k, v, g, state, dec, q_dec, k_dec, c_dec, bd, gn):
    h0 = _lane_is_head0((RET_CHUNK, LANES))
    inv = 1.0 / HEAD_DIM
    sc = lax.dot_general(_stack_heads(q), k, (((1,), (1,)), ((), ())),
                         preferred_element_type=F32) * dec
    inner = _unstack_heads(jnp.dot(sc.astype(BF16), v, preferred_element_type=F32))
    qd = (q.astype(F32) * q_dec).astype(BF16)
    o = inner + jnp.dot(qd, state.astype(BF16), preferred_element_type=F32)
    kd = (k.astype(F32) * k_dec).astype(BF16)
    kv = lax.dot_general(kd, v, (((0,), (0,)), ((), ())), preferred_element_type=F32)
    state = state * c_dec + kv * bd
    s0 = jnp.sum(jnp.where(h0, o, 0.0), axis=-1, keepdims=True)
    s1 = jnp.sum(jnp.where(h0, 0.0, o), axis=-1, keepdims=True)
    d = o - jnp.where(h0, s0, s1) * inv
    d2 = d * d
    v0 = jnp.sum(jnp.where(h0, d2, 0.0), axis=-1, keepdims=True)
    v1 = jnp.sum(jnp.where(h0, 0.0, d2), axis=-1, keepdims=True)
    y = d * lax.rsqrt(jnp.where(h0, v0, v1) * inv + GN_EPS) * gn
    return (y * _silu(g.astype(F32))).astype(BF16), state


def _retention_body(q_ref, k_ref, v_ref, g_ref, dec_ref, qd_ref, kd_ref, cd_ref, bd_ref, gn_ref,
                    o_ref, state_ref):
    @pl.when(pl.program_id(1) == 0)
    def _():
        state_ref[...] = jnp.zeros_like(state_ref)

    for p in range(RET_HEADS // PAIR):
        cols = slice(p * LANES, (p + 1) * LANES)
        state = state_ref[p]
        for c in range(RET_TOK // RET_CHUNK):
            rows = slice(c * RET_CHUNK, (c + 1) * RET_CHUNK)
            o_ref[rows, cols], state = _retention_chunk(
                q_ref[rows, cols], k_ref[rows, cols], v_ref[rows, cols], g_ref[rows, cols], state,
                dec_ref[p], qd_ref[p], kd_ref[p], cd_ref[p], bd_ref[p], gn_ref[:, cols])
        state_ref[p] = state


def _retention(proj, ret_gn, batch, seq):
    n = proj.shape[0]
    per_seq = seq // RET_TOK
    consts = _retention_consts()

    def slab(j):
        return pl.BlockSpec((RET_TOK, RET_WIDTH), lambda b, i, j=j: (b * per_seq + i, j))

    return pl.pallas_call(
        _retention_body,
        grid=(batch, per_seq),
        in_specs=[slab(0), slab(1), slab(2), slab(3)]
        + [_resident(c.shape) for c in consts] + [_resident((1, RET_WIDTH))],
        out_specs=pl.BlockSpec((RET_TOK, RET_WIDTH), lambda b, i: (b * per_seq + i, 0)),
        out_shape=jax.ShapeDtypeStruct((n, RET_WIDTH), BF16),
        scratch_shapes=[pltpu.VMEM((RET_HEADS // PAIR, LANES, LANES), F32)],
        compiler_params=_cparams(2),
        name="retention",
    )(proj, proj, proj, proj, *consts, ret_gn.reshape(1, RET_WIDTH))


DIL_SPAN = BLOCK * max(DILATIONS)


def _band_bias():
    qi = np.arange(PAIR * BLOCK)[:, None] % BLOCK
    kj = np.arange(2 * BLOCK)[None, :]
    return jnp.asarray(np.where((kj >= qi) & (kj <= qi + BLOCK), 0.0, NEG_BIG), dtype=F32)


def _dilated_body(q_ref, kc_ref, kp_ref, vc_ref, vp_ref, bias_ref, out_ref,
                  qf_ref, kf_ref, vf_ref, num_ref, den_ref, max_ref):
    first = pl.program_id(2) == 0
    qf_ref[...] = q_ref[...].astype(F32)
    kf_ref[0:DIL_SPAN, :] = kp_ref[...].astype(F32)
    kf_ref[DIL_SPAN:, :] = kc_ref[...].astype(F32)
    vf_ref[0:DIL_SPAN, :] = vp_ref[...].astype(F32)
    vf_ref[DIL_SPAN:, :] = vc_ref[...].astype(F32)
    k_min = jnp.where(first, BLOCK, 0)
    kj = lax.broadcasted_iota(jnp.int32, (PAIR * BLOCK, 2 * BLOCK), 1)
    ones = jnp.ones((2 * BLOCK, LANES), BF16)
    head0 = _lane_is_head0((BLOCK, LANES))

    for pi, dil in enumerate(DILATIONS):
        for r in range(dil):
            for n in range(DIL_SPAN // (BLOCK * dil)):
                q0 = n * (BLOCK * dil) + r
                k0 = q0 + DIL_SPAN - BLOCK * dil
                q = qf_ref[pl.ds(q0, BLOCK, stride=dil), :].astype(BF16)
                keys = kf_ref[pl.ds(k0, 2 * BLOCK, stride=dil), :].astype(BF16)
                vals = vf_ref[pl.ds(k0, 2 * BLOCK, stride=dil), :].astype(BF16)
                s = lax.dot_general(_stack_heads(q), keys, (((1,), (1,)), ((), ())),
                                    preferred_element_type=F32) + bias_ref[...]
                if n == 0:
                    s = jnp.where(kj >= k_min, s, NEG_BIG)
                m = jnp.max(s, axis=-1, keepdims=True)
                p = jnp.exp2(s - m)
                pv = jnp.dot(p.astype(BF16), jnp.concatenate([vals, ones], axis=1),
                             preferred_element_type=F32)
                rows = pl.ds(q0, BLOCK, stride=dil)
                num_ref[pi, rows, :] = _unstack_heads(pv[:, :LANES])
                den_ref[pi, rows, :] = _unstack_heads(pv[:, LANES:])
                max_ref[pi, rows, :] = jnp.where(head0, m[:BLOCK], m[BLOCK:])

    patterns = range(len(DILATIONS))
    m = functools.reduce(jnp.maximum, [max_ref[pi] for pi in patterns])
    e = [jnp.exp2(max_ref[pi] - m) for pi in patterns]
    num = functools.reduce(lambda a, b: a + b, [e[pi] * num_ref[pi] for pi in patterns])
    den = functools.reduce(lambda a, b: a + b, [e[pi] * den_ref[pi] for pi in patterns])
    out_ref[...] = (num / den).astype(BF16)


def _dilated_attention(proj, batch, seq):
    n = proj.shape[0]
    q0, k0, v0 = ((4 * RET_WIDTH + t * DIL_WIDTH) // LANES for t in range(3))
    per_seq = seq // DIL_SPAN
    n_pairs = DIL_HEADS // PAIR

    def cur(c0):
        return pl.BlockSpec((DIL_SPAN, LANES), lambda b, p, i: (b * per_seq + i, c0 + p))

    def prev(c0):
        return pl.BlockSpec((DIL_SPAN, LANES),
                            lambda b, p, i: (jnp.maximum(b * per_seq + i - 1, 0), c0 + p))

    return pl.pallas_call(
        _dilated_body,
        grid=(batch, n_pairs, per_seq),
        in_specs=[cur(q0), cur(k0), prev(k0), cur(v0), prev(v0),
                  _resident((PAIR * BLOCK, 2 * BLOCK))],
        out_specs=pl.BlockSpec((DIL_SPAN, LANES), lambda b, p, i: (b * per_seq + i, p)),
        out_shape=jax.ShapeDtypeStruct((n, DIL_WIDTH), BF16),
        scratch_shapes=[pltpu.VMEM((DIL_SPAN, LANES), F32),
                        pltpu.VMEM((2 * DIL_SPAN, LANES), F32),
                        pltpu.VMEM((2 * DIL_SPAN, LANES), F32),
                        pltpu.VMEM((len(DILATIONS), DIL_SPAN, LANES), F32),
                        pltpu.VMEM((len(DILATIONS), DIL_SPAN, LANES), F32),
                        pltpu.VMEM((len(DILATIONS), DIL_SPAN, LANES), F32)],
        compiler_params=_cparams(3),
        name="dilated",
    )(proj, proj, proj, proj, proj, _band_bias())


SB_ROWS = 512
SB_PAIRS = 4
SB_QBLOCKS = 2
SB_SLOT_ROWS = SB_PAIRS * PAIR * SB_TQ
SB_DEAD = -105.0


def _suffix_matrix():
    return jnp.asarray(np.arange(SB_TK)[:, None] > np.arange(SB_TK)[None, :], dtype=BF16)


def _sb_visit(qs_ref, k_ref, v_ref, tri_ref, acc_ref, carry_ref, slot, first_kb, n_blk, diagonal):
    start = pl.multiple_of(first_kb * SB_TK, SB_TK)
    width = n_blk * SB_TK
    tri = tri_ref[...]
    for rc in range(SB_SLOT_ROWS // SB_ROWS):
        rows = slice(slot * SB_SLOT_ROWS + rc * SB_ROWS, slot * SB_SLOT_ROWS + (rc + 1) * SB_ROWS)
        pair = (rc * SB_ROWS) // (PAIR * SB_TQ)
        kwin = k_ref[pl.ds(start, width), pair * LANES:(pair + 1) * LANES]
        vwin = v_ref[pl.ds(start, width), pair * LANES:(pair + 1) * LANES]
        z = lax.dot_general(qs_ref[rows, :], kwin, (((1,), (1,)), ((), ())),
                            preferred_element_type=F32)
        if diagonal:
            own = (SB_ROWS, SB_TK)
            q_row = (rc * SB_ROWS + lax.broadcasted_iota(jnp.int32, own, 0)) & (SB_TQ - 1)
            causal = lax.broadcasted_iota(jnp.int32, own, 1) < q_row
            masked = jnp.where(causal, z[:, width - SB_TK:], NEG_BIG)
            z = masked if width == SB_TK else jnp.concatenate([z[:, :width - SB_TK], masked], axis=1)
        log_go = jnp.minimum(z, 0.0) - jnp.log(1.0 + jnp.exp2(jnp.abs(z) * -LOG2E))
        log_stay = log_go - z
        blocks = [slice(b * SB_TK, (b + 1) * SB_TK) for b in range(n_blk)]
        stay16 = log_stay.astype(BF16)
        within = jnp.dot(jnp.concatenate([stay16[:, c] for c in blocks], axis=0), tri,
                         preferred_element_type=F32)
        after = None if diagonal else carry_ref[rows, :]
        later = [None] * n_blk
        for b in reversed(range(n_blk)):
            later[b] = within[b * SB_ROWS:(b + 1) * SB_ROWS]
            if after is not None:
                later[b] = later[b] + jnp.concatenate([after] * (SB_TK // LANES), axis=1)
            total = jnp.broadcast_to(jnp.sum(log_stay[:, blocks[b]], axis=-1, keepdims=True),
                                     (SB_ROWS, LANES))
            after = total if after is None else after + total
        a = jnp.exp(log_go + jnp.concatenate(later, axis=1))
        pv = jnp.dot(a.astype(BF16), vwin, preferred_element_type=F32)
        if diagonal:
            acc_ref[rows, :] = pv
        else:
            acc_ref[rows, :] += pv
        carry_ref[rows, :] = after


def _sb_body(q_ref, k_ref, v_ref, tri_ref, o_ref, qs_ref, acc_ref, carry_ref):
    step_i = pl.program_id(2)
    stacked = PAIR * SB_TQ
    units = [(slot, slice(slot * SB_TQ, (slot + 1) * SB_TQ), slice(pair * LANES, (pair + 1) * LANES),
              slice(slot * SB_SLOT_ROWS + pair * stacked, slot * SB_SLOT_ROWS + (pair + 1) * stacked))
             for slot in range(SB_QBLOCKS) for pair in range(SB_PAIRS)]
    for _, q_rows, lanes, rows in units:
        qs_ref[rows, :] = _stack_heads(q_ref[q_rows, lanes])
    visit = functools.partial(_sb_visit, qs_ref, k_ref, v_ref, tri_ref, acc_ref, carry_ref)
    q_blk = [step_i * SB_QBLOCKS + slot for slot in range(SB_QBLOCKS)]

    @pl.when(step_i == 0)
    def _():
        visit(0, 0, 1, True)
        for slot in range(1, SB_QBLOCKS):
            visit(slot, slot - 1, 2, True)

    @pl.when(step_i > 0)
    def _():
        for slot in range(SB_QBLOCKS):
            visit(slot, q_blk[slot] - 1, 2, True)

    for slot in range(SB_QBLOCKS):
        slot_rows = slice(slot * SB_SLOT_ROWS, (slot + 1) * SB_SLOT_ROWS)

        def alive(state, slot=slot):
            t, top = state
            return (t <= q_blk[slot]) & (top > SB_DEAD)

        def step(state, slot=slot, slot_rows=slot_rows):
            t, _ = state
            visit(slot, q_blk[slot] - t, 1, False)
            return t + 1, jnp.max(carry_ref[slot_rows, :])

        lax.while_loop(alive, step, (2, jnp.max(carry_ref[slot_rows, :])))
    for _, q_rows, lanes, rows in units:
        o_ref[q_rows, lanes] = _unstack_heads(acc_ref[rows, :]).astype(BF16)


def _stick_breaking(proj, batch, seq):
    n = proj.shape[0]
    groups = SB_HEADS // PAIR // SB_PAIRS
    per_seq = seq // (SB_QBLOCKS * SB_TQ)
    wide = SB_PAIRS * LANES
    qo = lambda b, p, i: (b * per_seq + i, p)
    return pl.pallas_call(
        _sb_body,
        grid=(batch, groups, per_seq),
        in_specs=[pl.BlockSpec((SB_QBLOCKS * SB_TQ, wide), qo),
                  pl.BlockSpec((seq, wide), lambda b, p, i: (b, groups + p)),
                  pl.BlockSpec((seq, wide), lambda b, p, i: (b, 2 * groups + p)),
                  _resident((SB_TK, SB_TK))],
        out_specs=pl.BlockSpec((SB_QBLOCKS * SB_TQ, wide), qo),
        out_shape=jax.ShapeDtypeStruct((n, SB_WIDTH), BF16),
        scratch_shapes=[pltpu.VMEM((SB_QBLOCKS * SB_SLOT_ROWS, LANES), BF16),
                        pltpu.VMEM((SB_QBLOCKS * SB_SLOT_ROWS, LANES), F32),
                        pltpu.VMEM((SB_QBLOCKS * SB_SLOT_ROWS, LANES), F32)],
        compiler_params=_cparams(3),
        name="stick_breaking",
    )(proj, proj, proj, _suffix_matrix())


HYB_PLAN = (("ret_rot", 1.0), ("ret_rot", HEAD_DIM ** -0.5), ("plain", 1.0), ("plain", 1.0),
            ("dil_rot", HEAD_DIM ** -0.5 * LOG2E), ("dil_rot", 1.0), ("plain", 1.0))
SB_PLAN = (("plain", HEAD_DIM ** -0.5),) * 2 + (("plain", 1.0),) * 4


def kernel(x, ffn1_norm, ffn1_w_in, ffn1_w_out, mix_norm, ffn2_norm, ffn2_w_in, ffn2_w_out,
           hyb_w_in, ret_gn, hyb_w_out, sb_w_in, sb_w_out, final_norm):
    batch, seq, d_model = x.shape
    depth = ffn1_norm.shape[0]
    assert d_model == D_MODEL and x.dtype == F32
    assert seq % max(TOK_TILE, FFN_TILE, RET_TOK, DIL_SPAN, SB_QBLOCKS * SB_TQ) == 0
    tables = (_rotary_tables(seq, HEAD_DIM, RET_ROPE_THETA)
              + _rotary_tables(seq, ROPE_DIM, ROPE_THETA))
    (ffn1_w_in, ffn1_w_out, ffn2_w_in, ffn2_w_out, hyb_w_in, hyb_w_out, sb_w_in, sb_w_out) = (
        w.astype(BF16) for w in (ffn1_w_in, ffn1_w_out, ffn2_w_in, ffn2_w_out,
                                 hyb_w_in, hyb_w_out, sb_w_in, sb_w_out))
    h = x.reshape(batch * seq, D_MODEL)
    for layer in range(depth):
        h = _ffn(h, ffn1_norm[layer], ffn1_w_in, ffn1_w_out, layer)
        if layer % 2 == 0:
            e = layer // 2
            proj = _proj(h, mix_norm[layer], hyb_w_in, e, HYB_PLAN, tables, seq)
            mix = ([_retention(proj, ret_gn[e], batch, seq),
                    _dilated_attention(proj, batch, seq)], hyb_w_out, e)
        else:
            o = layer // 2
            proj = _proj(h, mix_norm[layer], sb_w_in, o, SB_PLAN, (), seq)
            mix = ([_stick_breaking(proj, batch, seq)], sb_w_out, o)
        h = _ffn(h, ffn2_norm[layer], ffn2_w_in, ffn2_w_out, layer, mix,
                 final_norm if layer == depth - 1 else None)
    return h.reshape(batch, seq, D_MODEL)
```

```python
import functools

import numpy as np
import jax
import jax.numpy as jnp
from jax import lax
from jax.experimental import pallas as pl
from jax.experimental.pallas import tpu as pltpu

F32 = jnp.float32
BF16 = jnp.bfloat16

D_MODEL = 1024
HEAD_DIM = 64
RET_HEADS = 8
DIL_HEADS = 8
SB_HEADS = 16
RET_WIDTH = RET_HEADS * HEAD_DIM
DIL_WIDTH = DIL_HEADS * HEAD_DIM
SB_WIDTH = SB_HEADS * HEAD_DIM
D_FF = 2816
BLOCK = 128
RET_CHUNK = 256
RET_ROPE_THETA = 10000.0
ROPE_THETA = 500000.0
ROPE_DIM = HEAD_DIM // 4
DILATIONS = (1, 4, 16)
NORM_EPS = 1e-6
GN_EPS = 1e-5

LANES = 128
PAIR = LANES // HEAD_DIM
VMEM_LIMIT = 56 * 1024 * 1024
NEG_BIG = -1e30
LOG2E = 1.4426950408889634

TOK_TILE = 1024
FFN_TILE = 1024
FF_CHUNK = 256
SB_TQ = 256
SB_TK = 256


def _cparams(n_axes):
    return pltpu.CompilerParams(
        dimension_semantics=("arbitrary",) * n_axes,
        vmem_limit_bytes=VMEM_LIMIT)


def _resident(shape, layer=None):
    if layer is None:
        return pl.BlockSpec(shape, lambda *_: (0,) * len(shape), pipeline_mode=pl.Buffered(1))
    return pl.BlockSpec((None,) + tuple(shape), lambda *_: (layer,) + (0,) * len(shape),
                        pipeline_mode=pl.Buffered(1))


def _rms(x, g):
    return x * lax.rsqrt(jnp.mean(x * x, axis=-1, keepdims=True) + NORM_EPS) * g


def _silu(x):
    return x / (1.0 + jnp.exp(-x))


def _lane_is_head0(shape):
    return lax.broadcasted_iota(jnp.int32, shape, len(shape) - 1) < HEAD_DIM


def _stack_heads(q):
    h0 = _lane_is_head0(q.shape)
    zero = jnp.zeros_like(q)
    return jnp.concatenate([jnp.where(h0, q, zero), jnp.where(h0, zero, q)], axis=0)


def _unstack_heads(o2):
    t = o2.shape[0] // 2
    return jnp.where(_lane_is_head0((t, LANES)), o2[:t], o2[t:])


def _ffn_body(*refs, n_slabs, final_norm):
    refs = list(refs)
    x_ref = refs.pop(0)
    slabs = [refs.pop(0) for _ in range(n_slabs)]
    wmix_ref = refs.pop(0) if n_slabs else None
    g_ref, win_ref, wout_ref = refs.pop(0), refs.pop(0), refs.pop(0)
    fg_ref = refs.pop(0) if final_norm else None
    o_ref, act_ref = refs

    x = x_ref[...]
    lo = 0
    for a_ref in slabs:
        width = a_ref.shape[1]
        x = x + jnp.dot(a_ref[...], wmix_ref[lo:lo + width, :], preferred_element_type=F32)
        lo += width
    hb = _rms(x, g_ref[...]).astype(BF16)
    for c in range(D_FF // FF_CHUNK):
        cols = slice(c * FF_CHUNK, (c + 1) * FF_CHUNK)
        up_cols = slice(D_FF + c * FF_CHUNK, D_FF + (c + 1) * FF_CHUNK)
        gate = jnp.dot(hb, win_ref[:, cols], preferred_element_type=F32)
        up = jnp.dot(hb, win_ref[:, up_cols], preferred_element_type=F32)
        act_ref[:, cols] = (_silu(gate) * up).astype(BF16)
    y = x + 0.5 * jnp.dot(act_ref[...], wout_ref[...], preferred_element_type=F32)
    if final_norm:
        y = _rms(y, fg_ref[...])
    o_ref[...] = y


def _ffn(x, g, w_in, w_out, layer, mix=None, final_g=None):
    n = x.shape[0]
    row = pl.BlockSpec((FFN_TILE, D_MODEL), lambda i: (i, 0))
    args, specs = [x], [row]
    slabs = ()
    if mix is not None:
        slabs, w_mix, mix_layer = mix
        args += [*slabs, w_mix]
        specs += [pl.BlockSpec((FFN_TILE, a.shape[1]), lambda i: (i, 0)) for a in slabs]
        specs += [_resident(w_mix.shape[1:], mix_layer)]
    args += [g.reshape(1, D_MODEL), w_in, w_out]
    specs += [_resident((1, D_MODEL)), _resident((D_MODEL, 2 * D_FF), layer),
              _resident((D_FF, D_MODEL), layer)]
    if final_g is not None:
        args.append(final_g.reshape(1, D_MODEL))
        specs.append(_resident((1, D_MODEL)))
    return pl.pallas_call(
        functools.partial(_ffn_body, n_slabs=len(slabs), final_norm=final_g is not None),
        grid=(n // FFN_TILE,),
        in_specs=specs,
        out_specs=row,
        out_shape=jax.ShapeDtypeStruct((n, D_MODEL), F32),
        scratch_shapes=[pltpu.VMEM((FFN_TILE, D_FF), BF16)],
        compiler_params=_cparams(1),
        name="ffn",
    )(*args)


PROJ_GROUP = 512


def _rotate(y, cos, sin_a, sin_b, half):
    return (y * cos + pltpu.roll(y, LANES - half, 1) * sin_a + pltpu.roll(y, half, 1) * sin_b)


def _proj_body(x_ref, g_ref, w_ref, *rest, plan):
    tables, o_ref = rest[:-1], rest[-1]
    hb = _rms(x_ref[...], g_ref[...]).astype(BF16)
    for gi, (kind, scale) in enumerate(plan):
        lo = gi * PROJ_GROUP
        y = jnp.dot(hb, w_ref[:, lo:lo + PROJ_GROUP], preferred_element_type=F32)
        for s in range(PROJ_GROUP // LANES):
            ys = y[:, s * LANES:(s + 1) * LANES]
            if kind == "ret_rot":
                ys = _rotate(ys, tables[0][...], tables[1][...], tables[2][...], HEAD_DIM // 2)
            elif kind == "dil_rot":
                ys = _rotate(ys, tables[3][...], tables[4][...], tables[5][...], ROPE_DIM // 2)
            if scale != 1.0:
                ys = ys * scale
            o_ref[:, lo + s * LANES:lo + (s + 1) * LANES] = ys.astype(BF16)


def _proj(x, g, w, layer, plan, tables, seq):
    n = x.shape[0]
    width = w.shape[2]
    row = pl.BlockSpec((TOK_TILE, D_MODEL), lambda i: (i, 0))
    per_seq = seq // TOK_TILE
    tab = pl.BlockSpec((TOK_TILE, LANES), lambda i: (i % per_seq, 0))
    return pl.pallas_call(
        functools.partial(_proj_body, plan=plan),
        grid=(n // TOK_TILE,),
        in_specs=[row, _resident((1, D_MODEL)), _resident((D_MODEL, width), layer)]
        + [tab] * len(tables),
        out_specs=pl.BlockSpec((TOK_TILE, width), lambda i: (i, 0)),
        out_shape=jax.ShapeDtypeStruct((n, width), BF16),
        compiler_params=_cparams(1),
        name="proj",
    )(x, g.reshape(1, D_MODEL), w, *tables)


def _rotary_tables(seq, rot_dim, theta):
    half = rot_dim // 2
    lane = np.arange(LANES) % HEAD_DIM
    inv_freq = 1.0 / (theta ** (np.arange(half, dtype=np.float64) / half))
    ang = (np.arange(seq, dtype=np.float64)[:, None] * inv_freq[None, :])[:, lane % half]
    cos, sin = np.cos(ang), np.sin(ang)
    is_x1 = (lane < half)[None, :]
    is_x2 = ((lane >= half) & (lane < rot_dim))[None, :]
    cos_t = np.where(is_x1 | is_x2, cos, 1.0)
    sin_a = np.where(is_x1, -sin, 0.0)
    sin_b = np.where(is_x2, sin, 0.0)
    return tuple(jnp.asarray(t, dtype=F32) for t in (cos_t, sin_a, sin_b))


RET_TOK = 1024


def _retention_consts():
    h = np.arange(RET_HEADS, dtype=np.float64)
    log_g = np.log(1.0 - 2.0 ** (-5.0 - h))
    i = np.arange(RET_CHUNK, dtype=np.float64)
    diff = i[:, None] - i[None, :]
    decay_in = np.where(diff >= 0, np.exp(np.maximum(diff, 0.0)[None] * log_g[:, None, None]), 0.0)
    n_pairs = RET_HEADS // PAIR
    decay = decay_in.reshape(n_pairs, PAIR * RET_CHUNK, RET_CHUNK)
    lane_head = np.arange(LANES) // HEAD_DIM
    head_of = (np.arange(n_pairs)[:, None] * PAIR + lane_head[None, :])
    lg = log_g[head_of]
    q_dec = np.exp((i + 1.0)[None, :, None] * lg[:, None, :])
    k_dec = np.exp((RET_CHUNK - 1.0 - i)[None, :, None] * lg[:, None, :])
    same_head = (lane_head[:, None] == lane_head[None, :])
    c_dec = np.exp(RET_CHUNK * lg)[:, :, None] * same_head[None]
    bd = np.broadcast_to(same_head[None], c_dec.shape)
    return tuple(jnp.asarray(t, dtype=F32) for t in (decay, q_dec, k_dec, c_dec, bd))


def _retention_chunk(q, k, v, g, state, dec, q_dec, k_dec, c_dec, bd, gn):
    h0 = _lane_is_head0((RET_CHUNK, LANES))
    inv = 1.0 / HEAD_DIM
    sc = lax.dot_general(_stack_heads(q), k, (((1,), (1,)), ((), ())),
                         preferred_element_type=F32) * dec
    inner = _unstack_heads(jnp.dot(sc.astype(BF16), v, preferred_element_type=F32))
    qd = (q.astype(F32) * q_dec).astype(BF16)
    o = inner + jnp.dot(qd, state.astype(BF16), preferred_element_type=F32)
    kd = (k.astype(F32) * k_dec).astype(BF16)
    kv = lax.dot_general(kd, v, (((0,), (0,)), ((), ())), preferred_element_type=F32)
    state = state * c_dec + kv * bd
    s0 = jnp.sum(jnp.where(h0, o, 0.0), axis=-1, keepdims=True)
    s1 = jnp.sum(jnp.where(h0, 0.0, o), axis=-1, keepdims=True)
    d = o - jnp.where(h0, s0, s1) * inv
    d2 = d * d
    v0 = jnp.sum(jnp.where(h0, d2, 0.0), axis=-1, keepdims=True)
    v1 = jnp.sum(jnp.where(h0, 0.0, d2), axis=-1, keepdims=True)
    y = d * lax.rsqrt(jnp.where(h0, v0, v1) * inv + GN_EPS) * gn
    return (y * _silu(g.astype(F32))).astype(BF16), state


def _retention_body(q_ref, k_ref, v_ref, g_ref, dec_ref, qd_ref, kd_ref, cd_ref, bd_ref, gn_ref,
                    o_ref, state_ref):
    @pl.when(pl.program_id(1) == 0)
    def _():
        state_ref[...] = jnp.zeros_like(state_ref)

    for p in range(RET_HEADS // PAIR):
        cols = slice(p * LANES, (p + 1) * LANES)
        state = state_ref[p]
        for c in range(RET_TOK // RET_CHUNK):
            rows = slice(c * RET_CHUNK, (c + 1) * RET_CHUNK)
            o_ref[rows, cols], state = _retention_chunk(
                q_ref[rows, cols], k_ref[rows, cols], v_ref[rows, cols], g_ref[rows, cols], state,
                dec_ref[p], qd_ref[p], kd_ref[p], cd_ref[p], bd_ref[p], gn_ref[:, cols])
        state_ref[p] = state


def _retention(proj, ret_gn, batch, seq):
    n = proj.shape[0]
    per_seq = seq // RET_TOK
    consts = _retention_consts()

    def slab(j):
        return pl.BlockSpec((RET_TOK, RET_WIDTH), lambda b, i, j=j: (b * per_seq + i, j))

    return pl.pallas_call(
        _retention_body,
        grid=(batch, per_seq),
        in_specs=[slab(0), slab(1), slab(2), slab(3)]
        + [_resident(c.shape) for c in consts] + [_resident((1, RET_WIDTH))],
        out_specs=pl.BlockSpec((RET_TOK, RET_WIDTH), lambda b, i: (b * per_seq + i, 0)),
        out_shape=jax.ShapeDtypeStruct((n, RET_WIDTH), BF16),
        scratch_shapes=[pltpu.VMEM((RET_HEADS // PAIR, LANES, LANES), F32)],
        compiler_params=_cparams(2),
        name="retention",
    )(proj, proj, proj, proj, *consts, ret_gn.reshape(1, RET_WIDTH))


DIL_SPAN = BLOCK * max(DILATIONS)


def _band_bias():
    qi = np.arange(PAIR * BLOCK)[:, None] % BLOCK
    kj = np.arange(2 * BLOCK)[None, :]
    return jnp.asarray(np.where((kj >= qi) & (kj <= qi + BLOCK), 0.0, NEG_BIG), dtype=F32)


def _dilated_body(q_ref, kc_ref, kp_ref, vc_ref, vp_ref, bias_ref, out_ref,
                  qf_ref, kf_ref, vf_ref, num_ref, den_ref, max_ref):
    first = pl.program_id(2) == 0
    qf_ref[...] = q_ref[...].astype(F32)
    kf_ref[0:DIL_SPAN, :] = kp_ref[...].astype(F32)
    kf_ref[DIL_SPAN:, :] = kc_ref[...].astype(F32)
    vf_ref[0:DIL_SPAN, :] = vp_ref[...].astype(F32)
    vf_ref[DIL_SPAN:, :] = vc_ref[...].astype(F32)
    k_min = jnp.where(first, BLOCK, 0)
    kj = lax.broadcasted_iota(jnp.int32, (PAIR * BLOCK, 2 * BLOCK), 1)
    ones = jnp.ones((2 * BLOCK, LANES), BF16)
    head0 = _lane_is_head0((BLOCK, LANES))

    for pi, dil in enumerate(DILATIONS):
        for r in range(dil):
            for n in range(DIL_SPAN // (BLOCK * dil)):
                q0 = n * (BLOCK * dil) + r
                k0 = q0 + DIL_SPAN - BLOCK * dil
                q = qf_ref[pl.ds(q0, BLOCK, stride=dil), :].astype(BF16)
                keys = kf_ref[pl.ds(k0, 2 * BLOCK, stride=dil), :].astype(BF16)
                vals = vf_ref[pl.ds(k0, 2 * BLOCK, stride=dil), :].astype(BF16)
                s = lax.dot_general(_stack_heads(q), keys, (((1,), (1,)), ((), ())),
                                    preferred_element_type=F32) + bias_ref[...]
                if n == 0:
                    s = jnp.where(kj >= k_min, s, NEG_BIG)
                m = jnp.max(s, axis=-1, keepdims=True)
                p = jnp.exp2(s - m)
                pv = jnp.dot(p.astype(BF16), jnp.concatenate([vals, ones], axis=1),
                             preferred_element_type=F32)
                rows = pl.ds(q0, BLOCK, stride=dil)
                num_ref[pi, rows, :] = _unstack_heads(pv[:, :LANES])
                den_ref[pi, rows, :] = _unstack_heads(pv[:, LANES:])
                max_ref[pi, rows, :] = jnp.where(head0, m[:BLOCK], m[BLOCK:])

    patterns = range(len(DILATIONS))
    m = functools.reduce(jnp.maximum, [max_ref[pi] for pi in patterns])
    e = [jnp.exp2(max_ref[pi] - m) for pi in patterns]
    num = functools.reduce(lambda a, b: a + b, [e[pi] * num_ref[pi] for pi in patterns])
    den = functools.reduce(lambda a, b: a + b, [e[pi] * den_ref[pi] for pi in patterns])
    out_ref[...] = (num / den).astype(BF16)


def _dilated_attention(proj, batch, seq):
    n = proj.shape[0]
    q0, k0, v0 = ((4 * RET_WIDTH + t * DIL_WIDTH) // LANES for t in range(3))
    per_seq = seq // DIL_SPAN
    n_pairs = DIL_HEADS // PAIR

    def cur(c0):
        return pl.BlockSpec((DIL_SPAN, LANES), lambda b, p, i: (b * per_seq + i, c0 + p))

    def prev(c0):
        return pl.BlockSpec((DIL_SPAN, LANES),
                            lambda b, p, i: (jnp.maximum(b * per_seq + i - 1, 0), c0 + p))

    return pl.pallas_call(
        _dilated_body,
        grid=(batch, n_pairs, per_seq),
        in_specs=[cur(q0), cur(k0), prev(k0), cur(v0), prev(v0),
                  _resident((PAIR * BLOCK, 2 * BLOCK))],
        out_specs=pl.BlockSpec((DIL_SPAN, LANES), lambda b, p, i: (b * per_seq + i, p)),
        out_shape=jax.ShapeDtypeStruct((n, DIL_WIDTH), BF16),
        scratch_shapes=[pltpu.VMEM((DIL_SPAN, LANES), F32),
                        pltpu.VMEM((2 * DIL_SPAN, LANES), F32),
                        pltpu.VMEM((2 * DIL_SPAN, LANES), F32),
                        pltpu.VMEM((len(DILATIONS), DIL_SPAN, LANES), F32),
                        pltpu.VMEM((len(DILATIONS), DIL_SPAN, LANES), F32),
                        pltpu.VMEM((len(DILATIONS), DIL_SPAN, LANES), F32)],
        compiler_params=_cparams(3),
        name="dilated",
    )(proj, proj, proj, proj, proj, _band_bias())


SB_ROWS = 512
SB_PAIRS = 4
SB_QBLOCKS = 2
SB_SLOT_ROWS = SB_PAIRS * PAIR * SB_TQ
SB_DEAD = float("-inf")


def _suffix_matrix():
    return jnp.asarray(np.arange(SB_TK)[:, None] > np.arange(SB_TK)[None, :], dtype=BF16)


def _sb_visit(qs_ref, k_ref, v_ref, tri_ref, acc_ref, carry_ref, slot, first_kb, n_blk, diagonal):
    start = pl.multiple_of(first_kb * SB_TK, SB_TK)
    width = n_blk * SB_TK
    tri = tri_ref[...]
    for rc in range(SB_SLOT_ROWS // SB_ROWS):
        rows = slice(slot * SB_SLOT_ROWS + rc * SB_ROWS, slot * SB_SLOT_ROWS + (rc + 1) * SB_ROWS)
        pair = (rc * SB_ROWS) // (PAIR * SB_TQ)
        kwin = k_ref[pl.ds(start, width), pair * LANES:(pair + 1) * LANES]
        vwin = v_ref[pl.ds(start, width), pair * LANES:(pair + 1) * LANES]
        z = lax.dot_general(qs_ref[rows, :], kwin, (((1,), (1,)), ((), ())),
                            preferred_element_type=F32)
        if diagonal:
            own = (SB_ROWS, SB_TK)
            q_row = (rc * SB_ROWS + lax.broadcasted_iota(jnp.int32, own, 0)) & (SB_TQ - 1)
            causal = lax.broadcasted_iota(jnp.int32, own, 1) < q_row
            masked = jnp.where(causal, z[:, width - SB_TK:], NEG_BIG)
            z = masked if width == SB_TK else jnp.concatenate([z[:, :width - SB_TK], masked], axis=1)
        log_go = jnp.minimum(z, 0.0) - jnp.log(1.0 + jnp.exp2(jnp.abs(z) * -LOG2E))
        log_stay = log_go - z
        blocks = [slice(b * SB_TK, (b + 1) * SB_TK) for b in range(n_blk)]
        stay16 = log_stay.astype(BF16)
        within = jnp.dot(jnp.concatenate([stay16[:, c] for c in blocks], axis=0), tri,
                         preferred_element_type=F32)
        after = None if diagonal else carry_ref[rows, :]
        later = [None] * n_blk
        for b in reversed(range(n_blk)):
            later[b] = within[b * SB_ROWS:(b + 1) * SB_ROWS]
            if after is not None:
                later[b] = later[b] + jnp.concatenate([after] * (SB_TK // LANES), axis=1)
            total = jnp.broadcast_to(jnp.sum(log_stay[:, blocks[b]], axis=-1, keepdims=True),
                                     (SB_ROWS, LANES))
            after = total if after is None else after + total
        a = jnp.exp(log_go + jnp.concatenate(later, axis=1))
        pv = jnp.dot(a.astype(BF16), vwin, preferred_element_type=F32)
        if diagonal:
            acc_ref[rows, :] = pv
        else:
            acc_ref[rows, :] += pv
        carry_ref[rows, :] = after


def _sb_body(q_ref, k_ref, v_ref, tri_ref, o_ref, qs_ref, acc_ref, carry_ref):
    step_i = pl.program_id(2)
    stacked = PAIR * SB_TQ
    units = [(slot, slice(slot * SB_TQ, (slot + 1) * SB_TQ), slice(pair * LANES, (pair + 1) * LANES),
              slice(slot * SB_SLOT_ROWS + pair * stacked, slot * SB_SLOT_ROWS + (pair + 1) * stacked))
             for slot in range(SB_QBLOCKS) for pair in range(SB_PAIRS)]
    for _, q_rows, lanes, rows in units:
        qs_ref[rows, :] = _stack_heads(q_ref[q_rows, lanes])
    visit = functools.partial(_sb_visit, qs_ref, k_ref, v_ref, tri_ref, acc_ref, carry_ref)
    q_blk = [step_i * SB_QBLOCKS + slot for slot in range(SB_QBLOCKS)]

    @pl.when(step_i == 0)
    def _():
        visit(0, 0, 1, True)
        for slot in range(1, SB_QBLOCKS):
            visit(slot, slot - 1, 2, True)

    @pl.when(step_i > 0)
    def _():
        for slot in range(SB_QBLOCKS):
            visit(slot, q_blk[slot] - 1, 2, True)

    for slot in range(SB_QBLOCKS):
        slot_rows = slice(slot * SB_SLOT_ROWS, (slot + 1) * SB_SLOT_ROWS)

        def alive(state, slot=slot):
            t, top = state
            return (t <= q_blk[slot]) & (top > SB_DEAD)

        def step(state, slot=slot, slot_rows=slot_rows):
            t, _ = state
            visit(slot, q_blk[slot] - t, 1, False)
            return t + 1, jnp.max(carry_ref[slot_rows, :])

        lax.while_loop(alive, step, (2, jnp.max(carry_ref[slot_rows, :])))
    for _, q_rows, lanes, rows in units:
        o_ref[q_rows, lanes] = _unstack_heads(acc_ref[rows, :]).astype(BF16)


def _stick_breaking(proj, batch, seq):
    n = proj.shape[0]
    groups = SB_HEADS // PAIR // SB_PAIRS
    per_seq = seq // (SB_QBLOCKS * SB_TQ)
    wide = SB_PAIRS * LANES
    qo = lambda b, p, i: (b * per_seq + i, p)
    return pl.pallas_call(
        _sb_body,
        grid=(batch, groups, per_seq),
        in_specs=[pl.BlockSpec((SB_QBLOCKS * SB_TQ, wide), qo),
                  pl.BlockSpec((seq, wide), lambda b, p, i: (b, groups + p)),
                  pl.BlockSpec((seq, wide), lambda b, p, i: (b, 2 * groups + p)),
                  _resident((SB_TK, SB_TK))],
        out_specs=pl.BlockSpec((SB_QBLOCKS * SB_TQ, wide), qo),
        out_shape=jax.ShapeDtypeStruct((n, SB_WIDTH), BF16),
        scratch_shapes=[pltpu.VMEM((SB_QBLOCKS * SB_SLOT_ROWS, LANES), BF16),
                        pltpu.VMEM((SB_QBLOCKS * SB_SLOT_ROWS, LANES), F32),
                        pltpu.VMEM((SB_QBLOCKS * SB_SLOT_ROWS, LANES), F32)],
        compiler_params=_cparams(3),
        name="stick_breaking",
    )(proj, proj, proj, _suffix_matrix())


HYB_PLAN = (("ret_rot", 1.0), ("ret_rot", HEAD_DIM ** -0.5), ("plain", 1.0), ("plain", 1.0),
            ("dil_rot", HEAD_DIM ** -0.5 * LOG2E), ("dil_rot", 1.0), ("plain", 1.0))
SB_PLAN = (("plain", HEAD_DIM ** -0.5),) * 2 + (("plain", 1.0),) * 4


def kernel(x, ffn1_norm, ffn1_w_in, ffn1_w_out, mix_norm, ffn2_norm, ffn2_w_in, ffn2_w_out,
           hyb_w_in, ret_gn, hyb_w_out, sb_w_in, sb_w_out, final_norm):
    batch, seq, d_model = x.shape
    depth = ffn1_norm.shape[0]
    assert d_model == D_MODEL and x.dtype == F32
    assert seq % max(TOK_TILE, FFN_TILE, RET_TOK, DIL_SPAN, SB_QBLOCKS * SB_TQ) == 0
    tables = (_rotary_tables(seq, HEAD_DIM, RET_ROPE_THETA)
              + _rotary_tables(seq, ROPE_DIM, ROPE_THETA))
    (ffn1_w_in, ffn1_w_out, ffn2_w_in, ffn2_w_out, hyb_w_in, hyb_w_out, sb_w_in, sb_w_out) = (
        w.astype(BF16) for w in (ffn1_w_in, ffn1_w_out, ffn2_w_in, ffn2_w_out,
                                 hyb_w_in, hyb_w_out, sb_w_in, sb_w_out))
    h = x.reshape(batch * seq, D_MODEL)
    for layer in range(depth):
        h = _ffn(h, ffn1_norm[layer], ffn1_w_in, ffn1_w_out, layer)
        if layer % 2 == 0:
            e = layer // 2
            proj = _proj(h, mix_norm[layer], hyb_w_in, e, HYB_PLAN, tables, seq)
            mix = ([_retention(proj, ret_gn[e], batch, seq),
                    _dilated_attention(proj, batch, seq)], hyb_w_out, e)
        else:
            o = layer // 2
            proj = _proj(h, mix_norm[layer], sb_w_in, o, SB_PLAN, (), seq)
            mix = ([_stick_breaking(proj, batch, seq)], sb_w_out, o)
        h = _ffn(h, ffn2_norm[layer], ffn2_w_in, ffn2_w_out, layer, mix,
                 final_norm if layer == depth - 1 else None)
    return h.reshape(batch, seq, D_MODEL)
```

```python
import functools

import numpy as np
import jax
import jax.numpy as jnp
from jax import lax
from jax.experimental import pallas as pl
from jax.experimental.pallas import tpu as pltpu

F32 = jnp.float32
BF16 = jnp.bfloat16

D_MODEL = 1024
HEAD_DIM = 64
RET_HEADS = 8
DIL_HEADS = 8
SB_HEADS = 16
RET_WIDTH = RET_HEADS * HEAD_DIM
DIL_WIDTH = DIL_HEADS * HEAD_DIM
SB_WIDTH = SB_HEADS * HEAD_DIM
D_FF = 2816
BLOCK = 128
RET_CHUNK = 256
RET_ROPE_THETA = 10000.0
ROPE_THETA = 500000.0
ROPE_DIM = HEAD_DIM // 4
DILATIONS = (1, 4, 16)
NORM_EPS = 1e-6
GN_EPS = 1e-5

LANES = 128
PAIR = LANES // HEAD_DIM
VMEM_LIMIT = 56 * 1024 * 1024
NEG_BIG = -1e30
LOG2E = 1.4426950408889634

TOK_TILE = 1024
FFN_TILE = 1024
FF_CHUNK = 256
SB_TQ = 256
SB_TK = 256


def _cparams(n_axes, fusible=None):
    return pltpu.CompilerParams(
        dimension_semantics=("arbitrary",) * n_axes,
        vmem_limit_bytes=VMEM_LIMIT, allow_input_fusion=fusible)


def _resident(shape, layer=None):
    if layer is None:
        return pl.BlockSpec(shape, lambda *_: (0,) * len(shape), pipeline_mode=pl.Buffered(1))
    return pl.BlockSpec((None,) + tuple(shape), lambda *_: (layer,) + (0,) * len(shape),
                        pipeline_mode=pl.Buffered(1))


def _rms(x, g):
    return x * lax.rsqrt(jnp.mean(x * x, axis=-1, keepdims=True) + NORM_EPS) * g


def _silu(x):
    return x / (1.0 + jnp.exp(-x))


def _lane_is_head0(shape):
    return lax.broadcasted_iota(jnp.int32, shape, len(shape) - 1) < HEAD_DIM


def _stack_heads(q):
    h0 = _lane_is_head0(q.shape)
    zero = jnp.zeros_like(q)
    return jnp.concatenate([jnp.where(h0, q, zero), jnp.where(h0, zero, q)], axis=0)


def _unstack_heads(o2):
    t = o2.shape[0] // 2
    return jnp.where(_lane_is_head0((t, LANES)), o2[:t], o2[t:])


def _ffn_body(*refs, n_slabs, final_norm):
    refs = list(refs)
    x_ref = refs.pop(0)
    slabs = [refs.pop(0) for _ in range(n_slabs)]
    wmix_ref = refs.pop(0) if n_slabs else None
    g_ref, win_ref, wout_ref = refs.pop(0), refs.pop(0), refs.pop(0)
    fg_ref = refs.pop(0) if final_norm else None
    o_ref, act_ref = refs

    x = x_ref[...]
    lo = 0
    for a_ref in slabs:
        width = a_ref.shape[1]
        x = x + jnp.dot(a_ref[...], wmix_ref[lo:lo + width, :], preferred_element_type=F32)
        lo += width
    hb = _rms(x, g_ref[...]).astype(BF16)
    for c in range(D_FF // FF_CHUNK):
        cols = slice(c * FF_CHUNK, (c + 1) * FF_CHUNK)
        up_cols = slice(D_FF + c * FF_CHUNK, D_FF + (c + 1) * FF_CHUNK)
        gate = jnp.dot(hb, win_ref[:, cols], preferred_element_type=F32)
        up = jnp.dot(hb, win_ref[:, up_cols], preferred_element_type=F32)
        act_ref[:, cols] = (_silu(gate) * up).astype(BF16)
    y = x + 0.5 * jnp.dot(act_ref[...], wout_ref[...], preferred_element_type=F32)
    if final_norm:
        y = _rms(y, fg_ref[...])
    o_ref[...] = y


def _ffn(x, g, w_in, w_out, layer, mix=None, final_g=None):
    n = x.shape[0]
    row = pl.BlockSpec((FFN_TILE, D_MODEL), lambda i: (i, 0))
    args, specs = [x], [row]
    slabs = ()
    if mix is not None:
        slabs, w_mix, mix_layer = mix
        args += [*slabs, w_mix]
        specs += [pl.BlockSpec((FFN_TILE, a.shape[1]), lambda i: (i, 0)) for a in slabs]
        specs += [_resident(w_mix.shape[1:], mix_layer)]
    args += [g.reshape(1, D_MODEL), w_in, w_out]
    specs += [_resident((1, D_MODEL)), _resident((D_MODEL, 2 * D_FF), layer),
              _resident((D_FF, D_MODEL), layer)]
    if final_g is not None:
        args.append(final_g.reshape(1, D_MODEL))
        specs.append(_resident((1, D_MODEL)))
    return pl.pallas_call(
        functools.partial(_ffn_body, n_slabs=len(slabs), final_norm=final_g is not None),
        grid=(n // FFN_TILE,),
        in_specs=specs,
        out_specs=row,
        out_shape=jax.ShapeDtypeStruct((n, D_MODEL), F32),
        scratch_shapes=[pltpu.VMEM((FFN_TILE, D_FF), BF16)],
        compiler_params=_cparams(1, [a.dtype == BF16 and a.ndim == 3 for a in args]),
        name="ffn",
    )(*args)


PROJ_GROUP = 512


def _rotate(y, cos, sin_a, sin_b, half):
    return (y * cos + pltpu.roll(y, LANES - half, 1) * sin_a + pltpu.roll(y, half, 1) * sin_b)


def _proj_body(x_ref, g_ref, w_ref, *rest, plan):
    tables, o_ref = rest[:-1], rest[-1]
    hb = _rms(x_ref[...], g_ref[...]).astype(BF16)
    for gi, (kind, scale) in enumerate(plan):
        lo = gi * PROJ_GROUP
        y = jnp.dot(hb, w_ref[:, lo:lo + PROJ_GROUP], preferred_element_type=F32)
        for s in range(PROJ_GROUP // LANES):
            ys = y[:, s * LANES:(s + 1) * LANES]
            if kind == "ret_rot":
                ys = _rotate(ys, tables[0][...], tables[1][...], tables[2][...], HEAD_DIM // 2)
            elif kind == "dil_rot":
                ys = _rotate(ys, tables[3][...], tables[4][...], tables[5][...], ROPE_DIM // 2)
            if scale != 1.0:
                ys = ys * scale
            o_ref[:, lo + s * LANES:lo + (s + 1) * LANES] = ys.astype(BF16)


def _proj(x, g, w, layer, plan, tables, seq):
    n = x.shape[0]
    width = w.shape[2]
    row = pl.BlockSpec((TOK_TILE, D_MODEL), lambda i: (i, 0))
    per_seq = seq // TOK_TILE
    tab = pl.BlockSpec((TOK_TILE, LANES), lambda i: (i % per_seq, 0))
    return pl.pallas_call(
        functools.partial(_proj_body, plan=plan),
        grid=(n // TOK_TILE,),
        in_specs=[row, _resident((1, D_MODEL)), _resident((D_MODEL, width), layer)]
        + [tab] * len(tables),
        out_specs=pl.BlockSpec((TOK_TILE, width), lambda i: (i, 0)),
        out_shape=jax.ShapeDtypeStruct((n, width), BF16),
        compiler_params=_cparams(1, [False, False, True] + [False] * len(tables)),
        name="proj",
    )(x, g.reshape(1, D_MODEL), w, *tables)


def _rotary_tables(seq, rot_dim, theta):
    half = rot_dim // 2
    lane = np.arange(LANES) % HEAD_DIM
    inv_freq = 1.0 / (theta ** (np.arange(half, dtype=np.float64) / half))
    ang = (np.arange(seq, dtype=np.float64)[:, None] * inv_freq[None, :])[:, lane % half]
    cos, sin = np.cos(ang), np.sin(ang)
    is_x1 = (lane < half)[None, :]
    is_x2 = ((lane >= half) & (lane < rot_dim))[None, :]
    cos_t = np.where(is_x1 | is_x2, cos, 1.0)
    sin_a = np.where(is_x1, -sin, 0.0)
    sin_b = np.where(is_x2, sin, 0.0)
    return tuple(jnp.asarray(t, dtype=F32) for t in (cos_t, sin_a, sin_b))


RET_TOK = 1024


def _retention_consts():
    h = np.arange(RET_HEADS, dtype=np.float64)
    log_g = np.log(1.0 - 2.0 ** (-5.0 - h))
    i = np.arange(RET_CHUNK, dtype=np.float64)
    diff = i[:, None] - i[None, :]
    decay_in = np.where(diff >= 0, np.exp(np.maximum(diff, 0.0)[None] * log_g[:, None, None]), 0.0)
    n_pairs = RET_HEADS // PAIR
    decay = decay_in.reshape(n_pairs, PAIR * RET_CHUNK, RET_CHUNK)
    lane_head = np.arange(LANES) // HEAD_DIM
    head_of = (np.arange(n_pairs)[:, None] * PAIR + lane_head[None, :])
    lg = log_g[head_of]
    q_dec = np.exp((i + 1.0)[None, :, None] * lg[:, None, :])
    k_dec = np.exp((RET_CHUNK - 1.0 - i)[None, :, None] * lg[:, None, :])
    same_head = (lane_head[:, None] == lane_head[None, :])
    c_dec = np.exp(RET_CHUNK * lg)[:, :, None] * same_head[None]
    bd = np.broadcast_to(same_head[None], c_dec.shape)
    return tuple(jnp.asarray(t, dtype=F32) for t in (decay, q_dec, k_dec, c_dec, bd))


def _retention_chunk(q, k, v, g, state, dec, q_dec, k_dec, c_dec, bd, gn):
    h0 = _lane_is_head0((RET_CHUNK, LANES))
    inv = 1.0 / HEAD_DIM
    sc = lax.dot_general(_stack_heads(q), k, (((1,), (1,)), ((), ())),
                         preferred_element_type=F32) * dec
    inner = _unstack_heads(jnp.dot(sc.astype(BF16), v, preferred_element_type=F32))
    qd = (q.astype(F32) * q_dec).astype(BF16)
    o = inner + jnp.dot(qd, state.astype(BF16), preferred_element_type=F32)
    kd = (k.astype(F32) * k_dec).astype(BF16)
    kv = lax.dot_general(kd, v, (((0,), (0,)), ((), ())), preferred_element_type=F32)
    state = state * c_dec + kv * bd
    s0 = jnp.sum(jnp.where(h0, o, 0.0), axis=-1, keepdims=True)
    s1 = jnp.sum(jnp.where(h0, 0.0, o), axis=-1, keepdims=True)
    d = o - jnp.where(h0, s0, s1) * inv
    d2 = d * d
    v0 = jnp.sum(jnp.where(h0, d2, 0.0), axis=-1, keepdims=True)
    v1 = jnp.sum(jnp.where(h0, 0.0, d2), axis=-1, keepdims=True)
    y = d * lax.rsqrt(jnp.where(h0, v0, v1) * inv + GN_EPS) * gn
    return (y * _silu(g.astype(F32))).astype(BF16), state


def _retention_body(q_ref, k_ref, v_ref, g_ref, dec_ref, qd_ref, kd_ref, cd_ref, bd_ref, gn_ref,
                    o_ref, state_ref):
    @pl.when(pl.program_id(1) == 0)
    def _():
        state_ref[...] = jnp.zeros_like(state_ref)

    for p in range(RET_HEADS // PAIR):
        cols = slice(p * LANES, (p + 1) * LANES)
        state = state_ref[p]
        for c in range(RET_TOK // RET_CHUNK):
            rows = slice(c * RET_CHUNK, (c + 1) * RET_CHUNK)
            o_ref[rows, cols], state = _retention_chunk(
                q_ref[rows, cols], k_ref[rows, cols], v_ref[rows, cols], g_ref[rows, cols], state,
                dec_ref[p], qd_ref[p], kd_ref[p], cd_ref[p], bd_ref[p], gn_ref[:, cols])
        state_ref[p] = state


def _retention(proj, ret_gn, batch, seq):
    n = proj.shape[0]
    per_seq = seq // RET_TOK
    consts = _retention_consts()

    def slab(j):
        return pl.BlockSpec((RET_TOK, RET_WIDTH), lambda b, i, j=j: (b * per_seq + i, j))

    return pl.pallas_call(
        _retention_body,
        grid=(batch, per_seq),
        in_specs=[slab(0), slab(1), slab(2), slab(3)]
        + [_resident(c.shape) for c in consts] + [_resident((1, RET_WIDTH))],
        out_specs=pl.BlockSpec((RET_TOK, RET_WIDTH), lambda b, i: (b * per_seq + i, 0)),
        out_shape=jax.ShapeDtypeStruct((n, RET_WIDTH), BF16),
        scratch_shapes=[pltpu.VMEM((RET_HEADS // PAIR, LANES, LANES), F32)],
        compiler_params=_cparams(2),
        name="retention",
    )(proj, proj, proj, proj, *consts, ret_gn.reshape(1, RET_WIDTH))


DIL_SPAN = BLOCK * max(DILATIONS)


def _band_bias():
    qi = np.arange(PAIR * BLOCK)[:, None] % BLOCK
    kj = np.arange(2 * BLOCK)[None, :]
    return jnp.asarray(np.where((kj >= qi) & (kj <= qi + BLOCK), 0.0, NEG_BIG), dtype=F32)


def _dilated_body(q_ref, kc_ref, kp_ref, vc_ref, vp_ref, bias_ref, out_ref,
                  qf_ref, kf_ref, vf_ref, num_ref, den_ref, max_ref):
    first = pl.program_id(2) == 0
    qf_ref[...] = q_ref[...].astype(F32)
    kf_ref[0:DIL_SPAN, :] = kp_ref[...].astype(F32)
    kf_ref[DIL_SPAN:, :] = kc_ref[...].astype(F32)
    vf_ref[0:DIL_SPAN, :] = vp_ref[...].astype(F32)
    vf_ref[DIL_SPAN:, :] = vc_ref[...].astype(F32)
    k_min = jnp.where(first, BLOCK, 0)
    kj = lax.broadcasted_iota(jnp.int32, (PAIR * BLOCK, 2 * BLOCK), 1)
    ones = jnp.ones((2 * BLOCK, LANES), BF16)
    head0 = _lane_is_head0((BLOCK, LANES))

    for pi, dil in enumerate(DILATIONS):
        for r in range(dil):
            for n in range(DIL_SPAN // (BLOCK * dil)):
                q0 = n * (BLOCK * dil) + r
                k0 = q0 + DIL_SPAN - BLOCK * dil
                q = qf_ref[pl.ds(q0, BLOCK, stride=dil), :].astype(BF16)
                keys = kf_ref[pl.ds(k0, 2 * BLOCK, stride=dil), :].astype(BF16)
                vals = vf_ref[pl.ds(k0, 2 * BLOCK, stride=dil), :].astype(BF16)
                s = lax.dot_general(_stack_heads(q), keys, (((1,), (1,)), ((), ())),
                                    preferred_element_type=F32) + bias_ref[...]
                if n == 0:
                    s = jnp.where(kj >= k_min, s, NEG_BIG)
                m = jnp.max(s, axis=-1, keepdims=True)
                p = jnp.exp2(s - m)
                pv = jnp.dot(p.astype(BF16), jnp.concatenate([vals, ones], axis=1),
                             preferred_element_type=F32)
                rows = pl.ds(q0, BLOCK, stride=dil)
                num_ref[pi, rows, :] = _unstack_heads(pv[:, :LANES])
                den_ref[pi, rows, :] = _unstack_heads(pv[:, LANES:])
                max_ref[pi, rows, :] = jnp.where(head0, m[:BLOCK], m[BLOCK:])

    patterns = range(len(DILATIONS))
    m = functools.reduce(jnp.maximum, [max_ref[pi] for pi in patterns])
    e = [jnp.exp2(max_ref[pi] - m) for pi in patterns]
    num = functools.reduce(lambda a, b: a + b, [e[pi] * num_ref[pi] for pi in patterns])
    den = functools.reduce(lambda a, b: a + b, [e[pi] * den_ref[pi] for pi in patterns])
    out_ref[...] = (num / den).astype(BF16)


def _dilated_attention(proj, batch, seq):
    n = proj.shape[0]
    q0, k0, v0 = ((4 * RET_WIDTH + t * DIL_WIDTH) // LANES for t in range(3))
    per_seq = seq // DIL_SPAN
    n_pairs = DIL_HEADS // PAIR

    def cur(c0):
        return pl.BlockSpec((DIL_SPAN, LANES), lambda b, p, i: (b * per_seq + i, c0 + p))

    def prev(c0):
        return pl.BlockSpec((DIL_SPAN, LANES),
                            lambda b, p, i: (jnp.maximum(b * per_seq + i - 1, 0), c0 + p))

    return pl.pallas_call(
        _dilated_body,
        grid=(batch, n_pairs, per_seq),
        in_specs=[cur(q0), cur(k0), prev(k0), cur(v0), prev(v0),
                  _resident((PAIR * BLOCK, 2 * BLOCK))],
        out_specs=pl.BlockSpec((DIL_SPAN, LANES), lambda b, p, i: (b * per_seq + i, p)),
        out_shape=jax.ShapeDtypeStruct((n, DIL_WIDTH), BF16),
        scratch_shapes=[pltpu.VMEM((DIL_SPAN, LANES), F32),
                        pltpu.VMEM((2 * DIL_SPAN, LANES), F32),
                        pltpu.VMEM((2 * DIL_SPAN, LANES), F32),
                        pltpu.VMEM((len(DILATIONS), DIL_SPAN, LANES), F32),
                        pltpu.VMEM((len(DILATIONS), DIL_SPAN, LANES), F32),
                        pltpu.VMEM((len(DILATIONS), DIL_SPAN, LANES), F32)],
        compiler_params=_cparams(3),
        name="dilated",
    )(proj, proj, proj, proj, proj, _band_bias())


SB_ROWS = 512
SB_PAIRS = 4
SB_QBLOCKS = 2
SB_SLOT_ROWS = SB_PAIRS * PAIR * SB_TQ
SB_DEAD = -105.0


def _suffix_matrix():
    return jnp.asarray(np.arange(SB_TK)[:, None] > np.arange(SB_TK)[None, :], dtype=BF16)


def _sb_visit(qs_ref, k_ref, v_ref, tri_ref, acc_ref, carry_ref, slot, first_kb, n_blk, diagonal):
    start = pl.multiple_of(first_kb * SB_TK, SB_TK)
    width = n_blk * SB_TK
    tri = tri_ref[...]
    for rc in range(SB_SLOT_ROWS // SB_ROWS):
        rows = slice(slot * SB_SLOT_ROWS + rc * SB_ROWS, slot * SB_SLOT_ROWS + (rc + 1) * SB_ROWS)
        pair = (rc * SB_ROWS) // (PAIR * SB_TQ)
        kwin = k_ref[pl.ds(start, width), pair * LANES:(pair + 1) * LANES]
        vwin = v_ref[pl.ds(start, width), pair * LANES:(pair + 1) * LANES]
        z = lax.dot_general(qs_ref[rows, :], kwin, (((1,), (1,)), ((), ())),
                            preferred_element_type=F32)
        if diagonal:
            own = (SB_ROWS, SB_TK)
            q_row = (rc * SB_ROWS + lax.broadcasted_iota(jnp.int32, own, 0)) & (SB_TQ - 1)
            causal = lax.broadcasted_iota(jnp.int32, own, 1) < q_row
            masked = jnp.where(causal, z[:, width - SB_TK:], NEG_BIG)
            z = masked if width == SB_TK else jnp.concatenate([z[:, :width - SB_TK], masked], axis=1)
        log_go = jnp.minimum(z, 0.0) - jnp.log(1.0 + jnp.exp2(jnp.abs(z) * -LOG2E))
        log_stay = log_go - z
        blocks = [slice(b * SB_TK, (b + 1) * SB_TK) for b in range(n_blk)]
        stay16 = log_stay.astype(BF16)
        within = jnp.dot(jnp.concatenate([stay16[:, c] for c in blocks], axis=0), tri,
                         preferred_element_type=F32)
        after = None if diagonal else carry_ref[rows, :]
        later = [None] * n_blk
        for b in reversed(range(n_blk)):
            later[b] = within[b * SB_ROWS:(b + 1) * SB_ROWS]
            if after is not None:
                later[b] = later[b] + jnp.concatenate([after] * (SB_TK // LANES), axis=1)
            total = jnp.broadcast_to(jnp.sum(log_stay[:, blocks[b]], axis=-1, keepdims=True),
                                     (SB_ROWS, LANES))
            after = total if after is None else after + total
        a = jnp.exp(log_go + jnp.concatenate(later, axis=1))
        pv = jnp.dot(a.astype(BF16), vwin, preferred_element_type=F32)
        if diagonal:
            acc_ref[rows, :] = pv
        else:
            acc_ref[rows, :] += pv
        carry_ref[rows, :] = after


def _sb_body(q_ref, k_ref, v_ref, tri_ref, o_ref, qs_ref, acc_ref, carry_ref):
    step_i = pl.program_id(2)
    stacked = PAIR * SB_TQ
    units = [(slot, slice(slot * SB_TQ, (slot + 1) * SB_TQ), slice(pair * LANES, (pair + 1) * LANES),
              slice(slot * SB_SLOT_ROWS + pair * stacked, slot * SB_SLOT_ROWS + (pair + 1) * stacked))
             for slot in range(SB_QBLOCKS) for pair in range(SB_PAIRS)]
    for _, q_rows, lanes, rows in units:
        qs_ref[rows, :] = _stack_heads(q_ref[q_rows, lanes])
    visit = functools.partial(_sb_visit, qs_ref, k_ref, v_ref, tri_ref, acc_ref, carry_ref)
    q_blk = [step_i * SB_QBLOCKS + slot for slot in range(SB_QBLOCKS)]

    @pl.when(step_i == 0)
    def _():
        visit(0, 0, 1, True)
        for slot in range(1, SB_QBLOCKS):
            visit(slot, slot - 1, 2, True)

    @pl.when(step_i > 0)
    def _():
        for slot in range(SB_QBLOCKS):
            visit(slot, q_blk[slot] - 1, 2, True)

    for slot in range(SB_QBLOCKS):
        slot_rows = slice(slot * SB_SLOT_ROWS, (slot + 1) * SB_SLOT_ROWS)

        def alive(state, slot=slot):
            t, top = state
            return (t <= q_blk[slot]) & (top > SB_DEAD)

        def step(state, slot=slot, slot_rows=slot_rows):
            t, _ = state
            visit(slot, q_blk[slot] - t, 1, False)
            return t + 1, jnp.max(carry_ref[slot_rows, :])

        lax.while_loop(alive, step, (2, jnp.max(carry_ref[slot_rows, :])))
    for _, q_rows, lanes, rows in units:
        o_ref[q_rows, lanes] = _unstack_heads(acc_ref[rows, :]).astype(BF16)


def _stick_breaking(proj, batch, seq):
    n = proj.shape[0]
    groups = SB_HEADS // PAIR // SB_PAIRS
    per_seq = seq // (SB_QBLOCKS * SB_TQ)
    wide = SB_PAIRS * LANES
    qo = lambda b, p, i: (b * per_seq + i, p)
    return pl.pallas_call(
        _sb_body,
        grid=(batch, groups, per_seq),
        in_specs=[pl.BlockSpec((SB_QBLOCKS * SB_TQ, wide), qo),
                  pl.BlockSpec((seq, wide), lambda b, p, i: (b, groups + p)),
                  pl.BlockSpec((seq, wide), lambda b, p, i: (b, 2 * groups + p)),
                  _resident((SB_TK, SB_TK))],
        out_specs=pl.BlockSpec((SB_QBLOCKS * SB_TQ, wide), qo),
        out_shape=jax.ShapeDtypeStruct((n, SB_WIDTH), BF16),
        scratch_shapes=[pltpu.VMEM((SB_QBLOCKS * SB_SLOT_ROWS, LANES), BF16),
                        pltpu.VMEM((SB_QBLOCKS * SB_SLOT_ROWS, LANES), F32),
                        pltpu.VMEM((SB_QBLOCKS * SB_SLOT_ROWS, LANES), F32)],
        compiler_params=_cparams(3),
        name="stick_breaking",
    )(proj, proj, proj, _suffix_matrix())


HYB_PLAN = (("ret_rot", 1.0), ("ret_rot", HEAD_DIM ** -0.5), ("plain", 1.0), ("plain", 1.0),
            ("dil_rot", HEAD_DIM ** -0.5 * LOG2E), ("dil_rot", 1.0), ("plain", 1.0))
SB_PLAN = (("plain", HEAD_DIM ** -0.5),) * 2 + (("plain", 1.0),) * 4


def kernel(x, ffn1_norm, ffn1_w_in, ffn1_w_out, mix_norm, ffn2_norm, ffn2_w_in, ffn2_w_out,
           hyb_w_in, ret_gn, hyb_w_out, sb_w_in, sb_w_out, final_norm):
    batch, seq, d_model = x.shape
    depth = ffn1_norm.shape[0]
    assert d_model == D_MODEL and x.dtype == F32
    assert seq % max(TOK_TILE, FFN_TILE, RET_TOK, DIL_SPAN, SB_QBLOCKS * SB_TQ) == 0
    tables = (_rotary_tables(seq, HEAD_DIM, RET_ROPE_THETA)
              + _rotary_tables(seq, ROPE_DIM, ROPE_THETA))
    (ffn1_w_in, ffn1_w_out, ffn2_w_in, ffn2_w_out, hyb_w_in, hyb_w_out, sb_w_in, sb_w_out) = (
        w.astype(BF16) for w in (ffn1_w_in, ffn1_w_out, ffn2_w_in, ffn2_w_out,
                                 hyb_w_in, hyb_w_out, sb_w_in, sb_w_out))
    h = x.reshape(batch * seq, D_MODEL)
    for layer in range(depth):
        h = _ffn(h, ffn1_norm[layer], ffn1_w_in, ffn1_w_out, layer)
        if layer % 2 == 0:
            e = layer // 2
            proj = _proj(h, mix_norm[layer], hyb_w_in, e, HYB_PLAN, tables, seq)
            mix = ([_retention(proj, ret_gn[e], batch, seq),
                    _dilated_attention(proj, batch, seq)], hyb_w_out, e)
        else:
            o = layer // 2
            proj = _proj(h, mix_norm[layer], sb_w_in, o, SB_PLAN, (), seq)
            mix = ([_stick_breaking(proj, batch, seq)], sb_w_out, o)
        h = _ffn(h, ffn2_norm[layer], ffn2_w_in, ffn2_w_out, layer, mix,
                 final_norm if layer == depth - 1 else None)
    return h.reshape(batch, seq, D_MODEL)
```
